```python
import math
import jax
import jax.numpy as jnp
from jax import lax
import numpy as np

D_MODEL = 1024
BATCH = 2
SEQ = 8192
DEPTH = 4

GRID_W = 64
CTX_LEN = 256
HEAD_DIM = 64
MIX_W = D_MODEL // 2
N_BRANCH = 4
HY_W = MIX_W
HY_ORDER = 2
HY_SHORT = 3
HY_BANDS = 16
HY_EMB = 2 * HY_BANDS + 1
HY_HID = 64
HY_MOD_SHIFT = 0.05
HY_FAST_DECAY = 0.3
HY_SLOW_DECAY = 1.5
HY_TARGET = 1e-2
GDN_H = MIX_W // HEAD_DIM
GDN_CONV = 3
GDN_CHUNK = 64
DIFF_VD = 2 * HEAD_DIM
DIFF_H = MIX_W // DIFF_VD
NA_H = MIX_W // HEAD_DIM
NA_ROWS = 8
NA_COLS = 16
N_EXPERTS = 16
N_GROUPS = 4
TOP_K = 2
EXPERT_FF = D_MODEL // 2
MOE_BLOCK = 128
Q_BLOCK = 128
ROPE_THETA = 10000.0
LN_EPS = 1e-5
RMS_EPS = 1e-6
DN_ALPHA = (2 * DEPTH) ** 0.25
DN_BETA = (8 * DEPTH) ** -0.25
SPLIT_SIZES = (3 * HY_W, 3 * MIX_W, MIX_W, 2 * GDN_H, 2 * GDN_H, 3 * MIX_W, 3 * MIX_W, N_BRANCH * D_MODEL)
SPLIT_IDX = tuple(sum(SPLIT_SIZES[:i + 1]) for i in range(len(SPLIT_SIZES) - 1))
IN_TOTAL = sum(SPLIT_SIZES)

kernel_name = 'hybrid_diffusion_parallel_mixers_grouped_moe'


def layer_norm(x, g=None, b=None):
    xf = x.astype(jnp.float32)
    mu = jnp.mean(xf, -1, keepdims=True)
    var = jnp.mean(jnp.square(xf - mu), -1, keepdims=True)
    y = (xf - mu) * lax.rsqrt(var + LN_EPS)
    if g is not None:
        y = y * g.astype(jnp.float32) + b.astype(jnp.float32)
    return y.astype(x.dtype)


def rms_norm(x, w):
    xf = x.astype(jnp.float32)
    return xf * lax.rsqrt(jnp.mean(jnp.square(xf), -1, keepdims=True) + RMS_EPS) * w.astype(jnp.float32)


def l2_normalize(x):
    return x * lax.rsqrt(jnp.sum(jnp.square(x), -1, keepdims=True) + RMS_EPS)


def modulate(x, shift, scale):
    return layer_norm(x) * (1.0 + scale) + shift


def centred_dwconv(u, w):
    K = w.shape[-1]
    T = u.shape[1]
    up = jnp.pad(u, ((0, 0), (K // 2, K // 2), (0, 0)))
    out = up[:, 0:T, :] * w[:, 0]
    for j in range(1, K):
        out = out + up[:, j:j + T, :] * w[:, j]
    return out


def hyena_filters(L, w1, b1, w2, b2, w3, freq, deltas):
    f32 = jnp.float32
    t = jnp.linspace(0.0, 1.0, L, dtype=f32)[:, None]
    ang = 2.0 * math.pi * jnp.arange(L, dtype=f32)[:, None] / L
    bands = jnp.linspace(1e-4, HY_BANDS - 1, HY_BANDS, dtype=f32)[None, :]
    feats = jnp.concatenate([t, jnp.cos(bands * ang), -jnp.sin(bands * ang)], axis=-1)
    fr = freq.astype(f32)
    h = jnp.sin(fr * (feats @ w1.astype(f32) + b1.astype(f32)))
    h = jnp.sin(fr * (h @ w2.astype(f32) + b2.astype(f32)))
    h = (h @ w3.astype(f32)).reshape(L, 2, HY_ORDER, HY_W)
    window = jnp.exp(-t[:, :, None, None] * jnp.abs(deltas.astype(f32))) + HY_MOD_SHIFT
    h = h * window
    h_fwd, h_bwd = h[:, 0], h[:, 1]
    taps = jnp.concatenate([h_fwd, jnp.zeros_like(h_fwd[:1]), jnp.flip(h_bwd[1:], 0)], axis=0)
    taps = taps / jnp.sum(jnp.abs(taps), axis=0, keepdims=True)
    return jnp.fft.rfft(taps, axis=0)


def hyena_mix(proj, conv_w, skip, filt_f):
    L = proj.shape[1]
    u = centred_dwconv(proj, conv_w).astype(jnp.float32)
    v, x1, x2 = jnp.split(u, 3, axis=-1)
    sk = skip.astype(jnp.float32)
    z = v
    for o, gate in enumerate((x1, x2)):
        zf = jnp.fft.rfft(z, n=2 * L, axis=1)
        y = jnp.fft.irfft(zf * filt_f[:, o], n=2 * L, axis=1)[:, :L]
        z = gate * (y + z * sk[o])
    return z.astype(proj.dtype)


def gated_delta_chunks(q, k, v, g, beta, s0):
    B, T, H, _ = q.shape
    C = GDN_CHUNK
    N = T // C

    def chunks(a):
        return jnp.moveaxis(a.reshape(B, N, C, H, *a.shape[3:]), 3, 1)

    q, k, v, g, beta = chunks(q), chunks(k), chunks(v), chunks(g), chunks(beta)
    gc = jnp.cumsum(g, axis=-1)
    tri = jnp.tril(jnp.ones((C, C), bool))
    strict = jnp.tril(jnp.ones((C, C), bool), -1)
    decay = jnp.exp(jnp.where(tri, gc[..., :, None] - gc[..., None, :], -jnp.inf))
    kb = k * beta[..., None]
    A = jnp.where(strict, jnp.einsum('bhnid,bhnjd->bhnij', kb, k) * decay, 0.0)
    eye = jnp.eye(C, dtype=jnp.float32)
    Tm = lax.linalg.triangular_solve(eye + A, jnp.broadcast_to(eye, A.shape),
                                     left_side=True, lower=True, unit_diagonal=True)
    u_val = jnp.einsum('bhnij,bhnjd->bhnid', Tm, v * beta[..., None])
    w_val = jnp.einsum('bhnij,bhnjd->bhnid', Tm, kb * jnp.exp(gc)[..., None])
    intra = jnp.einsum('bhnid,bhnjd->bhnij', q, k) * decay
    q_dec = q * jnp.exp(gc)[..., None]
    g_last = gc[..., -1]
    k_dec = k * jnp.exp(g_last[..., None] - gc)[..., None]

    def step(S, xs):
        qd, w, u, att, kd, gl = xs
        v_new = u - jnp.einsum('bhcd,bhde->bhce', w, S)
        o = jnp.einsum('bhcd,bhde->bhce', qd, S) + jnp.einsum('bhij,bhje->bhie', att, v_new)
        S = S * jnp.exp(gl)[..., None, None] + jnp.einsum('bhcd,bhce->bhde', kd, v_new)
        return S, o

    xs = tuple(jnp.moveaxis(a, 2, 0) for a in (q_dec, w_val, u_val, intra, k_dec, g_last))
    s_fin, o = lax.scan(step, s0, xs)
    o = jnp.transpose(o, (1, 0, 3, 2, 4)).reshape(B, T, H, -1)
    return o, s_fin


def gdn_mix(qkv, z, a, b, conv_w, a_log, dt_bias, norm_w, init_states, with_output):
    f32 = jnp.float32
    B, T, _ = qkv.shape
    u = jax.nn.silu(centred_dwconv(qkv, conv_w).astype(f32))
    q, k, v = [t.reshape(B, T, GDN_H, HEAD_DIM) for t in jnp.split(u, 3, axis=-1)]
    q = l2_normalize(q) * HEAD_DIM ** -0.5
    k = l2_normalize(k)
    g = -jnp.exp(a_log.astype(f32)) * jax.nn.softplus(a.astype(f32).reshape(B, T, 2, GDN_H) + dt_bias.astype(f32))
    beta = jax.nn.sigmoid(b.astype(f32).reshape(B, T, 2, GDN_H))
    outs, finals = [], []
    for d in range(2):
        rev = (lambda t: jnp.flip(t, 1)) if d == 1 else (lambda t: t)
        o, s_fin = gated_delta_chunks(rev(q), rev(k), rev(v), rev(g[:, :, d]), rev(beta[:, :, d]), init_states[d])
        finals.append(s_fin)
        if with_output:
            outs.append(rev(o))
    if not with_output:
        return None, (finals[0], finals[1])
    o = rms_norm(outs[0] + outs[1], norm_w) * jax.nn.silu(z.astype(f32).reshape(B, T, GDN_H, HEAD_DIM))
    return o.reshape(B, T, GDN_H * HEAD_DIM).astype(qkv.dtype), (finals[0], finals[1])


def axial_rope_angles(n):
    t = jnp.arange(n)
    row = (t // GRID_W).astype(jnp.float32)
    col = (t % GRID_W).astype(jnp.float32)
    nf = HEAD_DIM // 4
    inv = ROPE_THETA ** (-jnp.arange(nf, dtype=jnp.float32) / nf)
    return row[:, None] * inv, col[:, None] * inv


def rope_half(x, ang):
    x1, x2 = jnp.split(x, 2, axis=-1)
    cos, sin = jnp.cos(ang).astype(x.dtype), jnp.sin(ang).astype(x.dtype)
    return jnp.concatenate([x1 * cos - x2 * sin, x1 * sin + x2 * cos], axis=-1)


def axial_rope(x, ang_r, ang_c):
    half = HEAD_DIM // 2
    ar = ang_r[None, :, None, None, :]
    ac = ang_c[None, :, None, None, :]
    return jnp.concatenate([rope_half(x[..., :half], ar), rope_half(x[..., half:], ac)], axis=-1)


def diff_attend(q, k, v, lam):
    s = jnp.einsum('bqhcd,bkhcd->bhcqk', q, k).astype(jnp.float32) * HEAD_DIM ** -0.5
    p = jax.nn.softmax(s, axis=-1)
    a = p[:, :, 0] - lam * p[:, :, 1]
    return jnp.einsum('bhqk,bkhd->bqhd', a.astype(v.dtype), v)


def diff_latent(q, k, v, kc, vc, lam, ang_r, ang_c):
    B, n = q.shape[:2]
    q = axial_rope(q, ang_r, ang_c)
    k_all = jnp.concatenate([axial_rope(k, ang_r, ang_c), kc], axis=1)
    v_all = jnp.concatenate([v, vc], axis=1)
    nb = n // Q_BLOCK
    qb = jnp.swapaxes(q.reshape(B, nb, Q_BLOCK, *q.shape[2:]), 0, 1)
    o = lax.map(lambda qi: diff_attend(qi, k_all, v_all, lam), qb)
    return jnp.swapaxes(o, 0, 1).reshape(B, n, DIFF_H, DIFF_VD)


def diff_finish(o, norm_w, lam_init):
    B, T = o.shape[:2]
    return (rms_norm(o, norm_w) * (1.0 - lam_init)).reshape(B, T, DIFF_H * DIFF_VD).astype(o.dtype)


def split_diff(p):
    B, T, _ = p.shape
    q, k, v = jnp.split(p, 3, axis=-1)
    return (q.reshape(B, T, DIFF_H, 2, HEAD_DIM), k.reshape(B, T, DIFF_H, 2, HEAD_DIM),
            v.reshape(B, T, DIFF_H, DIFF_VD))


def dense_attend(q, k, v):
    s = jnp.einsum('bqhd,bkhd->bhqk', q, k).astype(jnp.float32) * HEAD_DIM ** -0.5
    p = jax.nn.softmax(s, axis=-1)
    return jnp.einsum('bhqk,bkhd->bqhd', p.astype(v.dtype), v)


def na_latent(q, k, v, kc, vc, rpb):
    B, n, H, dh = q.shape
    R = n // GRID_W
    kh, kw = min(NA_ROWS, R), NA_COLS
    grid = lambda t: t.reshape(B, R, GRID_W, H, dh)
    kg, vg = grid(k), grid(v)
    cols = jnp.arange(GRID_W)
    cstart = jnp.clip(cols - kw // 2, 0, GRID_W - kw)
    col_ok = (cols[None, :] >= cstart[:, None]) & (cols[None, :] < cstart[:, None] + kw)
    mask = jnp.broadcast_to(col_ok[:, None, :], (GRID_W, kh, GRID_W)).reshape(GRID_W, kh * GRID_W)
    dc = jnp.clip(cols[None, :] - cols[:, None], -(kw - 1), kw - 1) + (kw - 1)
    rpb_c = rpb.astype(jnp.float32)[:, :, dc]
    scale = dh ** -0.5

    def row(args):
        r, qr = args
        rs = jnp.clip(r - kh // 2, 0, R - kh)
        kr = lax.dynamic_slice_in_dim(kg, rs, kh, axis=1).reshape(B, kh * GRID_W, H, dh)
        vr = lax.dynamic_slice_in_dim(vg, rs, kh, axis=1).reshape(B, kh * GRID_W, H, dh)
        dr = rs + jnp.arange(kh) - r + (NA_ROWS - 1)
        bias = jnp.transpose(rpb_c[:, dr], (0, 2, 1, 3)).reshape(H, GRID_W, kh * GRID_W)
        s_win = jnp.einsum('bqhd,bkhd->bhqk', qr, kr).astype(jnp.float32) * scale + bias[None]
        s_win = jnp.where(mask, s_win, -jnp.inf)
        s_ctx = jnp.einsum('bqhd,bkhd->bhqk', qr, kc).astype(jnp.float32) * scale
        p = jax.nn.softmax(jnp.concatenate([s_win, s_ctx], axis=-1), axis=-1).astype(v.dtype)
        nw = kh * GRID_W
        return (jnp.einsum('bhqk,bkhd->bqhd', p[..., :nw], vr)
                + jnp.einsum('bhqk,bkhd->bqhd', p[..., nw:], vc))

    o = lax.map(row, (jnp.arange(R), jnp.swapaxes(grid(q), 0, 1)))
    return jnp.swapaxes(o, 0, 1).reshape(B, n, H * dh)


def merge_branches(ys, gate_cols, proj, w_o):
    br = jnp.einsum('btmc,mcd->btmd', jnp.stack(ys, axis=2), proj)
    g = jax.nn.sigmoid(gate_cols.reshape(gate_cols.shape[0], gate_cols.shape[1], N_BRANCH, -1))
    return jnp.sum(g * br, axis=2) @ w_o


def moe_ffn(u, router_w, router_b, w1, w3, w2):
    n_tok, d = u.shape
    s = jax.nn.sigmoid((u @ router_w).astype(jnp.float32))
    sel = s + router_b.astype(jnp.float32)
    per = N_EXPERTS // N_GROUPS
    group_score = lax.top_k(sel.reshape(n_tok, N_GROUPS, per), TOP_K)[0].sum(-1)
    best_group = jnp.argmax(group_score, axis=-1)
    in_group = (jnp.arange(N_EXPERTS) // per)[None, :] == best_group[:, None]
    _, idx = lax.top_k(jnp.where(in_group, sel, -jnp.inf), TOP_K)
    wts = jnp.take_along_axis(s, idx, axis=-1)
    wts = wts / jnp.sum(wts, -1, keepdims=True)
    n_slot = n_tok * TOP_K
    e_flat = idx.reshape(-1).astype(jnp.int32)
    t_flat = jnp.repeat(jnp.arange(n_tok, dtype=jnp.int32), TOP_K)
    w_flat = wts.reshape(-1)
    order = jnp.argsort(e_flat)
    e_s, t_s, w_s = e_flat[order], t_flat[order], w_flat[order]
    counts = jax.ops.segment_sum(jnp.ones_like(e_flat), e_flat, num_segments=N_EXPERTS)
    padded = (counts + MOE_BLOCK - 1) // MOE_BLOCK * MOE_BLOCK
    start = jnp.cumsum(counts) - counts
    pend = jnp.cumsum(padded)
    pstart = pend - padded
    dest = pstart[e_s] + jnp.arange(n_slot, dtype=jnp.int32) - start[e_s]
    n_blk = -(-n_slot // MOE_BLOCK) + N_EXPERTS
    cap = n_blk * MOE_BLOCK
    buf_tok = jnp.full((cap,), n_tok, jnp.int32).at[dest].set(t_s)
    buf_w = jnp.zeros((cap,), jnp.float32).at[dest].set(w_s)
    blk_exp = jnp.minimum(jnp.searchsorted(pend, jnp.arange(n_blk, dtype=jnp.int32) * MOE_BLOCK, side='right'),
                          N_EXPERTS - 1)
    u_pad = jnp.concatenate([u, jnp.zeros((1, d), u.dtype)], axis=0)
    xb = u_pad[buf_tok].reshape(n_blk, MOE_BLOCK, d)

    def expert_block(args):
        xi, e = args
        return (jax.nn.silu(xi @ w1[e]) * (xi @ w3[e])) @ w2[e]

    yb = lax.map(expert_block, (xb, blk_exp)).reshape(cap, d)
    y = jax.ops.segment_sum(yb * buf_w[:, None].astype(yb.dtype), buf_tok, num_segments=n_tok + 1)
    return y[:n_tok]


def setup_inputs(seed: int = 0) -> dict:
    key = jax.random.key(seed)
    ks = iter(jax.random.split(key, 48))
    f32 = jnp.float32
    nrm = lambda shape, s: jax.random.normal(next(ks), shape, f32) * s
    D = D_MODEL
    max_decay = math.log(HY_TARGET) / HY_FAST_DECAY
    min_decay = math.log(HY_TARGET) / HY_SLOW_DECAY
    base_decay = jnp.linspace(min_decay, max_decay, HY_W, dtype=f32)
    dt = jnp.exp(jax.random.uniform(next(ks), (DEPTH, 2, GDN_H), f32, math.log(1e-3), math.log(1e-1)))
    return {
        'x': nrm((BATCH, SEQ, D), 1.0),
        'c': nrm((BATCH, D), 1.0),
        'ctx': nrm((BATCH, CTX_LEN, D), 1.0),
        'c_ctx': nrm((D,), 1.0),
        'w_mod': nrm((DEPTH, D, 6 * D), D ** -0.5),
        'b_mod': nrm((DEPTH, 6 * D), 0.02),
        'w_in': nrm((DEPTH, D, IN_TOTAL), D ** -0.5),
        'hy_conv': nrm((DEPTH, 3 * HY_W, HY_SHORT), HY_SHORT ** -0.5),
        'hy_w1': nrm((DEPTH, HY_EMB, HY_HID), HY_EMB ** -0.5),
        'hy_b1': nrm((DEPTH, HY_HID), 0.02),
        'hy_w2': nrm((DEPTH, HY_HID, HY_HID), HY_HID ** -0.5),
        'hy_b2': nrm((DEPTH, HY_HID), 0.02),
        'hy_w3': nrm((DEPTH, HY_HID, 2 * HY_ORDER * HY_W), HY_HID ** -0.5),
        'hy_freq': 1.0 + nrm((DEPTH, HY_HID), 0.02),
        'hy_deltas': base_decay * (1.0 + nrm((DEPTH, 2, HY_ORDER, HY_W), 0.02)),
        'hy_skip': nrm((DEPTH, HY_ORDER, HY_W), 0.5),
        'gdn_conv': nrm((DEPTH, 3 * MIX_W, GDN_CONV), GDN_CONV ** -0.5),
        'gdn_a_log': jnp.log(jax.random.uniform(next(ks), (DEPTH, 2, GDN_H), f32, 1.0, 16.0)),
        'gdn_dt_bias': dt + jnp.log(-jnp.expm1(-dt)),
        'gdn_norm': 1.0 + nrm((DEPTH, HEAD_DIM), 0.02),
        'diff_lam': nrm((DEPTH, 4, HEAD_DIM), 0.1),
        'diff_norm': 1.0 + nrm((DEPTH, DIFF_VD), 0.02),
        'na_rpb': nrm((DEPTH, NA_H, 2 * NA_ROWS - 1, 2 * NA_COLS - 1), 0.02),
        'branch_proj': nrm((DEPTH, N_BRANCH, MIX_W, D), MIX_W ** -0.5 * DN_BETA),
        'w_out': nrm((DEPTH, D, D), D ** -0.5 * DN_BETA),
        'ln_g': 1.0 + nrm((DEPTH, 2, D), 0.02),
        'ln_b': nrm((DEPTH, 2, D), 0.02),
        'router_w': nrm((D, N_EXPERTS), D ** -0.5),
        'router_b': nrm((N_EXPERTS,), 0.01),
        'exp_w1': nrm((DEPTH, N_EXPERTS, D, EXPERT_FF), D ** -0.5),
        'exp_w3': nrm((DEPTH, N_EXPERTS, D, EXPERT_FF), D ** -0.5),
        'exp_w2': nrm((DEPTH, N_EXPERTS, EXPERT_FF, D), EXPERT_FF ** -0.5 * DN_BETA),
    }


def reference(x, c, ctx, c_ctx, w_mod, b_mod, w_in, hy_conv, hy_w1, hy_b1, hy_w2, hy_b2, hy_w3, hy_freq,
              hy_deltas, hy_skip, gdn_conv, gdn_a_log, gdn_dt_bias, gdn_norm, diff_lam, diff_norm, na_rpb,
              branch_proj, w_out, ln_g, ln_b, router_w, router_b, exp_w1, exp_w3, exp_w2):
    B, n, D = x.shape
    m = ctx.shape[1]
    ang_r, ang_c = axial_rope_angles(n)
    zero_state = jnp.zeros((B, GDN_H, HEAD_DIM, HEAD_DIM), jnp.float32)
    xc = ctx
    for l in range(DEPTH):
        ctx_out = l < DEPTH - 1
        mx = jnp.split((jax.nn.silu(c) @ w_mod[l] + b_mod[l])[:, None, :], 6, axis=-1)
        mc = jnp.split((jax.nn.silu(c_ctx) @ w_mod[l] + b_mod[l])[None, None, :], 6, axis=-1)

        hx = modulate(x, mx[0], mx[1])
        hc = modulate(xc, mc[0], mc[1])
        hy_x, gqkv_x, gz_x, ga_x, gb_x, dqkv_x, nqkv_x, gate_x = jnp.split(hx @ w_in[l], SPLIT_IDX, axis=-1)
        hy_c, gqkv_c, gz_c, ga_c, gb_c, dqkv_c, nqkv_c, gate_c = jnp.split(hc @ w_in[l], SPLIT_IDX, axis=-1)
        filt = (hy_w1[l], hy_b1[l], hy_w2[l], hy_b2[l], hy_w3[l], hy_freq[l], hy_deltas[l])

        ya_x = hyena_mix(hy_x, hy_conv[l], hy_skip[l], hyena_filters(n, *filt))

        gdn_p = (gdn_conv[l], gdn_a_log[l], gdn_dt_bias[l], gdn_norm[l])
        yb_c, st_c = gdn_mix(gqkv_c, gz_c, ga_c, gb_c, *gdn_p, (zero_state, zero_state), ctx_out)
        yb_x, _ = gdn_mix(gqkv_x, gz_x, ga_x, gb_x, *gdn_p, st_c, True)

        lq1, lk1, lq2, lk2 = diff_lam[l].astype(jnp.float32)
        lam_init = 0.8 - 0.6 * math.exp(-0.3 * l)
        lam = jnp.exp(jnp.sum(lq1 * lk1)) - jnp.exp(jnp.sum(lq2 * lk2)) + lam_init
        dq_x, dk_x, dv_x = split_diff(dqkv_x)
        dq_c, dk_c, dv_c = split_diff(dqkv_c)
        yc_x = diff_finish(diff_latent(dq_x, dk_x, dv_x, dk_c, dv_c, lam, ang_r, ang_c), diff_norm[l], lam_init)

        nq_x, nk_x, nv_x = [t.reshape(B, n, NA_H, HEAD_DIM) for t in jnp.split(nqkv_x, 3, axis=-1)]
        nq_c, nk_c, nv_c = [t.reshape(B, m, NA_H, HEAD_DIM) for t in jnp.split(nqkv_c, 3, axis=-1)]
        yd_x = na_latent(nq_x, nk_x, nv_x, nk_c, nv_c, na_rpb[l])

        mix_x = merge_branches([ya_x, yb_x, yc_x, yd_x], gate_x, branch_proj[l], w_out[l])
        x = layer_norm(DN_ALPHA * x + mx[2] * mix_x, ln_g[l, 0], ln_b[l, 0])
        if ctx_out:
            ya_c = hyena_mix(hy_c, hy_conv[l], hy_skip[l], hyena_filters(m, *filt))
            yc_c = diff_finish(diff_attend(dq_c, dk_c, dv_c, lam), diff_norm[l], lam_init)
            yd_c = dense_attend(nq_c, nk_c, nv_c).reshape(B, m, NA_H * HEAD_DIM)
            mix_c = merge_branches([ya_c, yb_c, yc_c, yd_c], gate_c, branch_proj[l], w_out[l])
            xc = layer_norm(DN_ALPHA * xc + mc[2] * mix_c, ln_g[l, 0], ln_b[l, 0])

        h2x = modulate(x, mx[3], mx[4]).reshape(B * n, D)
        if ctx_out:
            h2c = modulate(xc, mc[3], mc[4]).reshape(B * m, D)
            y_all = moe_ffn(jnp.concatenate([h2x, h2c], axis=0), router_w, router_b, exp_w1[l], exp_w3[l], exp_w2[l])
            y_x = y_all[:B * n].reshape(B, n, D)
            y_c = y_all[B * n:].reshape(B, m, D)
            xc = layer_norm(DN_ALPHA * xc + mc[5] * y_c, ln_g[l, 1], ln_b[l, 1])
        else:
            y_x = moe_ffn(h2x, router_w, router_b, exp_w1[l], exp_w3[l], exp_w2[l]).reshape(B, n, D)
        x = layer_norm(DN_ALPHA * x + mx[5] * y_x, ln_g[l, 1], ln_b[l, 1])
    return x
```

```python
import functools
import math

import jax
import jax.numpy as jnp
from jax import lax
from jax.experimental import pallas as pl
from jax.experimental.pallas import tpu as pltpu

D_MODEL = 1024
GRID_W = 64
HEAD_DIM = 64
MIX_W = D_MODEL // 2
N_BRANCH = 4
HY_W = MIX_W
HY_ORDER = 2
HY_BANDS = 16
HY_MOD_SHIFT = 0.05
GDN_H = MIX_W // HEAD_DIM
GDN_CHUNK = 64
DIFF_VD = 2 * HEAD_DIM
DIFF_H = MIX_W // DIFF_VD
NA_H = MIX_W // HEAD_DIM
NA_ROWS = 8
NA_COLS = 16
N_EXPERTS = 16
N_GROUPS = 4
TOP_K = 2
MOE_BLOCK = 128
Q_BLOCK = 128
ROPE_THETA = 10000.0
LN_EPS = 1e-5
RMS_EPS = 1e-6
SPLIT_SIZES = (3 * HY_W, 3 * MIX_W, MIX_W, 2 * GDN_H, 2 * GDN_H, 3 * MIX_W, 3 * MIX_W, N_BRANCH * D_MODEL)
SPLIT_IDX = tuple(sum(SPLIT_SIZES[:i + 1]) for i in range(len(SPLIT_SIZES) - 1))

LANE = 128
ROW_TILE = 512
VMEM_LIMIT = 48 * 1024 * 1024


def _ln_mod_matmul_kernel(x_ref, mod_ref, w_ref, o_ref, h_ref):
    @pl.when(pl.program_id(1) == 0)
    def _():
        x = x_ref[...]
        mu = jnp.mean(x, axis=-1, keepdims=True)
        xc = x - mu
        var = jnp.mean(xc * xc, axis=-1, keepdims=True)
        y = xc * lax.rsqrt(var + LN_EPS)
        h_ref[...] = (y * mod_ref[0, 1:2, :] + mod_ref[0, 0:1, :]).astype(h_ref.dtype)

    o_ref[...] = jnp.dot(h_ref[...], w_ref[...], preferred_element_type=jnp.float32).astype(o_ref.dtype)


def ln_mod_matmul(x, mod, w, tn, out_dtype=jnp.float32):
    T, D = x.shape
    N = w.shape[1]
    return pl.pallas_call(
        _ln_mod_matmul_kernel,
        grid=(T // ROW_TILE, N // tn),
        in_specs=[
            pl.BlockSpec((ROW_TILE, D), lambda i, j: (i, 0)),
            pl.BlockSpec((1, 2, D), lambda i, j: (i, 0, 0)),
            pl.BlockSpec((D, tn), lambda i, j: (0, j)),
        ],
        out_specs=pl.BlockSpec((ROW_TILE, tn), lambda i, j: (i, j)),
        out_shape=jax.ShapeDtypeStruct((T, N), out_dtype),
        scratch_shapes=[pltpu.VMEM((ROW_TILE, D), jnp.bfloat16)],
        compiler_params=pltpu.CompilerParams(
            dimension_semantics=("arbitrary", "arbitrary"), vmem_limit_bytes=VMEM_LIMIT),
        name="ln_mod_matmul",
    )(x, mod, w)


def layer_norm(x, g=None, b=None):
    mu = jnp.mean(x, -1, keepdims=True)
    var = jnp.mean(jnp.square(x - mu), -1, keepdims=True)
    y = (x - mu) * lax.rsqrt(var + LN_EPS)
    if g is not None:
        y = y * g + b
    return y


def rms_norm(x, w):
    return x * lax.rsqrt(jnp.mean(jnp.square(x), -1, keepdims=True) + RMS_EPS) * w


def l2_normalize(x):
    return x * lax.rsqrt(jnp.sum(jnp.square(x), -1, keepdims=True) + RMS_EPS)


def modulate(x, shift, scale):
    return layer_norm(x) * (1.0 + scale) + shift


def centred_dwconv(u, w):
    K = w.shape[-1]
    T = u.shape[1]
    up = jnp.pad(u, ((0, 0), (K // 2, K // 2), (0, 0)))
    out = up[:, 0:T, :] * w[:, 0]
    for j in range(1, K):
        out = out + up[:, j:j + T, :] * w[:, j]
    return out


def hyena_filters(L, w1, b1, w2, b2, w3, freq, deltas):
    f32 = jnp.float32
    t = jnp.linspace(0.0, 1.0, L, dtype=f32)[:, None]
    ang = 2.0 * math.pi * jnp.arange(L, dtype=f32)[:, None] / L
    bands = jnp.linspace(1e-4, HY_BANDS - 1, HY_BANDS, dtype=f32)[None, :]
    feats = jnp.concatenate([t, jnp.cos(bands * ang), -jnp.sin(bands * ang)], axis=-1)
    h = jnp.sin(freq * (feats @ w1 + b1))
    h = jnp.sin(freq * (h @ w2 + b2))
    h = (h @ w3).reshape(L, 2, HY_ORDER, HY_W)
    window = jnp.exp(-t[:, :, None, None] * jnp.abs(deltas)) + HY_MOD_SHIFT
    h = h * window
    h_fwd, h_bwd = h[:, 0], h[:, 1]
    taps = jnp.concatenate([h_fwd, jnp.zeros_like(h_fwd[:1]), jnp.flip(h_bwd[1:], 0)], axis=0)
    taps = taps / jnp.sum(jnp.abs(taps), axis=0, keepdims=True)
    return jnp.fft.rfft(taps, axis=0)


def hyena_mix(proj, conv_w, skip, filt_f):
    L = proj.shape[1]
    u = centred_dwconv(proj, conv_w)
    v, x1, x2 = jnp.split(u, 3, axis=-1)
    z = v
    for o, gate in enumerate((x1, x2)):
        zf = jnp.fft.rfft(z, n=2 * L, axis=1)
        y = jnp.fft.irfft(zf * filt_f[:, o], n=2 * L, axis=1)[:, :L]
        z = gate * (y + z * skip[o])
    return z


def gated_delta_chunks(q, k, v, g, beta, s0):
    B, T, H, _ = q.shape
    C = GDN_CHUNK
    N = T // C

    def chunks(a):
        return jnp.moveaxis(a.reshape(B, N, C, H, *a.shape[3:]), 3, 1)

    q, k, v, g, beta = chunks(q), chunks(k), chunks(v), chunks(g), chunks(beta)
    gc = jnp.cumsum(g, axis=-1)
    tri = jnp.tril(jnp.ones((C, C), bool))
    strict = jnp.tril(jnp.ones((C, C), bool), -1)
    decay = jnp.exp(jnp.where(tri, gc[..., :, None] - gc[..., None, :], -jnp.inf))
    kb = k * beta[..., None]
    A = jnp.where(strict, jnp.einsum('bhnid,bhnjd->bhnij', kb, k) * decay, 0.0)
    eye = jnp.eye(C, dtype=jnp.float32)
    Tm = lax.linalg.triangular_solve(eye + A, jnp.broadcast_to(eye, A.shape),
                                     left_side=True, lower=True, unit_diagonal=True)
    u_val = jnp.einsum('bhnij,bhnjd->bhnid', Tm, v * beta[..., None])
    w_val = jnp.einsum('bhnij,bhnjd->bhnid', Tm, kb * jnp.exp(gc)[..., None])
    intra = jnp.einsum('bhnid,bhnjd->bhnij', q, k) * decay
    q_dec = q * jnp.exp(gc)[..., None]
    g_last = gc[..., -1]
    k_dec = k * jnp.exp(g_last[..., None] - gc)[..., None]

    def step(S, xs):
        qd, w, u, att, kd, gl = xs
        v_new = u - jnp.einsum('bhcd,bhde->bhce', w, S)
        o = jnp.einsum('bhcd,bhde->bhce', qd, S) + jnp.einsum('bhij,bhje->bhie', att, v_new)
        S = S * jnp.exp(gl)[..., None, None] + jnp.einsum('bhcd,bhce->bhde', kd, v_new)
        return S, o

    xs = tuple(jnp.moveaxis(a, 2, 0) for a in (q_dec, w_val, u_val, intra, k_dec, g_last))
    s_fin, o = lax.scan(step, s0, xs)
    o = jnp.transpose(o, (1, 0, 3, 2, 4)).reshape(B, T, H, -1)
    return o, s_fin


def gdn_mix(qkv, z, a, b, conv_w, a_log, dt_bias, norm_w, init_states, with_output):
    B, T, _ = qkv.shape
    u = jax.nn.silu(centred_dwconv(qkv, conv_w))
    q, k, v = [t.reshape(B, T, GDN_H, HEAD_DIM) for t in jnp.split(u, 3, axis=-1)]
    q = l2_normalize(q) * HEAD_DIM ** -0.5
    k = l2_normalize(k)
    g = -jnp.exp(a_log) * jax.nn.softplus(a.reshape(B, T, 2, GDN_H) + dt_bias)
    beta = jax.nn.sigmoid(b.reshape(B, T, 2, GDN_H))
    outs, finals = [], []
    for d in range(2):
        rev = (lambda t: jnp.flip(t, 1)) if d == 1 else (lambda t: t)
        o, s_fin = gated_delta_chunks(rev(q), rev(k), rev(v), rev(g[:, :, d]), rev(beta[:, :, d]), init_states[d])
        finals.append(s_fin)
        if with_output:
            outs.append(rev(o))
    if not with_output:
        return None, (finals[0], finals[1])
    o = rms_norm(outs[0] + outs[1], norm_w) * jax.nn.silu(z.reshape(B, T, GDN_H, HEAD_DIM))
    return o.reshape(B, T, GDN_H * HEAD_DIM), (finals[0], finals[1])


def axial_rope_angles(n):
    t = jnp.arange(n)
    row = (t // GRID_W).astype(jnp.float32)
    col = (t % GRID_W).astype(jnp.float32)
    nf = HEAD_DIM // 4
    inv = ROPE_THETA ** (-jnp.arange(nf, dtype=jnp.float32) / nf)
    return row[:, None] * inv, col[:, None] * inv


def rope_half(x, ang):
    x1, x2 = jnp.split(x, 2, axis=-1)
    cos, sin = jnp.cos(ang), jnp.sin(ang)
    return jnp.concatenate([x1 * cos - x2 * sin, x1 * sin + x2 * cos], axis=-1)


def axial_rope(x, ang_r, ang_c):
    half = HEAD_DIM // 2
    ar = ang_r[None, :, None, None, :]
    ac = ang_c[None, :, None, None, :]
    return jnp.concatenate([rope_half(x[..., :half], ar), rope_half(x[..., half:], ac)], axis=-1)


def diff_attend(q, k, v, lam):
    s = jnp.einsum('bqhcd,bkhcd->bhcqk', q, k) * HEAD_DIM ** -0.5
    p = jax.nn.softmax(s, axis=-1)
    a = p[:, :, 0] - lam * p[:, :, 1]
    return jnp.einsum('bhqk,bkhd->bqhd', a, v)


def diff_latent(q, k, v, kc, vc, lam, ang_r, ang_c):
    B, n = q.shape[:2]
    q = axial_rope(q, ang_r, ang_c)
    k_all = jnp.concatenate([axial_rope(k, ang_r, ang_c), kc], axis=1)
    v_all = jnp.concatenate([v, vc], axis=1)
    nb = n // Q_BLOCK
    qb = jnp.swapaxes(q.reshape(B, nb, Q_BLOCK, *q.shape[2:]), 0, 1)
    o = lax.map(lambda qi: diff_attend(qi, k_all, v_all, lam), qb)
    return jnp.swapaxes(o, 0, 1).reshape(B, n, DIFF_H, DIFF_VD)


def diff_finish(o, norm_w, lam_init):
    B, T = o.shape[:2]
    return (rms_norm(o, norm_w) * (1.0 - lam_init)).reshape(B, T, DIFF_H * DIFF_VD)


def split_diff(p):
    B, T, _ = p.shape
    q, k, v = jnp.split(p, 3, axis=-1)
    return (q.reshape(B, T, DIFF_H, 2, HEAD_DIM), k.reshape(B, T, DIFF_H, 2, HEAD_DIM),
            v.reshape(B, T, DIFF_H, DIFF_VD))


def dense_attend(q, k, v):
    s = jnp.einsum('bqhd,bkhd->bhqk', q, k) * HEAD_DIM ** -0.5
    p = jax.nn.softmax(s, axis=-1)
    return jnp.einsum('bhqk,bkhd->bqhd', p, v)


def na_latent(q, k, v, kc, vc, rpb):
    B, n, H, dh = q.shape
    R = n // GRID_W
    kh, kw = min(NA_ROWS, R), NA_COLS
    grid = lambda t: t.reshape(B, R, GRID_W, H, dh)
    kg, vg = grid(k), grid(v)
    cols = jnp.arange(GRID_W)
    cstart = jnp.clip(cols - kw // 2, 0, GRID_W - kw)
    col_ok = (cols[None, :] >= cstart[:, None]) & (cols[None, :] < cstart[:, None] + kw)
    mask = jnp.broadcast_to(col_ok[:, None, :], (GRID_W, kh, GRID_W)).reshape(GRID_W, kh * GRID_W)
    dc = jnp.clip(cols[None, :] - cols[:, None], -(kw - 1), kw - 1) + (kw - 1)
    rpb_c = rpb[:, :, dc]
    scale = dh ** -0.5

    def row(args):
        r, qr = args
        rs = jnp.clip(r - kh // 2, 0, R - kh)
        kr = lax.dynamic_slice_in_dim(kg, rs, kh, axis=1).reshape(B, kh * GRID_W, H, dh)
        vr = lax.dynamic_slice_in_dim(vg, rs, kh, axis=1).reshape(B, kh * GRID_W, H, dh)
        dr = rs + jnp.arange(kh) - r + (NA_ROWS - 1)
        bias = jnp.transpose(rpb_c[:, dr], (0, 2, 1, 3)).reshape(H, GRID_W, kh * GRID_W)
        s_win = jnp.einsum('bqhd,bkhd->bhqk', qr, kr) * scale + bias[None]
        s_win = jnp.where(mask, s_win, -jnp.inf)
        s_ctx = jnp.einsum('bqhd,bkhd->bhqk', qr, kc) * scale
        p = jax.nn.softmax(jnp.concatenate([s_win, s_ctx], axis=-1), axis=-1)
        nw = kh * GRID_W
        return (jnp.einsum('bhqk,bkhd->bqhd', p[..., :nw], vr)
                + jnp.einsum('bhqk,bkhd->bqhd', p[..., nw:], vc))

    o = lax.map(row, (jnp.arange(R), jnp.swapaxes(grid(q), 0, 1)))
    return jnp.swapaxes(o, 0, 1).reshape(B, n, H * dh)


def merge_branches(ys, gate_cols, proj, w_o):
    br = jnp.einsum('btmc,mcd->btmd', jnp.stack(ys, axis=2), proj)
    g = jax.nn.sigmoid(gate_cols.reshape(gate_cols.shape[0], gate_cols.shape[1], N_BRANCH, -1))
    return jnp.sum(g * br, axis=2) @ w_o


def moe_ffn(u, router_w, router_b, w1, w3, w2):
    n_tok, d = u.shape
    s = jax.nn.sigmoid(u @ router_w)
    sel = s + router_b
    per = N_EXPERTS // N_GROUPS
    group_score = lax.top_k(sel.reshape(n_tok, N_GROUPS, per), TOP_K)[0].sum(-1)
    best_group = jnp.argmax(group_score, axis=-1)
    in_group = (jnp.arange(N_EXPERTS) // per)[None, :] == best_group[:, None]
    _, idx = lax.top_k(jnp.where(in_group, sel, -jnp.inf), TOP_K)
    wts = jnp.take_along_axis(s, idx, axis=-1)
    wts = wts / jnp.sum(wts, -1, keepdims=True)
    n_slot = n_tok * TOP_K
    e_flat = idx.reshape(-1).astype(jnp.int32)
    t_flat = jnp.repeat(jnp.arange(n_tok, dtype=jnp.int32), TOP_K)
    w_flat = wts.reshape(-1)
    order = jnp.argsort(e_flat)
    e_s, t_s, w_s = e_flat[order], t_flat[order], w_flat[order]
    counts = jax.ops.segment_sum(jnp.ones_like(e_flat), e_flat, num_segments=N_EXPERTS)
    padded = (counts + MOE_BLOCK - 1) // MOE_BLOCK * MOE_BLOCK
    start = jnp.cumsum(counts) - counts
    pend = jnp.cumsum(padded)
    pstart = pend - padded
    dest = pstart[e_s] + jnp.arange(n_slot, dtype=jnp.int32) - start[e_s]
    n_blk = -(-n_slot // MOE_BLOCK) + N_EXPERTS
    cap = n_blk * MOE_BLOCK
    buf_tok = jnp.full((cap,), n_tok, jnp.int32).at[dest].set(t_s)
    buf_w = jnp.zeros((cap,), jnp.float32).at[dest].set(w_s)
    blk_exp = jnp.minimum(jnp.searchsorted(pend, jnp.arange(n_blk, dtype=jnp.int32) * MOE_BLOCK, side='right'),
                          N_EXPERTS - 1)
    u_pad = jnp.concatenate([u, jnp.zeros((1, d), u.dtype)], axis=0)
    xb = u_pad[buf_tok].reshape(n_blk, MOE_BLOCK, d)

    def expert_block(args):
        xi, e = args
        return (jax.nn.silu(xi @ w1[e]) * (xi @ w3[e])) @ w2[e]

    yb = lax.map(expert_block, (xb, blk_exp)).reshape(cap, d)
    y = jax.ops.segment_sum(yb * buf_w[:, None], buf_tok, num_segments=n_tok + 1)
    return y[:n_tok]


def _pad_w_in(w):
    segs = jnp.split(w, SPLIT_IDX, axis=-1)
    ab = jnp.concatenate([segs[3], segs[4]], axis=-1)
    ab = jnp.pad(ab, ((0, 0), (0, LANE - ab.shape[-1])))
    return jnp.concatenate([segs[0], segs[1], segs[2], ab, segs[5], segs[6], segs[7]], axis=-1)


PAD_SIZES = (3 * HY_W, 3 * MIX_W, MIX_W, LANE, 3 * MIX_W, 3 * MIX_W, N_BRANCH * D_MODEL)
PAD_IDX = tuple(sum(PAD_SIZES[:i + 1]) for i in range(len(PAD_SIZES) - 1))
IN_PAD = sum(PAD_SIZES)
IN_TN = 640


def kernel(x, c, ctx, c_ctx, w_mod, b_mod, w_in, hy_conv, hy_w1, hy_b1, hy_w2, hy_b2, hy_w3, hy_freq,
           hy_deltas, hy_skip, gdn_conv, gdn_a_log, gdn_dt_bias, gdn_norm, diff_lam, diff_norm, na_rpb,
           branch_proj, w_out, ln_g, ln_b, router_w, router_b, exp_w1, exp_w3, exp_w2):
    B, n, D = x.shape
    m = ctx.shape[1]
    depth = w_mod.shape[0]
    dn_alpha = (2 * depth) ** 0.25
    ang_r, ang_c = axial_rope_angles(n)
    zero_state = jnp.zeros((B, GDN_H, HEAD_DIM, HEAD_DIM), jnp.float32)
    xc = ctx
    tiles_x = n // ROW_TILE
    for l in range(depth):
        ctx_out = l < depth - 1
        mx = jnp.split((jax.nn.silu(c) @ w_mod[l] + b_mod[l])[:, None, :], 6, axis=-1)
        mc = jnp.split((jax.nn.silu(c_ctx) @ w_mod[l] + b_mod[l])[None, None, :], 6, axis=-1)

        tok = jnp.concatenate([x.reshape(B * n, D), xc.reshape(B * m, D)], axis=0)
        mod_x = jnp.stack([mx[0][:, 0], 1.0 + mx[1][:, 0]], axis=1)
        mod_c = jnp.stack([mc[0][0], 1.0 + mc[1][0]], axis=1)
        mod = jnp.concatenate([jnp.repeat(mod_x, tiles_x, axis=0),
                               jnp.repeat(mod_c, B * m // ROW_TILE, axis=0)], axis=0)
        proj = ln_mod_matmul(tok, mod, _pad_w_in(w_in[l]).astype(jnp.bfloat16), IN_TN)
        px = proj[:B * n].reshape(B, n, IN_PAD)
        pc = proj[B * n:].reshape(B, m, IN_PAD)
        hy_x, gqkv_x, gz_x, gab_x, dqkv_x, nqkv_x, gate_x = jnp.split(px, PAD_IDX, axis=-1)
        hy_c, gqkv_c, gz_c, gab_c, dqkv_c, nqkv_c, gate_c = jnp.split(pc, PAD_IDX, axis=-1)
        ga_x, gb_x = gab_x[..., :2 * GDN_H], gab_x[..., 2 * GDN_H:4 * GDN_H]
        ga_c, gb_c = gab_c[..., :2 * GDN_H], gab_c[..., 2 * GDN_H:4 * GDN_H]
        filt = (hy_w1[l], hy_b1[l], hy_w2[l], hy_b2[l], hy_w3[l], hy_freq[l], hy_deltas[l])

        ya_x = hyena_mix(hy_x, hy_conv[l], hy_skip[l], hyena_filters(n, *filt))

        gdn_p = (gdn_conv[l], gdn_a_log[l], gdn_dt_bias[l], gdn_norm[l])
        yb_c, st_c = gdn_mix(gqkv_c, gz_c, ga_c, gb_c, *gdn_p, (zero_state, zero_state), ctx_out)
        yb_x, _ = gdn_mix(gqkv_x, gz_x, ga_x, gb_x, *gdn_p, st_c, True)

        lq1, lk1, lq2, lk2 = diff_lam[l]
        lam_init = 0.8 - 0.6 * math.exp(-0.3 * l)
        lam = jnp.exp(jnp.sum(lq1 * lk1)) - jnp.exp(jnp.sum(lq2 * lk2)) + lam_init
        dq_x, dk_x, dv_x = split_diff(dqkv_x)
        dq_c, dk_c, dv_c = split_diff(dqkv_c)
        yc_x = diff_finish(diff_latent(dq_x, dk_x, dv_x, dk_c, dv_c, lam, ang_r, ang_c), diff_norm[l], lam_init)

        nq_x, nk_x, nv_x = [t.reshape(B, n, NA_H, HEAD_DIM) for t in jnp.split(nqkv_x, 3, axis=-1)]
        nq_c, nk_c, nv_c = [t.reshape(B, m, NA_H, HEAD_DIM) for t in jnp.split(nqkv_c, 3, axis=-1)]
        yd_x = na_latent(nq_x, nk_x, nv_x, nk_c, nv_c, na_rpb[l])

        mix_x = merge_branches([ya_x, yb_x, yc_x, yd_x], gate_x, branch_proj[l], w_out[l])
        x = layer_norm(dn_alpha * x + mx[2] * mix_x, ln_g[l, 0], ln_b[l, 0])
        if ctx_out:
            ya_c = hyena_mix(hy_c, hy_conv[l], hy_skip[l], hyena_filters(m, *filt))
            yc_c = diff_finish(diff_attend(dq_c, dk_c, dv_c, lam), diff_norm[l], lam_init)
            yd_c = dense_attend(nq_c, nk_c, nv_c).reshape(B, m, NA_H * HEAD_DIM)
            mix_c = merge_branches([ya_c, yb_c, yc_c, yd_c], gate_c, branch_proj[l], w_out[l])
            xc = layer_norm(dn_alpha * xc + mc[2] * mix_c, ln_g[l, 0], ln_b[l, 0])

        h2x = modulate(x, mx[3], mx[4]).reshape(B * n, D)
        if ctx_out:
            h2c = modulate(xc, mc[3], mc[4]).reshape(B * m, D)
            y_all = moe_ffn(jnp.concatenate([h2x, h2c], axis=0), router_w, router_b, exp_w1[l], exp_w3[l], exp_w2[l])
            y_x = y_all[:B * n].reshape(B, n, D)
            y_c = y_all[B * n:].reshape(B, m, D)
            xc = layer_norm(dn_alpha * xc + mc[5] * y_c, ln_g[l, 1], ln_b[l, 1])
        else:
            y_x = moe_ffn(h2x, router_w, router_b, exp_w1[l], exp_w3[l], exp_w2[l]).reshape(B, n, D)
        x = layer_norm(dn_alpha * x + mx[5] * y_x, ln_g[l, 1], ln_b[l, 1])
    return x
```

```python
import functools
import math

import jax
import jax.numpy as jnp
from jax import lax
from jax.experimental import pallas as pl
from jax.experimental.pallas import tpu as pltpu

D_MODEL = 1024
GRID_W = 64
HEAD_DIM = 64
MIX_W = D_MODEL // 2
N_BRANCH = 4
HY_W = MIX_W
HY_ORDER = 2
HY_BANDS = 16
HY_MOD_SHIFT = 0.05
GDN_H = MIX_W // HEAD_DIM
GDN_CHUNK = 64
DIFF_VD = 2 * HEAD_DIM
DIFF_H = MIX_W // DIFF_VD
NA_H = MIX_W // HEAD_DIM
NA_ROWS = 8
NA_COLS = 16
N_EXPERTS = 16
N_GROUPS = 4
TOP_K = 2
MOE_BLOCK = 128
Q_BLOCK = 128
ROPE_THETA = 10000.0
LN_EPS = 1e-5
RMS_EPS = 1e-6
SPLIT_SIZES = (3 * HY_W, 3 * MIX_W, MIX_W, 2 * GDN_H, 2 * GDN_H, 3 * MIX_W, 3 * MIX_W, N_BRANCH * D_MODEL)
SPLIT_IDX = tuple(sum(SPLIT_SIZES[:i + 1]) for i in range(len(SPLIT_SIZES) - 1))

LANE = 128
ROW_TILE = 512
MERGE_TILE = 256
VMEM_LIMIT = 48 * 1024 * 1024


def _ln_mod_matmul_kernel(x_ref, mod_ref, w_ref, o_ref, h_ref):
    @pl.when(pl.program_id(1) == 0)
    def _():
        x = x_ref[...]
        mu = jnp.mean(x, axis=-1, keepdims=True)
        xc = x - mu
        var = jnp.mean(xc * xc, axis=-1, keepdims=True)
        y = xc * lax.rsqrt(var + LN_EPS)
        h_ref[...] = (y * mod_ref[0, 1:2, :] + mod_ref[0, 0:1, :]).astype(h_ref.dtype)

    o_ref[...] = jnp.dot(h_ref[...], w_ref[...], preferred_element_type=jnp.float32).astype(o_ref.dtype)


def ln_mod_matmul(x, mod, w, tn, out_dtype=jnp.float32):
    T, D = x.shape
    N = w.shape[1]
    return pl.pallas_call(
        _ln_mod_matmul_kernel,
        grid=(T // ROW_TILE, N // tn),
        in_specs=[
            pl.BlockSpec((ROW_TILE, D), lambda i, j: (i, 0)),
            pl.BlockSpec((1, 6, D), lambda i, j: (i * (ROW_TILE // MERGE_TILE), 0, 0)),
            pl.BlockSpec((D, tn), lambda i, j: (0, j)),
        ],
        out_specs=pl.BlockSpec((ROW_TILE, tn), lambda i, j: (i, j)),
        out_shape=jax.ShapeDtypeStruct((T, N), out_dtype),
        scratch_shapes=[pltpu.VMEM((ROW_TILE, D), jnp.bfloat16)],
        compiler_params=pltpu.CompilerParams(
            dimension_semantics=("arbitrary", "arbitrary"), vmem_limit_bytes=VMEM_LIMIT),
        name="ln_mod_matmul",
    )(x, mod, w)


def _layer_norm_rows(x):
    mu = jnp.mean(x, axis=-1, keepdims=True)
    xc = x - mu
    var = jnp.mean(xc * xc, axis=-1, keepdims=True)
    return xc * lax.rsqrt(var + LN_EPS)


_NT_DIMS = (((1,), (1,)), ((), ()))


def _diff_attn_kernel(lam_ref, q_ref, k_ref, v_ref, nw_ref, o_ref, m_ref, l_ref, acc_ref, *, out_scale):
    kv = pl.program_id(3)

    @pl.when(kv == 0)
    def _():
        m_ref[...] = jnp.full(m_ref.shape, -jnp.inf, jnp.float32)
        l_ref[...] = jnp.zeros(l_ref.shape, jnp.float32)
        acc_ref[...] = jnp.zeros(acc_ref.shape, jnp.float32)

    q = q_ref[0]
    k = k_ref[0]
    v = v_ref[0]
    for c in range(2):
        sl = slice(c * HEAD_DIM, (c + 1) * HEAD_DIM)
        s = lax.dot_general(q[:, sl], k[:, sl], _NT_DIMS, preferred_element_type=jnp.float32)
        m_prev = m_ref[c]
        m_new = jnp.maximum(m_prev, jnp.max(s, axis=-1, keepdims=True))
        alpha = jnp.exp(m_prev - m_new)
        p = jnp.exp(s - m_new)
        l_ref[c] = alpha * l_ref[c] + jnp.sum(p, axis=-1, keepdims=True)
        acc_ref[c] = alpha * acc_ref[c] + jnp.dot(p.astype(v.dtype), v, preferred_element_type=jnp.float32)
        m_ref[c] = m_new

    @pl.when(kv == pl.num_programs(3) - 1)
    def _():
        o = acc_ref[0] / l_ref[0] - lam_ref[0] * (acc_ref[1] / l_ref[1])
        o = o * lax.rsqrt(jnp.mean(o * o, axis=-1, keepdims=True) + RMS_EPS) * (nw_ref[...] * out_scale)
        o_ref[0] = o.astype(o_ref.dtype)


def _largest_divisor(n, candidates):
    return next(t for t in candidates if n % t == 0)


def diff_attention(q, k, v, lam, norm_w, out_scale, out_dtype=jnp.float32):
    B, nq, W = q.shape
    nk = k.shape[1]
    tq = _largest_divisor(nq, (512, 256, 128))
    tk = _largest_divisor(nk, (768, 512, 256, 128))
    H = W // DIFF_VD
    return pl.pallas_call(
        functools.partial(_diff_attn_kernel, out_scale=out_scale),
        grid=(B, H, nq // tq, nk // tk),
        in_specs=[
            pl.BlockSpec(memory_space=pltpu.SMEM),
            pl.BlockSpec((1, tq, DIFF_VD), lambda b, h, i, j: (b, i, h)),
            pl.BlockSpec((1, tk, DIFF_VD), lambda b, h, i, j: (b, j, h)),
            pl.BlockSpec((1, tk, DIFF_VD), lambda b, h, i, j: (b, j, h)),
            pl.BlockSpec((1, DIFF_VD), lambda b, h, i, j: (0, 0)),
        ],
        out_specs=pl.BlockSpec((1, tq, DIFF_VD), lambda b, h, i, j: (b, i, h)),
        out_shape=jax.ShapeDtypeStruct((B, nq, W), out_dtype),
        scratch_shapes=[pltpu.VMEM((2, tq, 1), jnp.float32), pltpu.VMEM((2, tq, 1), jnp.float32),
                        pltpu.VMEM((2, tq, DIFF_VD), jnp.float32)],
        compiler_params=pltpu.CompilerParams(
            dimension_semantics=("arbitrary",) * 4, vmem_limit_bytes=VMEM_LIMIT),
        name="diff_attention",
    )(lam.reshape(1).astype(jnp.float32), q, k, v, norm_w.reshape(1, DIFF_VD))


NA_QROWS = 8
NA_KROWS = 16
NA_NEG = -1e30


def _na_kernel(q_ref, k_ref, v_ref, kc_ref, vc_ref, tab_ref, o_ref, *, n_kblocks):
    g = pl.program_id(2)
    kb = (NA_KROWS // 4) * GRID_W
    base = jnp.clip(2 * g - 1, 0, n_kblocks - 4)
    start = pl.multiple_of(base * kb, kb)
    nwin = NA_KROWS * GRID_W
    q = q_ref[0]
    kw = k_ref[0, pl.ds(start, nwin), :]
    vw = v_ref[0, pl.ds(start, nwin), :]
    kc = kc_ref[0]
    vc = vc_ref[0]
    outs = []
    for hh in range(2):
        sl = slice(hh * HEAD_DIM, (hh + 1) * HEAD_DIM)
        qh = q[:, sl]
        s = lax.dot_general(qh, kw[:, sl], _NT_DIMS, preferred_element_type=jnp.float32) + tab_ref[0, hh]
        sc = lax.dot_general(qh, kc[:, sl], _NT_DIMS, preferred_element_type=jnp.float32)
        m = jnp.maximum(jnp.max(s, axis=-1, keepdims=True), jnp.max(sc, axis=-1, keepdims=True))
        p = jnp.exp(s - m)
        pc = jnp.exp(sc - m)
        l = jnp.sum(p, axis=-1, keepdims=True) + jnp.sum(pc, axis=-1, keepdims=True)
        o = (jnp.dot(p.astype(vw.dtype), vw[:, sl], preferred_element_type=jnp.float32)
             + jnp.dot(pc.astype(vc.dtype), vc[:, sl], preferred_element_type=jnp.float32))
        outs.append(o / l)
    o_ref[0] = jnp.concatenate(outs, axis=-1).astype(o_ref.dtype)


def na_bias_tables(rpb, R):
    H = rpb.shape[0]
    G = R // NA_QROWS
    cols = jnp.arange(GRID_W)
    cstart = jnp.clip(cols - NA_COLS // 2, 0, GRID_W - NA_COLS)
    col_ok = (cols[None, :] >= cstart[:, None]) & (cols[None, :] < cstart[:, None] + NA_COLS)
    dc = jnp.clip(cols[None, :] - cols[:, None], -(NA_COLS - 1), NA_COLS - 1) + (NA_COLS - 1)
    rpb_c = rpb[:, :, dc]
    tabs = []
    for g in (0, 1, G - 1):
        base = min(max(2 * g - 1, 0), R // 4 - 4)
        r = NA_QROWS * g + jnp.arange(NA_QROWS)
        rs = jnp.clip(r - NA_ROWS // 2, 0, R - NA_ROWS)
        kr = 4 * base + jnp.arange(NA_KROWS)
        valid = (kr[None, :] >= rs[:, None]) & (kr[None, :] < rs[:, None] + NA_ROWS)
        dr = jnp.clip(kr[None, :] - r[:, None] + (NA_ROWS - 1), 0, 2 * NA_ROWS - 2)
        tab = rpb_c[:, dr]
        ok = valid[None, :, :, None, None] & col_ok[None, None, None, :, :]
        tab = jnp.where(ok, tab, NA_NEG)
        tabs.append(jnp.transpose(tab, (0, 1, 3, 2, 4)).reshape(H, NA_QROWS * GRID_W, NA_KROWS * GRID_W))
    return jnp.stack(tabs, axis=0)


def na_attention(q, k, v, kc, vc, tabs, out_dtype=jnp.float32):
    B, n, W = q.shape
    m = kc.shape[1]
    tq = NA_QROWS * GRID_W
    G = n // tq
    n_kblocks = n // ((NA_KROWS // 4) * GRID_W)

    def tab_index(b, hp, g):
        return (jnp.where(g == 0, 0, jnp.where(g == G - 1, 2, 1)), hp, 0, 0)

    return pl.pallas_call(
        functools.partial(_na_kernel, n_kblocks=n_kblocks),
        grid=(B, W // LANE, G),
        in_specs=[
            pl.BlockSpec((1, tq, LANE), lambda b, hp, g: (b, g, hp)),
            pl.BlockSpec((1, n, LANE), lambda b, hp, g: (b, 0, hp)),
            pl.BlockSpec((1, n, LANE), lambda b, hp, g: (b, 0, hp)),
            pl.BlockSpec((1, m, LANE), lambda b, hp, g: (b, 0, hp)),
            pl.BlockSpec((1, m, LANE), lambda b, hp, g: (b, 0, hp)),
            pl.BlockSpec((1, 2, tq, NA_KROWS * GRID_W), tab_index),
        ],
        out_specs=pl.BlockSpec((1, tq, LANE), lambda b, hp, g: (b, g, hp)),
        out_shape=jax.ShapeDtypeStruct((B, n, W), out_dtype),
        compiler_params=pltpu.CompilerParams(
            dimension_semantics=("arbitrary",) * 3, vmem_limit_bytes=VMEM_LIMIT),
        name="na_attention",
    )(q, k, v, kc, vc, tabs)


def _merge_kernel(x_ref, ya_ref, yb_ref, yc_ref, yd_ref, gc_ref, mod_ref, p_ref, wo_ref, lng_ref, lnb_ref,
                  o_ref, *, alpha):
    acc = None
    for mi, y_ref in enumerate((ya_ref, yb_ref, yc_ref, yd_ref)):
        br = jnp.dot(y_ref[...].astype(jnp.bfloat16), p_ref[mi], preferred_element_type=jnp.float32)
        gm = jax.nn.sigmoid(gc_ref[:, mi * D_MODEL:(mi + 1) * D_MODEL].astype(jnp.float32))
        acc = gm * br if acc is None else acc + gm * br
    mix = jnp.dot(acc.astype(jnp.bfloat16), wo_ref[...], preferred_element_type=jnp.float32)
    z = alpha * x_ref[...] + mod_ref[0, 2:3, :] * mix
    o_ref[...] = _layer_norm_rows(z) * lng_ref[...] + lnb_ref[...]


def merge_residual_ln(x, ys, gate_cols, mod, proj, w_o, ln_g, ln_b, alpha):
    T, D = x.shape
    row = lambda i: (i, 0)
    const2 = lambda i: (0, 0)
    return pl.pallas_call(
        functools.partial(_merge_kernel, alpha=alpha),
        grid=(T // MERGE_TILE,),
        in_specs=[pl.BlockSpec((MERGE_TILE, D), row)]
        + [pl.BlockSpec((MERGE_TILE, MIX_W), row)] * 4
        + [pl.BlockSpec((MERGE_TILE, N_BRANCH * D), row),
           pl.BlockSpec((1, 6, D), lambda i: (i, 0, 0)),
           pl.BlockSpec((N_BRANCH, MIX_W, D), lambda i: (0, 0, 0)),
           pl.BlockSpec((D, D), const2),
           pl.BlockSpec((1, D), const2),
           pl.BlockSpec((1, D), const2)],
        out_specs=pl.BlockSpec((MERGE_TILE, D), row),
        out_shape=jax.ShapeDtypeStruct((T, D), jnp.float32),
        compiler_params=pltpu.CompilerParams(dimension_semantics=("arbitrary",), vmem_limit_bytes=VMEM_LIMIT),
        name="merge_residual_ln",
    )(x, *ys, gate_cols, mod, proj, w_o, ln_g.reshape(1, D), ln_b.reshape(1, D))


def _route_kernel(x_ref, mod_ref, rw_ref, rb_ref, h_ref, idx_ref, wts_ref):
    h = (_layer_norm_rows(x_ref[...]) * mod_ref[0, 4:5, :] + mod_ref[0, 3:4, :]).astype(jnp.bfloat16)
    h_ref[...] = h
    logits = lax.dot_general(rw_ref[...], h, _NT_DIMS, preferred_element_type=jnp.float32)
    s = jax.nn.sigmoid(logits)
    sel = s + rb_ref[...]
    per = N_EXPERTS // N_GROUPS
    srow = [s[e:e + 1, :] for e in range(N_EXPERTS)]
    vrow = [sel[e:e + 1, :] for e in range(N_EXPERTS)]
    best = None
    for gi in range(N_GROUPS):
        grp = vrow[gi * per:(gi + 1) * per]
        gs = None
        for a in range(per):
            for b in range(a + 1, per):
                ps = grp[a] + grp[b]
                gs = ps if gs is None else jnp.maximum(gs, ps)
        if best is None:
            best, bg = gs, jnp.zeros(gs.shape, jnp.int32)
        else:
            upd = gs > best
            bg = jnp.where(upd, gi, bg)
            best = jnp.where(upd, gs, best)
    cv, cs = [], []
    for j in range(per):
        v_j, s_j = vrow[j], srow[j]
        for gi in range(1, N_GROUPS):
            v_j = jnp.where(bg == gi, vrow[gi * per + j], v_j)
            s_j = jnp.where(bg == gi, srow[gi * per + j], s_j)
        cv.append(v_j)
        cs.append(s_j)

    def first_argmax(vals):
        bv, bi = vals[0], jnp.zeros(vals[0].shape, jnp.int32)
        for j in range(1, per):
            upd = vals[j] > bv
            bi = jnp.where(upd, j, bi)
            bv = jnp.where(upd, vals[j], bv)
        return bi

    i1 = first_argmax(cv)
    i2 = first_argmax([jnp.where(i1 == j, -jnp.inf, cv[j]) for j in range(per)])
    w1 = cs[0]
    w2 = cs[0]
    for j in range(1, per):
        w1 = jnp.where(i1 == j, cs[j], w1)
        w2 = jnp.where(i2 == j, cs[j], w2)
    tot = w1 + w2
    idx_ref[...] = jnp.concatenate([bg * per + i1, bg * per + i2], axis=0)
    wts_ref[...] = jnp.concatenate([w1 / tot, w2 / tot], axis=0)


def moe_route(x, mod, router_w, router_b):
    T, D = x.shape
    return pl.pallas_call(
        _route_kernel,
        grid=(T // MERGE_TILE,),
        in_specs=[pl.BlockSpec((MERGE_TILE, D), lambda i: (i, 0)),
                  pl.BlockSpec((1, 6, D), lambda i: (i, 0, 0)),
                  pl.BlockSpec((N_EXPERTS, D), lambda i: (0, 0)),
                  pl.BlockSpec((N_EXPERTS, 1), lambda i: (0, 0))],
        out_specs=[pl.BlockSpec((MERGE_TILE, D), lambda i: (i, 0)),
                   pl.BlockSpec((TOP_K, MERGE_TILE), lambda i: (0, i)),
                   pl.BlockSpec((TOP_K, MERGE_TILE), lambda i: (0, i))],
        out_shape=[jax.ShapeDtypeStruct((T, D), jnp.bfloat16),
                   jax.ShapeDtypeStruct((TOP_K, T), jnp.int32),
                   jax.ShapeDtypeStruct((TOP_K, T), jnp.float32)],
        compiler_params=pltpu.CompilerParams(dimension_semantics=("arbitrary",), vmem_limit_bytes=VMEM_LIMIT),
        name="moe_route",
    )(x, mod, router_w.T.astype(jnp.bfloat16), router_b.reshape(N_EXPERTS, 1).astype(jnp.float32))


EXPERT_BLOCK = 256


def _expert_kernel(be_ref, nu_ref, x_ref, w1_ref, w3_ref, w2_ref, o_ref):
    i = pl.program_id(0)

    @pl.when(i < nu_ref[0])
    def _():
        x = x_ref[...]
        a = jnp.dot(x, w1_ref[0], preferred_element_type=jnp.float32)
        b = jnp.dot(x, w3_ref[0], preferred_element_type=jnp.float32)
        hmid = (a * jax.nn.sigmoid(a) * b).astype(jnp.bfloat16)
        o_ref[...] = jnp.dot(hmid, w2_ref[0], preferred_element_type=jnp.float32).astype(o_ref.dtype)

    @pl.when(i >= nu_ref[0])
    def _():
        o_ref[...] = jnp.zeros(o_ref.shape, o_ref.dtype)


def expert_ffn(blk_exp, n_used, xb, w1, w3, w2, out_dtype=jnp.float32):
    cap, D = xb.shape
    F = w1.shape[-1]
    n_blk = cap // EXPERT_BLOCK
    return pl.pallas_call(
        _expert_kernel,
        grid_spec=pltpu.PrefetchScalarGridSpec(
            num_scalar_prefetch=2,
            grid=(n_blk,),
            in_specs=[pl.BlockSpec((EXPERT_BLOCK, D), lambda i, be, nu: (i, 0)),
                      pl.BlockSpec((1, D, F), lambda i, be, nu: (be[i], 0, 0)),
                      pl.BlockSpec((1, D, F), lambda i, be, nu: (be[i], 0, 0)),
                      pl.BlockSpec((1, F, D), lambda i, be, nu: (be[i], 0, 0))],
            out_specs=pl.BlockSpec((EXPERT_BLOCK, D), lambda i, be, nu: (i, 0)),
        ),
        out_shape=jax.ShapeDtypeStruct((cap, D), out_dtype),
        compiler_params=pltpu.CompilerParams(dimension_semantics=("arbitrary",), vmem_limit_bytes=VMEM_LIMIT),
        name="expert_ffn",
    )(blk_exp, n_used, xb, w1, w3, w2)


def _residual_ln_kernel(x_ref, y_ref, mod_ref, lng_ref, lnb_ref, o_ref, *, alpha, gate_row):
    z = alpha * x_ref[...] + mod_ref[0, gate_row:gate_row + 1, :] * y_ref[...].astype(jnp.float32)
    o_ref[...] = _layer_norm_rows(z) * lng_ref[...] + lnb_ref[...]


def residual_ln(x, y, mod, ln_g, ln_b, alpha, gate_row):
    T, D = x.shape
    row = lambda i: (i, 0)
    return pl.pallas_call(
        functools.partial(_residual_ln_kernel, alpha=alpha, gate_row=gate_row),
        grid=(T // MERGE_TILE,),
        in_specs=[pl.BlockSpec((MERGE_TILE, D), row), pl.BlockSpec((MERGE_TILE, D), row),
                  pl.BlockSpec((1, 6, D), lambda i: (i, 0, 0)),
                  pl.BlockSpec((1, D), lambda i: (0, 0)), pl.BlockSpec((1, D), lambda i: (0, 0))],
        out_specs=pl.BlockSpec((MERGE_TILE, D), row),
        out_shape=jax.ShapeDtypeStruct((T, D), jnp.float32),
        compiler_params=pltpu.CompilerParams(dimension_semantics=("arbitrary",), vmem_limit_bytes=VMEM_LIMIT),
        name="residual_ln",
    )(x, y, mod, ln_g.reshape(1, D), ln_b.reshape(1, D))


def moe_dispatch(idx, n_tok):
    n_slot = n_tok * TOP_K
    e_flat = idx.T.reshape(-1)
    onehot = (e_flat[:, None] == jnp.arange(N_EXPERTS, dtype=jnp.int32)[None, :]).astype(jnp.int32)
    csum = jnp.cumsum(onehot, axis=0)
    rank = jnp.sum(csum * onehot, axis=1) - 1
    counts = csum[-1]
    padded = (counts + EXPERT_BLOCK - 1) // EXPERT_BLOCK * EXPERT_BLOCK
    pend = jnp.cumsum(padded)
    pstart = pend - padded
    dest = jnp.sum(onehot * pstart[None, :], axis=1) + rank
    n_blk = -(-n_slot // EXPERT_BLOCK) + N_EXPERTS
    cap = n_blk * EXPERT_BLOCK
    t_flat = jnp.arange(n_slot, dtype=jnp.int32) // TOP_K
    buf_tok = jnp.zeros((cap,), jnp.int32).at[dest].set(t_flat)
    blk_exp = jnp.minimum(jnp.searchsorted(pend, jnp.arange(n_blk, dtype=jnp.int32) * EXPERT_BLOCK, side='right'),
                          N_EXPERTS - 1).astype(jnp.int32)
    n_used = (pend[-1] // EXPERT_BLOCK).astype(jnp.int32).reshape(1)
    return buf_tok, dest, blk_exp, n_used


def layer_norm(x, g=None, b=None):
    mu = jnp.mean(x, -1, keepdims=True)
    var = jnp.mean(jnp.square(x - mu), -1, keepdims=True)
    y = (x - mu) * lax.rsqrt(var + LN_EPS)
    if g is not None:
        y = y * g + b
    return y


def rms_norm(x, w):
    return x * lax.rsqrt(jnp.mean(jnp.square(x), -1, keepdims=True) + RMS_EPS) * w


def l2_normalize(x):
    return x * lax.rsqrt(jnp.sum(jnp.square(x), -1, keepdims=True) + RMS_EPS)


def modulate(x, shift, scale):
    return layer_norm(x) * (1.0 + scale) + shift


def centred_dwconv(u, w):
    K = w.shape[-1]
    T = u.shape[1]
    up = jnp.pad(u, ((0, 0), (K // 2, K // 2), (0, 0)))
    out = up[:, 0:T, :] * w[:, 0]
    for j in range(1, K):
        out = out + up[:, j:j + T, :] * w[:, j]
    return out


def hyena_filters(L, w1, b1, w2, b2, w3, freq, deltas):
    f32 = jnp.float32
    t = jnp.linspace(0.0, 1.0, L, dtype=f32)[:, None]
    ang = 2.0 * math.pi * jnp.arange(L, dtype=f32)[:, None] / L
    bands = jnp.linspace(1e-4, HY_BANDS - 1, HY_BANDS, dtype=f32)[None, :]
    feats = jnp.concatenate([t, jnp.cos(bands * ang), -jnp.sin(bands * ang)], axis=-1)
    h = jnp.sin(freq * (feats @ w1 + b1))
    h = jnp.sin(freq * (h @ w2 + b2))
    h = (h @ w3).reshape(L, 2, HY_ORDER, HY_W)
    window = jnp.exp(-t[:, :, None, None] * jnp.abs(deltas)) + HY_MOD_SHIFT
    h = h * window
    h_fwd, h_bwd = h[:, 0], h[:, 1]
    taps = jnp.concatenate([h_fwd, jnp.zeros_like(h_fwd[:1]), jnp.flip(h_bwd[1:], 0)], axis=0)
    taps = taps / jnp.sum(jnp.abs(taps), axis=0, keepdims=True)
    return jnp.fft.rfft(taps, axis=0)


def hyena_mix(proj, conv_w, skip, filt_f):
    L = proj.shape[1]
    u = centred_dwconv(proj, conv_w)
    v, x1, x2 = jnp.split(u, 3, axis=-1)
    z = v
    for o, gate in enumerate((x1, x2)):
        zf = jnp.fft.rfft(z, n=2 * L, axis=1)
        y = jnp.fft.irfft(zf * filt_f[:, o], n=2 * L, axis=1)[:, :L]
        z = gate * (y + z * skip[o])
    return z


def gated_delta_chunks(q, k, v, g, beta, s0):
    B, T, H, _ = q.shape
    C = GDN_CHUNK
    N = T // C

    def chunks(a):
        return jnp.moveaxis(a.reshape(B, N, C, H, *a.shape[3:]), 3, 1)

    q, k, v, g, beta = chunks(q), chunks(k), chunks(v), chunks(g), chunks(beta)
    gc = jnp.cumsum(g, axis=-1)
    tri = jnp.tril(jnp.ones((C, C), bool))
    strict = jnp.tril(jnp.ones((C, C), bool), -1)
    decay = jnp.exp(jnp.where(tri, gc[..., :, None] - gc[..., None, :], -jnp.inf))
    kb = k * beta[..., None]
    A = jnp.where(strict, jnp.einsum('bhnid,bhnjd->bhnij', kb, k) * decay, 0.0)
    eye = jnp.eye(C, dtype=jnp.float32)
    Tm = lax.linalg.triangular_solve(eye + A, jnp.broadcast_to(eye, A.shape),
                                     left_side=True, lower=True, unit_diagonal=True)
    u_val = jnp.einsum('bhnij,bhnjd->bhnid', Tm, v * beta[..., None])
    w_val = jnp.einsum('bhnij,bhnjd->bhnid', Tm, kb * jnp.exp(gc)[..., None])
    intra = jnp.einsum('bhnid,bhnjd->bhnij', q, k) * decay
    q_dec = q * jnp.exp(gc)[..., None]
    g_last = gc[..., -1]
    k_dec = k * jnp.exp(g_last[..., None] - gc)[..., None]

    def step(S, xs):
        qd, w, u, att, kd, gl = xs
        v_new = u - jnp.einsum('bhcd,bhde->bhce', w, S)
        o = jnp.einsum('bhcd,bhde->bhce', qd, S) + jnp.einsum('bhij,bhje->bhie', att, v_new)
        S = S * jnp.exp(gl)[..., None, None] + jnp.einsum('bhcd,bhce->bhde', kd, v_new)
        return S, o

    xs = tuple(jnp.moveaxis(a, 2, 0) for a in (q_dec, w_val, u_val, intra, k_dec, g_last))
    s_fin, o = lax.scan(step, s0, xs)
    o = jnp.transpose(o, (1, 0, 3, 2, 4)).reshape(B, T, H, -1)
    return o, s_fin


def gdn_mix(qkv, z, a, b, conv_w, a_log, dt_bias, norm_w, init_states, with_output):
    B, T, _ = qkv.shape
    u = jax.nn.silu(centred_dwconv(qkv, conv_w))
    q, k, v = [t.reshape(B, T, GDN_H, HEAD_DIM) for t in jnp.split(u, 3, axis=-1)]
    q = l2_normalize(q) * HEAD_DIM ** -0.5
    k = l2_normalize(k)
    g = -jnp.exp(a_log) * jax.nn.softplus(a.reshape(B, T, 2, GDN_H) + dt_bias)
    beta = jax.nn.sigmoid(b.reshape(B, T, 2, GDN_H))
    outs, finals = [], []
    for d in range(2):
        rev = (lambda t: jnp.flip(t, 1)) if d == 1 else (lambda t: t)
        o, s_fin = gated_delta_chunks(rev(q), rev(k), rev(v), rev(g[:, :, d]), rev(beta[:, :, d]), init_states[d])
        finals.append(s_fin)
        if with_output:
            outs.append(rev(o))
    if not with_output:
        return None, (finals[0], finals[1])
    o = rms_norm(outs[0] + outs[1], norm_w) * jax.nn.silu(z.reshape(B, T, GDN_H, HEAD_DIM))
    return o.reshape(B, T, GDN_H * HEAD_DIM), (finals[0], finals[1])


def axial_rope_angles(n):
    t = jnp.arange(n)
    row = (t // GRID_W).astype(jnp.float32)
    col = (t % GRID_W).astype(jnp.float32)
    nf = HEAD_DIM // 4
    inv = ROPE_THETA ** (-jnp.arange(nf, dtype=jnp.float32) / nf)
    return row[:, None] * inv, col[:, None] * inv


def rope_half(x, ang):
    x1, x2 = jnp.split(x, 2, axis=-1)
    cos, sin = jnp.cos(ang), jnp.sin(ang)
    return jnp.concatenate([x1 * cos - x2 * sin, x1 * sin + x2 * cos], axis=-1)


def axial_rope(x, ang_r, ang_c):
    half = HEAD_DIM // 2
    ar = ang_r[None, :, None, None, :]
    ac = ang_c[None, :, None, None, :]
    return jnp.concatenate([rope_half(x[..., :half], ar), rope_half(x[..., half:], ac)], axis=-1)


def diff_attend(q, k, v, lam):
    s = jnp.einsum('bqhcd,bkhcd->bhcqk', q, k) * HEAD_DIM ** -0.5
    p = jax.nn.softmax(s, axis=-1)
    a = p[:, :, 0] - lam * p[:, :, 1]
    return jnp.einsum('bhqk,bkhd->bqhd', a, v)


def diff_latent(q, k, v, kc, vc, lam, ang_r, ang_c):
    B, n = q.shape[:2]
    q = axial_rope(q, ang_r, ang_c)
    k_all = jnp.concatenate([axial_rope(k, ang_r, ang_c), kc], axis=1)
    v_all = jnp.concatenate([v, vc], axis=1)
    nb = n // Q_BLOCK
    qb = jnp.swapaxes(q.reshape(B, nb, Q_BLOCK, *q.shape[2:]), 0, 1)
    o = lax.map(lambda qi: diff_attend(qi, k_all, v_all, lam), qb)
    return jnp.swapaxes(o, 0, 1).reshape(B, n, DIFF_H, DIFF_VD)


def diff_finish(o, norm_w, lam_init):
    B, T = o.shape[:2]
    return (rms_norm(o, norm_w) * (1.0 - lam_init)).reshape(B, T, DIFF_H * DIFF_VD)


def split_diff(p):
    B, T, _ = p.shape
    q, k, v = jnp.split(p, 3, axis=-1)
    return (q.reshape(B, T, DIFF_H, 2, HEAD_DIM), k.reshape(B, T, DIFF_H, 2, HEAD_DIM),
            v.reshape(B, T, DIFF_H, DIFF_VD))


def dense_attend(q, k, v):
    s = jnp.einsum('bqhd,bkhd->bhqk', q, k) * HEAD_DIM ** -0.5
    p = jax.nn.softmax(s, axis=-1)
    return jnp.einsum('bhqk,bkhd->bqhd', p, v)


def na_latent(q, k, v, kc, vc, rpb):
    B, n, H, dh = q.shape
    R = n // GRID_W
    kh, kw = min(NA_ROWS, R), NA_COLS
    grid = lambda t: t.reshape(B, R, GRID_W, H, dh)
    kg, vg = grid(k), grid(v)
    cols = jnp.arange(GRID_W)
    cstart = jnp.clip(cols - kw // 2, 0, GRID_W - kw)
    col_ok = (cols[None, :] >= cstart[:, None]) & (cols[None, :] < cstart[:, None] + kw)
    mask = jnp.broadcast_to(col_ok[:, None, :], (GRID_W, kh, GRID_W)).reshape(GRID_W, kh * GRID_W)
    dc = jnp.clip(cols[None, :] - cols[:, None], -(kw - 1), kw - 1) + (kw - 1)
    rpb_c = rpb[:, :, dc]
    scale = dh ** -0.5

    def row(args):
        r, qr = args
        rs = jnp.clip(r - kh // 2, 0, R - kh)
        kr = lax.dynamic_slice_in_dim(kg, rs, kh, axis=1).reshape(B, kh * GRID_W, H, dh)
        vr = lax.dynamic_slice_in_dim(vg, rs, kh, axis=1).reshape(B, kh * GRID_W, H, dh)
        dr = rs + jnp.arange(kh) - r + (NA_ROWS - 1)
        bias = jnp.transpose(rpb_c[:, dr], (0, 2, 1, 3)).reshape(H, GRID_W, kh * GRID_W)
        s_win = jnp.einsum('bqhd,bkhd->bhqk', qr, kr) * scale + bias[None]
        s_win = jnp.where(mask, s_win, -jnp.inf)
        s_ctx = jnp.einsum('bqhd,bkhd->bhqk', qr, kc) * scale
        p = jax.nn.softmax(jnp.concatenate([s_win, s_ctx], axis=-1), axis=-1)
        nw = kh * GRID_W
        return (jnp.einsum('bhqk,bkhd->bqhd', p[..., :nw], vr)
                + jnp.einsum('bhqk,bkhd->bqhd', p[..., nw:], vc))

    o = lax.map(row, (jnp.arange(R), jnp.swapaxes(grid(q), 0, 1)))
    return jnp.swapaxes(o, 0, 1).reshape(B, n, H * dh)


def merge_branches(ys, gate_cols, proj, w_o):
    br = jnp.einsum('btmc,mcd->btmd', jnp.stack(ys, axis=2), proj)
    g = jax.nn.sigmoid(gate_cols.reshape(gate_cols.shape[0], gate_cols.shape[1], N_BRANCH, -1))
    return jnp.sum(g * br, axis=2) @ w_o


def moe_ffn(u, router_w, router_b, w1, w3, w2):
    n_tok, d = u.shape
    s = jax.nn.sigmoid(u @ router_w)
    sel = s + router_b
    per = N_EXPERTS // N_GROUPS
    group_score = lax.top_k(sel.reshape(n_tok, N_GROUPS, per), TOP_K)[0].sum(-1)
    best_group = jnp.argmax(group_score, axis=-1)
    in_group = (jnp.arange(N_EXPERTS) // per)[None, :] == best_group[:, None]
    _, idx = lax.top_k(jnp.where(in_group, sel, -jnp.inf), TOP_K)
    wts = jnp.take_along_axis(s, idx, axis=-1)
    wts = wts / jnp.sum(wts, -1, keepdims=True)
    n_slot = n_tok * TOP_K
    e_flat = idx.reshape(-1).astype(jnp.int32)
    t_flat = jnp.repeat(jnp.arange(n_tok, dtype=jnp.int32), TOP_K)
    w_flat = wts.reshape(-1)
    order = jnp.argsort(e_flat)
    e_s, t_s, w_s = e_flat[order], t_flat[order], w_flat[order]
    counts = jax.ops.segment_sum(jnp.ones_like(e_flat), e_flat, num_segments=N_EXPERTS)
    padded = (counts + MOE_BLOCK - 1) // MOE_BLOCK * MOE_BLOCK
    start = jnp.cumsum(counts) - counts
    pend = jnp.cumsum(padded)
    pstart = pend - padded
    dest = pstart[e_s] + jnp.arange(n_slot, dtype=jnp.int32) - start[e_s]
    n_blk = -(-n_slot // MOE_BLOCK) + N_EXPERTS
    cap = n_blk * MOE_BLOCK
    buf_tok = jnp.full((cap,), n_tok, jnp.int32).at[dest].set(t_s)
    buf_w = jnp.zeros((cap,), jnp.float32).at[dest].set(w_s)
    blk_exp = jnp.minimum(jnp.searchsorted(pend, jnp.arange(n_blk, dtype=jnp.int32) * MOE_BLOCK, side='right'),
                          N_EXPERTS - 1)
    u_pad = jnp.concatenate([u, jnp.zeros((1, d), u.dtype)], axis=0)
    xb = u_pad[buf_tok].reshape(n_blk, MOE_BLOCK, d)

    def expert_block(args):
        xi, e = args
        return (jax.nn.silu(xi @ w1[e]) * (xi @ w3[e])) @ w2[e]

    yb = lax.map(expert_block, (xb, blk_exp)).reshape(cap, d)
    y = jax.ops.segment_sum(yb * buf_w[:, None], buf_tok, num_segments=n_tok + 1)
    return y[:n_tok]


def diff_branch(dqkv_x, dqkv_c, lam, norm_w, lam_init, ang_r, ang_c):
    B, n, _ = dqkv_x.shape
    m = dqkv_c.shape[1]
    q, k, v = jnp.split(dqkv_x, 3, axis=-1)
    kc, vc = dqkv_c[..., MIX_W:2 * MIX_W], dqkv_c[..., 2 * MIX_W:]
    rot = lambda t: axial_rope(t.reshape(B, n, DIFF_H, 2, HEAD_DIM), ang_r, ang_c).reshape(B, n, MIX_W)
    qr = (rot(q) * HEAD_DIM ** -0.5).astype(jnp.bfloat16)
    k_all = jnp.concatenate([rot(k), kc], axis=1).astype(jnp.bfloat16)
    v_all = jnp.concatenate([v, vc], axis=1).astype(jnp.bfloat16)
    return diff_attention(qr, k_all, v_all, lam, norm_w, 1.0 - lam_init)


def na_branch(nqkv_x, nqkv_c, rpb):
    n = nqkv_x.shape[1]
    q, k, v = jnp.split(nqkv_x, 3, axis=-1)
    kc, vc = nqkv_c[..., MIX_W:2 * MIX_W], nqkv_c[..., 2 * MIX_W:]
    bf = lambda t: t.astype(jnp.bfloat16)
    tabs = na_bias_tables(rpb, n // GRID_W)
    return na_attention(bf(q * HEAD_DIM ** -0.5), bf(k), bf(v), bf(kc), bf(vc), tabs)


def _pad_w_in(w):
    segs = jnp.split(w, SPLIT_IDX, axis=-1)
    ab = jnp.concatenate([segs[3], segs[4]], axis=-1)
    ab = jnp.pad(ab, ((0, 0), (0, LANE - ab.shape[-1])))
    return jnp.concatenate([segs[7], segs[0], segs[1], segs[2], ab, segs[5], segs[6]], axis=-1)


PAD_SIZES = (N_BRANCH * D_MODEL, 3 * HY_W, 3 * MIX_W, MIX_W, LANE, 3 * MIX_W, 3 * MIX_W)
PAD_IDX = tuple(sum(PAD_SIZES[:i + 1]) for i in range(len(PAD_SIZES) - 1))
IN_PAD = sum(PAD_SIZES)
IN_TN = 640


def kernel(x, c, ctx, c_ctx, w_mod, b_mod, w_in, hy_conv, hy_w1, hy_b1, hy_w2, hy_b2, hy_w3, hy_freq,
           hy_deltas, hy_skip, gdn_conv, gdn_a_log, gdn_dt_bias, gdn_norm, diff_lam, diff_norm, na_rpb,
           branch_proj, w_out, ln_g, ln_b, router_w, router_b, exp_w1, exp_w3, exp_w2):
    B, n, D = x.shape
    m = ctx.shape[1]
    depth = w_mod.shape[0]
    dn_alpha = (2 * depth) ** 0.25
    ang_r, ang_c = axial_rope_angles(n)
    zero_state = jnp.zeros((B, GDN_H, HEAD_DIM, HEAD_DIM), jnp.float32)
    bf16 = jnp.bfloat16
    n_lat = B * n
    tok = jnp.concatenate([x.reshape(n_lat, D), ctx.reshape(B * m, D)], axis=0)
    for l in range(depth):
        ctx_out = l < depth - 1
        mx = jax.nn.silu(c) @ w_mod[l] + b_mod[l]
        mc = jax.nn.silu(c_ctx) @ w_mod[l] + b_mod[l]
        one_plus = jnp.array([0.0, 1.0, 0.0, 0.0, 1.0, 0.0], jnp.float32)[:, None]
        mod_x = mx.reshape(B, 6, D) + one_plus
        mod_c = mc.reshape(1, 6, D) + one_plus
        mod = jnp.concatenate([jnp.repeat(mod_x, n // MERGE_TILE, axis=0),
                               jnp.repeat(mod_c, B * m // MERGE_TILE, axis=0)], axis=0)

        proj = ln_mod_matmul(tok, mod, _pad_w_in(w_in[l]).astype(bf16), IN_TN)
        px = proj[:n_lat].reshape(B, n, IN_PAD)
        pc = proj[n_lat:].reshape(B, m, IN_PAD)
        _, hy_x, gqkv_x, gz_x, gab_x, dqkv_x, nqkv_x = jnp.split(px, PAD_IDX, axis=-1)
        gate_c, hy_c, gqkv_c, gz_c, gab_c, dqkv_c, nqkv_c = jnp.split(pc, PAD_IDX, axis=-1)
        ga_x, gb_x = gab_x[..., :2 * GDN_H], gab_x[..., 2 * GDN_H:4 * GDN_H]
        ga_c, gb_c = gab_c[..., :2 * GDN_H], gab_c[..., 2 * GDN_H:4 * GDN_H]
        filt = (hy_w1[l], hy_b1[l], hy_w2[l], hy_b2[l], hy_w3[l], hy_freq[l], hy_deltas[l])

        ya_x = hyena_mix(hy_x, hy_conv[l], hy_skip[l], hyena_filters(n, *filt))

        gdn_p = (gdn_conv[l], gdn_a_log[l], gdn_dt_bias[l], gdn_norm[l])
        yb_c, st_c = gdn_mix(gqkv_c, gz_c, ga_c, gb_c, *gdn_p, (zero_state, zero_state), ctx_out)
        yb_x, _ = gdn_mix(gqkv_x, gz_x, ga_x, gb_x, *gdn_p, st_c, True)

        lq1, lk1, lq2, lk2 = diff_lam[l]
        lam_init = 0.8 - 0.6 * math.exp(-0.3 * l)
        lam = jnp.exp(jnp.sum(lq1 * lk1)) - jnp.exp(jnp.sum(lq2 * lk2)) + lam_init
        yc_x = diff_branch(dqkv_x, dqkv_c, lam, diff_norm[l], lam_init, ang_r, ang_c)
        yd_x = na_branch(nqkv_x, nqkv_c, na_rpb[l])

        ys = [ya_x, yb_x, yc_x, yd_x]
        ys = [t.reshape(n_lat, MIX_W) for t in ys]
        if ctx_out:
            dq_c, dk_c, dv_c = split_diff(dqkv_c)
            nq_c, nk_c, nv_c = [t.reshape(B, m, NA_H, HEAD_DIM) for t in jnp.split(nqkv_c, 3, axis=-1)]
            ya_c = hyena_mix(hy_c, hy_conv[l], hy_skip[l], hyena_filters(m, *filt))
            yc_c = diff_finish(diff_attend(dq_c, dk_c, dv_c, lam), diff_norm[l], lam_init)
            yd_c = dense_attend(nq_c, nk_c, nv_c)
            ys_c = [t.reshape(B * m, MIX_W) for t in (ya_c, yb_c, yc_c, yd_c)]
            ys = [jnp.concatenate([a, b], axis=0) for a, b in zip(ys, ys_c)]
        else:
            tok = tok[:n_lat]
        tok = merge_residual_ln(tok, ys, proj, mod, branch_proj[l].astype(bf16), w_out[l].astype(bf16),
                                ln_g[l, 0], ln_b[l, 0], dn_alpha)

        n_tok = tok.shape[0]
        h2, idx, wts = moe_route(tok, mod, router_w, router_b)
        buf_tok, dest, blk_exp, n_used = moe_dispatch(idx, n_tok)
        yb = expert_ffn(blk_exp, n_used, jnp.take(h2, buf_tok, axis=0),
                        exp_w1[l].astype(bf16), exp_w3[l].astype(bf16), exp_w2[l].astype(bf16))
        dest = dest.reshape(n_tok, TOP_K)
        y = (wts[0][:, None] * jnp.take(yb, dest[:, 0], axis=0)
             + wts[1][:, None] * jnp.take(yb, dest[:, 1], axis=0))
        tok = residual_ln(tok, y, mod, ln_g[l, 1], ln_b[l, 1], dn_alpha, 5)
    return tok[:n_lat].reshape(B, n, D)
```

```python
import functools
import math

import jax
import jax.numpy as jnp
from jax import lax
from jax.experimental import pallas as pl
from jax.experimental.pallas import tpu as pltpu

D_MODEL = 1024
GRID_W = 64
HEAD_DIM = 64
MIX_W = D_MODEL // 2
N_BRANCH = 4
HY_W = MIX_W
HY_ORDER = 2
HY_BANDS = 16
HY_MOD_SHIFT = 0.05
GDN_H = MIX_W // HEAD_DIM
GDN_CHUNK = 64
DIFF_VD = 2 * HEAD_DIM
DIFF_H = MIX_W // DIFF_VD
NA_H = MIX_W // HEAD_DIM
NA_ROWS = 8
NA_COLS = 16
N_EXPERTS = 16
N_GROUPS = 4
TOP_K = 2
MOE_BLOCK = 128
Q_BLOCK = 128
ROPE_THETA = 10000.0
LN_EPS = 1e-5
RMS_EPS = 1e-6
SPLIT_SIZES = (3 * HY_W, 3 * MIX_W, MIX_W, 2 * GDN_H, 2 * GDN_H, 3 * MIX_W, 3 * MIX_W, N_BRANCH * D_MODEL)
SPLIT_IDX = tuple(sum(SPLIT_SIZES[:i + 1]) for i in range(len(SPLIT_SIZES) - 1))

LANE = 128
ROW_TILE = 512
MERGE_TILE = 256
VMEM_LIMIT = 48 * 1024 * 1024


def _ln_mod_matmul_kernel(x_ref, mod_ref, w_ref, o_ref, h_ref):
    @pl.when(pl.program_id(1) == 0)
    def _():
        x = x_ref[...]
        mu = jnp.mean(x, axis=-1, keepdims=True)
        xc = x - mu
        var = jnp.mean(xc * xc, axis=-1, keepdims=True)
        y = xc * lax.rsqrt(var + LN_EPS)
        h_ref[...] = (y * mod_ref[0, 1:2, :] + mod_ref[0, 0:1, :]).astype(h_ref.dtype)

    o_ref[...] = jnp.dot(h_ref[...], w_ref[...], preferred_element_type=jnp.float32).astype(o_ref.dtype)


def ln_mod_matmul(x, mod, w, tn, out_dtype=jnp.float32):
    T, D = x.shape
    N = w.shape[1]
    return pl.pallas_call(
        _ln_mod_matmul_kernel,
        grid=(T // ROW_TILE, N // tn),
        in_specs=[
            pl.BlockSpec((ROW_TILE, D), lambda i, j: (i, 0)),
            pl.BlockSpec((1, 6, D), lambda i, j: (i * (ROW_TILE // MERGE_TILE), 0, 0)),
            pl.BlockSpec((D, tn), lambda i, j: (0, j)),
        ],
        out_specs=pl.BlockSpec((ROW_TILE, tn), lambda i, j: (i, j)),
        out_shape=jax.ShapeDtypeStruct((T, N), out_dtype),
        scratch_shapes=[pltpu.VMEM((ROW_TILE, D), jnp.bfloat16)],
        compiler_params=pltpu.CompilerParams(
            dimension_semantics=("arbitrary", "arbitrary"), vmem_limit_bytes=VMEM_LIMIT),
        name="ln_mod_matmul",
    )(x, mod, w)


def _layer_norm_rows(x):
    mu = jnp.mean(x, axis=-1, keepdims=True)
    xc = x - mu
    var = jnp.mean(xc * xc, axis=-1, keepdims=True)
    return xc * lax.rsqrt(var + LN_EPS)


_NT_DIMS = (((1,), (1,)), ((), ()))


DIFF_TQ = 256
DIFF_TK = 384
DIFF_ROWS = 32


def _diff_attn_kernel(lam_ref, q_ref, k_ref, vt_ref, nw_ref, o_ref, s_ref, *, out_scale):
    tq = q_ref.shape[1]
    n_tiles = k_ref.shape[3] // DIFF_TK
    q = q_ref[0]
    qc = (q[:, :HEAD_DIM], q[:, HEAD_DIM:])

    sub = 8
    fold = lambda t, op: op(t.reshape(t.shape[0] // sub, sub, t.shape[1]), axis=0)

    def scores(j, slot):
        start = pl.multiple_of(j * DIFF_TK, DIFF_TK)
        mx = []
        for c in range(2):
            k = k_ref[0, 0, c, pl.ds(start, DIFF_TK), :]
            st = lax.dot_general(k, qc[c], _NT_DIMS, preferred_element_type=jnp.float32)
            s_ref[slot, c] = st
            mx.append(fold(st, jnp.max))
        return tuple(mx)

    def softmax_pv(j, slot, mx, carry):
        new, ps = [], []
        for c in range(2):
            m_prev, l_prev, acc = carry[c]
            m_new = jnp.maximum(m_prev, jnp.max(mx[c], axis=0, keepdims=True))
            alpha = jnp.exp2(m_prev - m_new)
            psum, chunks = None, []
            for r in range(DIFF_TK // DIFF_ROWS):
                p = jnp.exp2(s_ref[slot, c, r * DIFF_ROWS:(r + 1) * DIFF_ROWS, :] - m_new)
                part = fold(p, jnp.sum)
                psum = part if psum is None else psum + part
                chunks.append(p.astype(jnp.bfloat16))
            new.append((m_new, alpha * l_prev + jnp.sum(psum, axis=0, keepdims=True), alpha * acc))
            ps.append(jnp.concatenate(chunks, axis=0))
        pv = jnp.dot(vt_ref[0, 0, j], jnp.concatenate(ps, axis=1), preferred_element_type=jnp.float32)
        return tuple((new[c][0], new[c][1], new[c][2] + pv[:, c * tq:(c + 1) * tq]) for c in range(2))

    def pair(jj, state):
        carry, mx0 = state
        j0 = 2 * jj
        mx1 = scores(j0 + 1, 1)
        carry = softmax_pv(j0, 0, mx0, carry)
        mx0 = scores(j0 + 2, 0)
        return softmax_pv(j0 + 1, 1, mx1, carry), mx0

    carry = tuple((jnp.full((1, tq), -jnp.inf, jnp.float32), jnp.zeros((1, tq), jnp.float32),
                   jnp.zeros((DIFF_VD, tq), jnp.float32)) for _ in range(2))
    mx0 = scores(0, 0)
    n_pairs = (n_tiles - 1) // 2
    carry, mx0 = lax.fori_loop(0, n_pairs, pair, (carry, mx0))
    if n_tiles % 2 == 0:
        mx1 = scores(n_tiles - 1, 1)
        carry = softmax_pv(n_tiles - 2, 0, mx0, carry)
        carry = softmax_pv(n_tiles - 1, 1, mx1, carry)
    else:
        carry = softmax_pv(n_tiles - 1, 0, mx0, carry)
    (_, l0, a0), (_, l1, a1) = carry
    o = a0 / l0 - lam_ref[0] * (a1 / l1)
    o = o * lax.rsqrt(jnp.mean(o * o, axis=0, keepdims=True) + RMS_EPS) * (nw_ref[...] * out_scale)
    o_ref[0] = o.T.astype(o_ref.dtype)


def diff_attention(q, k, vt, lam, norm_w, out_scale, out_dtype=jnp.float32):
    B, nq, W = q.shape
    nk = k.shape[3]
    assert nq % DIFF_TQ == 0 and nk % DIFF_TK == 0
    H = W // DIFF_VD
    return pl.pallas_call(
        functools.partial(_diff_attn_kernel, out_scale=out_scale),
        grid=(B, H, nq // DIFF_TQ),
        in_specs=[
            pl.BlockSpec(memory_space=pltpu.SMEM),
            pl.BlockSpec((1, DIFF_TQ, DIFF_VD), lambda b, h, i: (b, i, h)),
            pl.BlockSpec((1, 1, 2, nk, HEAD_DIM), lambda b, h, i: (b, h, 0, 0, 0)),
            pl.BlockSpec((1, 1, nk // DIFF_TK, DIFF_VD, DIFF_TK), lambda b, h, i: (b, h, 0, 0, 0)),
            pl.BlockSpec((DIFF_VD, 1), lambda b, h, i: (0, 0)),
        ],
        out_specs=pl.BlockSpec((1, DIFF_TQ, DIFF_VD), lambda b, h, i: (b, i, h)),
        out_shape=jax.ShapeDtypeStruct((B, nq, W), out_dtype),
        scratch_shapes=[pltpu.VMEM((2, 2, DIFF_TK, DIFF_TQ), jnp.float32)],
        compiler_params=pltpu.CompilerParams(
            dimension_semantics=("arbitrary",) * 3, vmem_limit_bytes=VMEM_LIMIT),
        name="diff_attention",
    )(lam.reshape(1).astype(jnp.float32), q, k, vt, norm_w.reshape(DIFF_VD, 1))


NA_QROWS = 8
NA_KROWS = 16
NA_NEG = -1e30


def _na_kernel(q_ref, k_ref, v_ref, kc_ref, vc_ref, tab_ref, o_ref, *, n_kblocks):
    g = pl.program_id(2)
    kb = (NA_KROWS // 4) * GRID_W
    base = jnp.clip(2 * g - 1, 0, n_kblocks - 4)
    start = pl.multiple_of(base * kb, kb)
    nwin = NA_KROWS * GRID_W
    q = q_ref[0]
    kw = k_ref[0, pl.ds(start, nwin), :]
    vw = v_ref[0, pl.ds(start, nwin), :]
    kc = kc_ref[0]
    vc = vc_ref[0]
    outs = []
    for hh in range(2):
        sl = slice(hh * HEAD_DIM, (hh + 1) * HEAD_DIM)
        qh = q[:, sl]
        s = lax.dot_general(qh, kw[:, sl], _NT_DIMS, preferred_element_type=jnp.float32) + tab_ref[0, hh]
        sc = lax.dot_general(qh, kc[:, sl], _NT_DIMS, preferred_element_type=jnp.float32)
        m = jnp.maximum(jnp.max(s, axis=-1, keepdims=True), jnp.max(sc, axis=-1, keepdims=True))
        p = jnp.exp(s - m)
        pc = jnp.exp(sc - m)
        l = jnp.sum(p, axis=-1, keepdims=True) + jnp.sum(pc, axis=-1, keepdims=True)
        o = (jnp.dot(p.astype(vw.dtype), vw[:, sl], preferred_element_type=jnp.float32)
             + jnp.dot(pc.astype(vc.dtype), vc[:, sl], preferred_element_type=jnp.float32))
        outs.append(o / l)
    o_ref[0] = jnp.concatenate(outs, axis=-1).astype(o_ref.dtype)


def na_bias_tables(rpb, R):
    H = rpb.shape[0]
    G = R // NA_QROWS
    cols = jnp.arange(GRID_W)
    cstart = jnp.clip(cols - NA_COLS // 2, 0, GRID_W - NA_COLS)
    col_ok = (cols[None, :] >= cstart[:, None]) & (cols[None, :] < cstart[:, None] + NA_COLS)
    dc = jnp.clip(cols[None, :] - cols[:, None], -(NA_COLS - 1), NA_COLS - 1) + (NA_COLS - 1)
    rpb_c = rpb[:, :, dc]
    tabs = []
    for g in (0, 1, G - 1):
        base = min(max(2 * g - 1, 0), R // 4 - 4)
        r = NA_QROWS * g + jnp.arange(NA_QROWS)
        rs = jnp.clip(r - NA_ROWS // 2, 0, R - NA_ROWS)
        kr = 4 * base + jnp.arange(NA_KROWS)
        valid = (kr[None, :] >= rs[:, None]) & (kr[None, :] < rs[:, None] + NA_ROWS)
        dr = jnp.clip(kr[None, :] - r[:, None] + (NA_ROWS - 1), 0, 2 * NA_ROWS - 2)
        tab = rpb_c[:, dr]
        ok = valid[None, :, :, None, None] & col_ok[None, None, None, :, :]
        tab = jnp.where(ok, tab, NA_NEG)
        tabs.append(jnp.transpose(tab, (0, 1, 3, 2, 4)).reshape(H, NA_QROWS * GRID_W, NA_KROWS * GRID_W))
    return jnp.stack(tabs, axis=0)


def na_attention(q, k, v, kc, vc, tabs, out_dtype=jnp.float32):
    B, n, W = q.shape
    m = kc.shape[1]
    tq = NA_QROWS * GRID_W
    G = n // tq
    n_kblocks = n // ((NA_KROWS // 4) * GRID_W)

    def tab_index(b, hp, g):
        return (jnp.where(g == 0, 0, jnp.where(g == G - 1, 2, 1)), hp, 0, 0)

    return pl.pallas_call(
        functools.partial(_na_kernel, n_kblocks=n_kblocks),
        grid=(B, W // LANE, G),
        in_specs=[
            pl.BlockSpec((1, tq, LANE), lambda b, hp, g: (b, g, hp)),
            pl.BlockSpec((1, n, LANE), lambda b, hp, g: (b, 0, hp)),
            pl.BlockSpec((1, n, LANE), lambda b, hp, g: (b, 0, hp)),
            pl.BlockSpec((1, m, LANE), lambda b, hp, g: (b, 0, hp)),
            pl.BlockSpec((1, m, LANE), lambda b, hp, g: (b, 0, hp)),
            pl.BlockSpec((1, 2, tq, NA_KROWS * GRID_W), tab_index),
        ],
        out_specs=pl.BlockSpec((1, tq, LANE), lambda b, hp, g: (b, g, hp)),
        out_shape=jax.ShapeDtypeStruct((B, n, W), out_dtype),
        compiler_params=pltpu.CompilerParams(
            dimension_semantics=("arbitrary",) * 3, vmem_limit_bytes=VMEM_LIMIT),
        name="na_attention",
    )(q, k, v, kc, vc, tabs)


def _merge_kernel(x_ref, ya_ref, yb_ref, yc_ref, yd_ref, gc_ref, mod_ref, p_ref, wo_ref, lng_ref, lnb_ref,
                  o_ref, *, alpha):
    acc = None
    for mi, y_ref in enumerate((ya_ref, yb_ref, yc_ref, yd_ref)):
        br = jnp.dot(y_ref[...].astype(jnp.bfloat16), p_ref[mi], preferred_element_type=jnp.float32)
        gm = jax.nn.sigmoid(gc_ref[:, mi * D_MODEL:(mi + 1) * D_MODEL].astype(jnp.float32))
        acc = gm * br if acc is None else acc + gm * br
    mix = jnp.dot(acc.astype(jnp.bfloat16), wo_ref[...], preferred_element_type=jnp.float32)
    z = alpha * x_ref[...] + mod_ref[0, 2:3, :] * mix
    o_ref[...] = _layer_norm_rows(z) * lng_ref[...] + lnb_ref[...]


def merge_residual_ln(x, ys, gate_cols, mod, proj, w_o, ln_g, ln_b, alpha):
    T, D = x.shape
    row = lambda i: (i, 0)
    const2 = lambda i: (0, 0)
    return pl.pallas_call(
        functools.partial(_merge_kernel, alpha=alpha),
        grid=(T // MERGE_TILE,),
        in_specs=[pl.BlockSpec((MERGE_TILE, D), row)]
        + [pl.BlockSpec((MERGE_TILE, MIX_W), row)] * 4
        + [pl.BlockSpec((MERGE_TILE, N_BRANCH * D), row),
           pl.BlockSpec((1, 6, D), lambda i: (i, 0, 0)),
           pl.BlockSpec((N_BRANCH, MIX_W, D), lambda i: (0, 0, 0)),
           pl.BlockSpec((D, D), const2),
           pl.BlockSpec((1, D), const2),
           pl.BlockSpec((1, D), const2)],
        out_specs=pl.BlockSpec((MERGE_TILE, D), row),
        out_shape=jax.ShapeDtypeStruct((T, D), jnp.float32),
        compiler_params=pltpu.CompilerParams(dimension_semantics=("arbitrary",), vmem_limit_bytes=VMEM_LIMIT),
        name="merge_residual_ln",
    )(x, *ys, gate_cols, mod, proj, w_o, ln_g.reshape(1, D), ln_b.reshape(1, D))


def _route_kernel(x_ref, mod_ref, rw_ref, rb_ref, h_ref, idx_ref, wts_ref):
    h = (_layer_norm_rows(x_ref[...]) * mod_ref[0, 4:5, :] + mod_ref[0, 3:4, :]).astype(jnp.bfloat16)
    h_ref[...] = h
    logits = lax.dot_general(rw_ref[...], h, _NT_DIMS, preferred_element_type=jnp.float32)
    s = jax.nn.sigmoid(logits)
    sel = s + rb_ref[...]
    per = N_EXPERTS // N_GROUPS
    srow = [s[e:e + 1, :] for e in range(N_EXPERTS)]
    vrow = [sel[e:e + 1, :] for e in range(N_EXPERTS)]
    best = None
    for gi in range(N_GROUPS):
        grp = vrow[gi * per:(gi + 1) * per]
        gs = None
        for a in range(per):
            for b in range(a + 1, per):
                ps = grp[a] + grp[b]
                gs = ps if gs is None else jnp.maximum(gs, ps)
        if best is None:
            best, bg = gs, jnp.zeros(gs.shape, jnp.int32)
        else:
            upd = gs > best
            bg = jnp.where(upd, gi, bg)
            best = jnp.where(upd, gs, best)
    cv, cs = [], []
    for j in range(per):
        v_j, s_j = vrow[j], srow[j]
        for gi in range(1, N_GROUPS):
            v_j = jnp.where(bg == gi, vrow[gi * per + j], v_j)
            s_j = jnp.where(bg == gi, srow[gi * per + j], s_j)
        cv.append(v_j)
        cs.append(s_j)

    def first_argmax(vals):
        bv, bi = vals[0], jnp.zeros(vals[0].shape, jnp.int32)
        for j in range(1, per):
            upd = vals[j] > bv
            bi = jnp.where(upd, j, bi)
            bv = jnp.where(upd, vals[j], bv)
        return bi

    i1 = first_argmax(cv)
    i2 = first_argmax([jnp.where(i1 == j, -jnp.inf, cv[j]) for j in range(per)])
    w1 = cs[0]
    w2 = cs[0]
    for j in range(1, per):
        w1 = jnp.where(i1 == j, cs[j], w1)
        w2 = jnp.where(i2 == j, cs[j], w2)
    tot = w1 + w2
    idx_ref[...] = jnp.concatenate([bg * per + i1, bg * per + i2], axis=0)
    wts_ref[...] = jnp.concatenate([w1 / tot, w2 / tot], axis=0)


def moe_route(x, mod, router_w, router_b):
    T, D = x.shape
    return pl.pallas_call(
        _route_kernel,
        grid=(T // MERGE_TILE,),
        in_specs=[pl.BlockSpec((MERGE_TILE, D), lambda i: (i, 0)),
                  pl.BlockSpec((1, 6, D), lambda i: (i, 0, 0)),
                  pl.BlockSpec((N_EXPERTS, D), lambda i: (0, 0)),
                  pl.BlockSpec((N_EXPERTS, 1), lambda i: (0, 0))],
        out_specs=[pl.BlockSpec((MERGE_TILE, D), lambda i: (i, 0)),
                   pl.BlockSpec((TOP_K, MERGE_TILE), lambda i: (0, i)),
                   pl.BlockSpec((TOP_K, MERGE_TILE), lambda i: (0, i))],
        out_shape=[jax.ShapeDtypeStruct((T, D), jnp.bfloat16),
                   jax.ShapeDtypeStruct((TOP_K, T), jnp.int32),
                   jax.ShapeDtypeStruct((TOP_K, T), jnp.float32)],
        compiler_params=pltpu.CompilerParams(dimension_semantics=("arbitrary",), vmem_limit_bytes=VMEM_LIMIT),
        name="moe_route",
    )(x, mod, router_w.T.astype(jnp.bfloat16), router_b.reshape(N_EXPERTS, 1).astype(jnp.float32))


EXPERT_BLOCK = 256


def _expert_kernel(be_ref, nu_ref, x_ref, w1_ref, w3_ref, w2_ref, o_ref):
    i = pl.program_id(0)

    @pl.when(i < nu_ref[0])
    def _():
        x = x_ref[...]
        a = jnp.dot(x, w1_ref[0], preferred_element_type=jnp.float32)
        b = jnp.dot(x, w3_ref[0], preferred_element_type=jnp.float32)
        hmid = (a * jax.nn.sigmoid(a) * b).astype(jnp.bfloat16)
        o_ref[...] = jnp.dot(hmid, w2_ref[0], preferred_element_type=jnp.float32).astype(o_ref.dtype)

    @pl.when(i >= nu_ref[0])
    def _():
        o_ref[...] = jnp.zeros(o_ref.shape, o_ref.dtype)


def expert_ffn(blk_exp, n_used, xb, w1, w3, w2, out_dtype=jnp.float32):
    cap, D = xb.shape
    F = w1.shape[-1]
    n_blk = cap // EXPERT_BLOCK
    return pl.pallas_call(
        _expert_kernel,
        grid_spec=pltpu.PrefetchScalarGridSpec(
            num_scalar_prefetch=2,
            grid=(n_blk,),
            in_specs=[pl.BlockSpec((EXPERT_BLOCK, D), lambda i, be, nu: (i, 0)),
                      pl.BlockSpec((1, D, F), lambda i, be, nu: (be[i], 0, 0)),
                      pl.BlockSpec((1, D, F), lambda i, be, nu: (be[i], 0, 0)),
                      pl.BlockSpec((1, F, D), lambda i, be, nu: (be[i], 0, 0))],
            out_specs=pl.BlockSpec((EXPERT_BLOCK, D), lambda i, be, nu: (i, 0)),
        ),
        out_shape=jax.ShapeDtypeStruct((cap, D), out_dtype),
        compiler_params=pltpu.CompilerParams(dimension_semantics=("arbitrary",), vmem_limit_bytes=VMEM_LIMIT),
        name="expert_ffn",
    )(blk_exp, n_used, xb, w1, w3, w2)


def _gdn_kernel(qf_ref, kf_ref, vf_ref, gf_ref, bf_ref, qb_ref, kb_ref, vb_ref, gb_ref, bb_ref,
                of_ref, ob_ref, s_ref):
    C = GDN_CHUNK
    nb = qf_ref.shape[0]

    @pl.when(pl.program_id(0) == 0)
    def _():
        s_ref[...] = jnp.zeros(s_ref.shape, jnp.float32)

    bf = lambda t: t.astype(jnp.bfloat16)
    mm = lambda a, b: jnp.dot(bf(a), bf(b), preferred_element_type=jnp.float32)
    nt = lambda a, b: lax.dot_general(bf(a), bf(b), _NT_DIMS, preferred_element_type=jnp.float32)
    tn = lambda a, b: lax.dot_general(bf(a), bf(b), (((0,), (0,)), ((), ())), preferred_element_type=jnp.float32)
    col = lambda t, h: t[:, h:h + 1]
    rel = lax.broadcasted_iota(jnp.int32, (C, C), 0) - lax.broadcasted_iota(jnp.int32, (C, C), 1)

    qs, ks, vs, decay, beta_c, e_gc, e_rest, e_tot, strict = [], [], [], [], [], [], [], [], []
    for bi in range(nb):
        for d, refs in enumerate(((qf_ref, kf_ref, vf_ref, gf_ref, bf_ref), (qb_ref, kb_ref, vb_ref, gb_ref, bb_ref))):
            q_ref, k_ref, v_ref, g_ref, b_ref = refs
            incl = rel >= 0 if d == 0 else rel <= 0
            g = g_ref[bi, 0]
            beta = b_ref[bi, 0]
            gc = jnp.dot(incl.astype(jnp.float32), g, precision=lax.Precision.HIGHEST,
                         preferred_element_type=jnp.float32)
            gc_t = gc.T
            tot = jnp.sum(g, axis=0, keepdims=True)
            eg, er, et = jnp.exp(gc), jnp.exp(tot - gc), jnp.exp(tot)
            q, k, v = q_ref[bi], k_ref[bi], v_ref[bi]
            for h in range(GDN_H):
                sl = slice(h * HEAD_DIM, (h + 1) * HEAD_DIM)
                qs.append(q[:, sl])
                ks.append(k[:, sl])
                vs.append(v[:, sl])
                decay.append(jnp.where(incl, jnp.exp(col(gc, h) - gc_t[h:h + 1, :]), 0.0))
                strict.append(rel > 0 if d == 0 else rel < 0)
                beta_c.append(col(beta, h))
                e_gc.append(col(eg, h))
                e_rest.append(col(er, h))
                e_tot.append(col(et, h))
    chains = range(len(qs))
    kb = [ks[c] * beta_c[c] for c in chains]
    a_mat = [jnp.where(strict[c], nt(kb[c], ks[c]) * decay[c], 0.0) for c in chains]
    intra = [nt(qs[c], ks[c]) * decay[c] for c in chains]
    ri = lax.broadcasted_iota(jnp.int32, (C, C), 0)
    ci = lax.broadcasted_iota(jnp.int32, (C, C), 1)

    def level_mask(s, upper):
        sh = s.bit_length() - 1
        same = (ri >> (sh + 1)) == (ci >> (sh + 1))
        r_hi = ((ri >> sh) & 1) == 1
        c_hi = ((ci >> sh) & 1) == 1
        return same & c_hi & ~r_hi if upper else same & r_hi & ~c_hi

    masks = {s: (level_mask(s, False), level_mask(s, True)) for s in (1, 2, 4, 8, 16, 32)}
    upper = [(c // GDN_H) % 2 == 1 for c in chains]
    eye = (ri == ci).astype(jnp.float32)
    t_inv = [eye - jnp.where(masks[1][upper[c]], a_mat[c], 0.0) for c in chains]
    for s in (2, 4, 8, 16, 32):
        tm = [mm(t_inv[c], jnp.where(masks[s][upper[c]], a_mat[c], 0.0)) for c in chains]
        t_inv = [t_inv[c] - mm(tm[c], t_inv[c]) for c in chains]
    xs = [mm(t_inv[c], jnp.concatenate([vs[c] * beta_c[c], kb[c] * e_gc[c]], axis=1)) for c in chains]
    st = [s_ref[c] for c in chains]
    v_new = [xs[c][:, :HEAD_DIM] - mm(xs[c][:, HEAD_DIM:], st[c]) for c in chains]
    o_st = [mm(qs[c] * e_gc[c], st[c]) for c in chains]
    o_in = [mm(intra[c], v_new[c]) for c in chains]
    s_up = [tn(ks[c] * e_rest[c], v_new[c]) for c in chains]
    for c in chains:
        s_ref[c] = st[c] * e_tot[c] + s_up[c]
    outs = [o_st[c] + o_in[c] for c in chains]
    for bi in range(nb):
        for d, o_ref in enumerate((of_ref, ob_ref)):
            c0 = (bi * 2 + d) * GDN_H
            o_ref[bi] = jnp.concatenate(outs[c0:c0 + GDN_H], axis=1)


def gdn_scan(q, k, v, g, beta, n_ctx_chunks):
    B, Tt, W = q.shape
    C = GDN_CHUNK
    NC = Tt // C

    def bwd(s):
        return jnp.where(s < n_ctx_chunks, n_ctx_chunks - 1 - s, NC - 1 - (s - n_ctx_chunks))

    tok_f = pl.BlockSpec((B, C, W), lambda s: (0, s, 0))
    tok_b = pl.BlockSpec((B, C, W), lambda s: (0, bwd(s), 0))
    gate_f = pl.BlockSpec((B, 1, C, LANE), lambda s: (0, 0, s, 0))
    gate_b = pl.BlockSpec((B, 1, C, LANE), lambda s: (0, 1, bwd(s), 0))
    return pl.pallas_call(
        _gdn_kernel,
        grid=(NC,),
        in_specs=[tok_f, tok_f, tok_f, gate_f, gate_f, tok_b, tok_b, tok_b, gate_b, gate_b],
        out_specs=[tok_f, tok_b],
        out_shape=[jax.ShapeDtypeStruct((B, Tt, W), jnp.float32)] * 2,
        scratch_shapes=[pltpu.VMEM((B * 2 * GDN_H, HEAD_DIM, HEAD_DIM), jnp.float32)],
        compiler_params=pltpu.CompilerParams(dimension_semantics=("arbitrary",), vmem_limit_bytes=VMEM_LIMIT),
        name="gdn_scan",
    )(q, k, v, g, beta, q, k, v, g, beta)


def gdn_pre(qkv, a, b, conv_w, a_log, dt_bias):
    B, T, _ = qkv.shape
    u = jax.nn.silu(centred_dwconv(qkv, conv_w))
    q, k, v = [t.reshape(B, T, GDN_H, HEAD_DIM) for t in jnp.split(u, 3, axis=-1)]
    q = (l2_normalize(q) * HEAD_DIM ** -0.5).reshape(B, T, MIX_W)
    k = l2_normalize(k).reshape(B, T, MIX_W)
    g = -jnp.exp(a_log) * jax.nn.softplus(a.reshape(B, T, 2, GDN_H) + dt_bias)
    beta = jax.nn.sigmoid(b.reshape(B, T, 2, GDN_H))
    lanes = lambda t: jnp.pad(jnp.swapaxes(t, 1, 2), ((0, 0), (0, 0), (0, 0), (0, LANE - GDN_H)))
    return q, k, v.reshape(B, T, MIX_W), lanes(g), lanes(beta)


def gdn_branch(gqkv_x, gz_x, ga_x, gb_x, gqkv_c, gz_c, ga_c, gb_c, conv_w, a_log, dt_bias, norm_w):
    B, n, _ = gqkv_x.shape
    m = gqkv_c.shape[1]
    pre_c = gdn_pre(gqkv_c, ga_c, gb_c, conv_w, a_log, dt_bias)
    pre_x = gdn_pre(gqkv_x, ga_x, gb_x, conv_w, a_log, dt_bias)
    q, k, v = [jnp.concatenate([pc, px], axis=1) for pc, px in zip(pre_c[:3], pre_x[:3])]
    g, beta = [jnp.concatenate([pc, px], axis=2) for pc, px in zip(pre_c[3:], pre_x[3:])]
    o_f, o_b = gdn_scan(q, k, v, g, beta, m // GDN_CHUNK)
    o = (o_f + o_b).reshape(B, m + n, GDN_H, HEAD_DIM)
    z = jnp.concatenate([gz_c, gz_x], axis=1).reshape(B, m + n, GDN_H, HEAD_DIM)
    y = (rms_norm(o, norm_w) * jax.nn.silu(z)).reshape(B, m + n, MIX_W)
    return y[:, m:], y[:, :m]


FFT_R = 128
FFT_COLS = 4096
FFT_K1_STEP = 2


def _dft_tables(L):
    R = FFT_R
    N = R * R
    half = L // R
    idx = jnp.arange(R, dtype=jnp.int32)
    ang1 = (-2.0 * math.pi / R) * ((idx[:, None] * idx[None, :]) % R).astype(jnp.float32)
    fr, fi = jnp.cos(ang1), jnp.sin(ang1)
    blk = lambda re, im: jnp.concatenate([jnp.concatenate([re, -im], axis=1),
                                          jnp.concatenate([im, re], axis=1)], axis=0)
    m1 = blk(fr[:, :half], fi[:, :half])
    m1_real = jnp.concatenate([fr, fi], axis=0)
    m3 = blk(fr.T[:half], -fi.T[:half]) * (1.0 / N)
    k = idx[:, None, None] + R * idx[None, :, None]
    ang2 = (-2.0 * math.pi / N) * ((idx[None, None, :] * k) % N).astype(jnp.float32)
    gr, gi = jnp.cos(ang2), jnp.sin(ang2)
    g_fwd = jnp.concatenate([jnp.concatenate([gr, -gi], axis=2),
                             jnp.concatenate([gi, gr], axis=2)], axis=1)
    g_inv = jnp.swapaxes(g_fwd, 1, 2)
    bf = lambda t: t.astype(jnp.bfloat16)
    return bf(m1), bf(m1_real), bf(m3), bf(g_fwd), bf(g_inv)


def _colmm_kernel(m_ref, x_ref, o_ref):
    o_ref[...] = jnp.dot(m_ref[...], x_ref[...].astype(jnp.bfloat16),
                         preferred_element_type=jnp.float32).astype(o_ref.dtype)


def colmm(mat, x, out_dtype=jnp.bfloat16):
    M, K = mat.shape
    n_cols = x.shape[1]
    return pl.pallas_call(
        _colmm_kernel,
        grid=(n_cols // FFT_COLS,),
        in_specs=[pl.BlockSpec((M, K), lambda j: (0, 0)), pl.BlockSpec((K, FFT_COLS), lambda j: (0, j))],
        out_specs=pl.BlockSpec((M, FFT_COLS), lambda j: (0, j)),
        out_shape=jax.ShapeDtypeStruct((M, n_cols), out_dtype),
        compiler_params=pltpu.CompilerParams(dimension_semantics=("arbitrary",), vmem_limit_bytes=VMEM_LIMIT),
        name="hyena_colmm",
    )(mat, x)


def _colmm_gate_kernel(m_ref, r_ref, z_ref, x_ref, skip_ref, o_ref):
    y = jnp.dot(m_ref[...], r_ref[...], preferred_element_type=jnp.float32)
    o_ref[...] = x_ref[...] * (y + z_ref[...] * skip_ref[...])


def colmm_gate(mat, r, z, gate, skip_cols):
    M, K = mat.shape
    n_cols = r.shape[1]
    col = lambda j: (0, j)
    return pl.pallas_call(
        _colmm_gate_kernel,
        grid=(n_cols // FFT_COLS,),
        in_specs=[pl.BlockSpec((M, K), lambda j: (0, 0)), pl.BlockSpec((K, FFT_COLS), col),
                  pl.BlockSpec((M, FFT_COLS), col), pl.BlockSpec((M, FFT_COLS), col),
                  pl.BlockSpec((1, FFT_COLS), col)],
        out_specs=pl.BlockSpec((M, FFT_COLS), col),
        out_shape=jax.ShapeDtypeStruct((M, n_cols), jnp.float32),
        compiler_params=pltpu.CompilerParams(dimension_semantics=("arbitrary",), vmem_limit_bytes=VMEM_LIMIT),
        name="hyena_colmm_gate",
    )(mat, r, z, gate, skip_cols)


def _spectrum_kernel(p_ref, g_ref, o_ref):
    R = FFT_R
    for j in range(FFT_K1_STEP):
        p = jnp.concatenate([p_ref[0, j], p_ref[1, j]], axis=0)
        q = jnp.dot(g_ref[j], p, preferred_element_type=jnp.float32)
        o_ref[0, j] = q[:R]
        o_ref[1, j] = q[R:]


def _freq_kernel(p_ref, g_ref, gi_ref, h_ref, o_ref):
    R = FFT_R
    for j in range(FFT_K1_STEP):
        p = jnp.concatenate([p_ref[0, j], p_ref[1, j]], axis=0)
        q = jnp.dot(g_ref[j], p, preferred_element_type=jnp.float32)
        qr, qi = q[:R], q[R:]
        hr, hi = h_ref[0, j], h_ref[1, j]
        y = jnp.concatenate([qr * hr - qi * hi, qr * hi + qi * hr], axis=0).astype(jnp.bfloat16)
        r = jnp.dot(gi_ref[j], y, preferred_element_type=jnp.float32).astype(o_ref.dtype)
        o_ref[0, j] = r[:R]
        o_ref[1, j] = r[R:]


def hyena_spectrum(p, g_fwd):
    C = p.shape[-1]
    R, S = FFT_R, FFT_K1_STEP
    blk = pl.BlockSpec((2, S, R, LANE * 4), lambda i, c: (0, i, 0, c))
    return pl.pallas_call(
        _spectrum_kernel,
        grid=(R // S, C // (LANE * 4)),
        in_specs=[blk, pl.BlockSpec((S, 2 * R, 2 * R), lambda i, c: (i, 0, 0))],
        out_specs=blk,
        out_shape=jax.ShapeDtypeStruct((2, R, R, C), jnp.float32),
        compiler_params=pltpu.CompilerParams(dimension_semantics=("arbitrary",) * 2, vmem_limit_bytes=VMEM_LIMIT),
        name="hyena_spectrum",
    )(p, g_fwd)


def hyena_freq(p, g_fwd, g_inv, spec, order):
    C = p.shape[-1]
    R, S = FFT_R, FFT_K1_STEP
    blk = pl.BlockSpec((2, S, R, C), lambda i: (0, i, 0, 0))
    gspec = pl.BlockSpec((S, 2 * R, 2 * R), lambda i: (i, 0, 0))
    return pl.pallas_call(
        _freq_kernel,
        grid=(R // S,),
        in_specs=[blk, gspec, gspec, pl.BlockSpec((2, S, R, C), lambda i: (0, i, 0, order))],
        out_specs=blk,
        out_shape=jax.ShapeDtypeStruct((2, R, R, C), jnp.bfloat16),
        compiler_params=pltpu.CompilerParams(dimension_semantics=("arbitrary",), vmem_limit_bytes=VMEM_LIMIT),
        name="hyena_freq",
    )(p, g_fwd, g_inv, spec)


def hyena_taps(L, w1, b1, w2, b2, w3, freq, deltas):
    f32 = jnp.float32
    pos = jnp.arange(L, dtype=f32)
    pos_b = L - pos
    bands = jnp.linspace(1e-4, HY_BANDS - 1, HY_BANDS, dtype=f32)[None, :]

    def mlp(j, w3_dir, delta_dir):
        t = (j / (L - 1))[:, None]
        ang = (2.0 * math.pi / L) * j[:, None]
        feats = jnp.concatenate([t, jnp.cos(bands * ang), -jnp.sin(bands * ang)], axis=-1)
        h = jnp.sin(freq * (feats @ w1 + b1))
        h = jnp.sin(freq * (h @ w2 + b2))
        h = h @ w3_dir
        return h * (jnp.exp(-t * jnp.abs(delta_dir).reshape(1, -1)) + HY_MOD_SHIFT)

    w3d = w3.reshape(w3.shape[0], 2, HY_ORDER * HY_W)
    h_f = mlp(pos, w3d[:, 0], deltas[0])
    h_b = mlp(pos_b, w3d[:, 1], deltas[1])
    h_b = jnp.where((pos > 0)[:, None], h_b, 0.0)
    taps = jnp.concatenate([h_f, h_b], axis=0)
    return taps / jnp.sum(jnp.abs(taps), axis=0, keepdims=True)


def hyena_latent(proj, conv_w, skip, filt, tables):
    B, L, _ = proj.shape
    assert B == 2 and L % FFT_R == 0 and 2 * L == FFT_R * FFT_R
    R, C = FFT_R, HY_W
    m1, m1_real, m3, g_fwd, g_inv = tables
    taps = hyena_taps(L, *filt)
    spec = hyena_spectrum(colmm(m1_real, taps.reshape(R, R * HY_ORDER * C)).reshape(2, R, R, HY_ORDER * C), g_fwd)
    u = centred_dwconv(proj, conv_w)
    v, x1, x2 = jnp.split(u, 3, axis=-1)
    rows = B * (L // R)
    z = v.reshape(rows, R * C)
    for o, gate in enumerate((x1, x2)):
        p = colmm(m1, z).reshape(2, R, R, C)
        r = hyena_freq(p, g_fwd, g_inv, spec, o).reshape(2 * R, R * C)
        z = colmm_gate(m3, r, z, gate.reshape(rows, R * C), jnp.tile(skip[o], R).reshape(1, R * C))
    return z.reshape(B, L, C)


def _residual_ln_kernel(x_ref, y_ref, mod_ref, lng_ref, lnb_ref, o_ref, *, alpha, gate_row):
    z = alpha * x_ref[...] + mod_ref[0, gate_row:gate_row + 1, :] * y_ref[...].astype(jnp.float32)
    o_ref[...] = _layer_norm_rows(z) * lng_ref[...] + lnb_ref[...]


def residual_ln(x, y, mod, ln_g, ln_b, alpha, gate_row):
    T, D = x.shape
    row = lambda i: (i, 0)
    return pl.pallas_call(
        functools.partial(_residual_ln_kernel, alpha=alpha, gate_row=gate_row),
        grid=(T // MERGE_TILE,),
        in_specs=[pl.BlockSpec((MERGE_TILE, D), row), pl.BlockSpec((MERGE_TILE, D), row),
                  pl.BlockSpec((1, 6, D), lambda i: (i, 0, 0)),
                  pl.BlockSpec((1, D), lambda i: (0, 0)), pl.BlockSpec((1, D), lambda i: (0, 0))],
        out_specs=pl.BlockSpec((MERGE_TILE, D), row),
        out_shape=jax.ShapeDtypeStruct((T, D), jnp.float32),
        compiler_params=pltpu.CompilerParams(dimension_semantics=("arbitrary",), vmem_limit_bytes=VMEM_LIMIT),
        name="residual_ln",
    )(x, y, mod, ln_g.reshape(1, D), ln_b.reshape(1, D))


def moe_dispatch(idx, n_tok):
    n_slot = n_tok * TOP_K
    e_flat = idx.T.reshape(-1)
    onehot = (e_flat[:, None] == jnp.arange(N_EXPERTS, dtype=jnp.int32)[None, :]).astype(jnp.int32)
    csum = jnp.cumsum(onehot, axis=0)
    rank = jnp.sum(csum * onehot, axis=1) - 1
    counts = csum[-1]
    padded = (counts + EXPERT_BLOCK - 1) // EXPERT_BLOCK * EXPERT_BLOCK
    pend = jnp.cumsum(padded)
    pstart = pend - padded
    dest = jnp.sum(onehot * pstart[None, :], axis=1) + rank
    n_blk = -(-n_slot // EXPERT_BLOCK) + N_EXPERTS
    cap = n_blk * EXPERT_BLOCK
    t_flat = jnp.arange(n_slot, dtype=jnp.int32) // TOP_K
    buf_tok = jnp.zeros((cap,), jnp.int32).at[dest].set(t_flat)
    blk_exp = jnp.minimum(jnp.searchsorted(pend, jnp.arange(n_blk, dtype=jnp.int32) * EXPERT_BLOCK, side='right'),
                          N_EXPERTS - 1).astype(jnp.int32)
    n_used = (pend[-1] // EXPERT_BLOCK).astype(jnp.int32).reshape(1)
    return buf_tok, dest, blk_exp, n_used


def layer_norm(x, g=None, b=None):
    mu = jnp.mean(x, -1, keepdims=True)
    var = jnp.mean(jnp.square(x - mu), -1, keepdims=True)
    y = (x - mu) * lax.rsqrt(var + LN_EPS)
    if g is not None:
        y = y * g + b
    return y


def rms_norm(x, w):
    return x * lax.rsqrt(jnp.mean(jnp.square(x), -1, keepdims=True) + RMS_EPS) * w


def l2_normalize(x):
    return x * lax.rsqrt(jnp.sum(jnp.square(x), -1, keepdims=True) + RMS_EPS)


def modulate(x, shift, scale):
    return layer_norm(x) * (1.0 + scale) + shift


def centred_dwconv(u, w):
    K = w.shape[-1]
    T = u.shape[1]
    up = jnp.pad(u, ((0, 0), (K // 2, K // 2), (0, 0)))
    out = up[:, 0:T, :] * w[:, 0]
    for j in range(1, K):
        out = out + up[:, j:j + T, :] * w[:, j]
    return out


def hyena_filters(L, w1, b1, w2, b2, w3, freq, deltas):
    f32 = jnp.float32
    t = jnp.linspace(0.0, 1.0, L, dtype=f32)[:, None]
    ang = 2.0 * math.pi * jnp.arange(L, dtype=f32)[:, None] / L
    bands = jnp.linspace(1e-4, HY_BANDS - 1, HY_BANDS, dtype=f32)[None, :]
    feats = jnp.concatenate([t, jnp.cos(bands * ang), -jnp.sin(bands * ang)], axis=-1)
    h = jnp.sin(freq * (feats @ w1 + b1))
    h = jnp.sin(freq * (h @ w2 + b2))
    h = (h @ w3).reshape(L, 2, HY_ORDER, HY_W)
    window = jnp.exp(-t[:, :, None, None] * jnp.abs(deltas)) + HY_MOD_SHIFT
    h = h * window
    h_fwd, h_bwd = h[:, 0], h[:, 1]
    taps = jnp.concatenate([h_fwd, jnp.zeros_like(h_fwd[:1]), jnp.flip(h_bwd[1:], 0)], axis=0)
    taps = taps / jnp.sum(jnp.abs(taps), axis=0, keepdims=True)
    return jnp.fft.rfft(taps, axis=0)


def hyena_mix(proj, conv_w, skip, filt_f):
    L = proj.shape[1]
    u = centred_dwconv(proj, conv_w)
    v, x1, x2 = jnp.split(u, 3, axis=-1)
    z = v
    for o, gate in enumerate((x1, x2)):
        zf = jnp.fft.rfft(z, n=2 * L, axis=1)
        y = jnp.fft.irfft(zf * filt_f[:, o], n=2 * L, axis=1)[:, :L]
        z = gate * (y + z * skip[o])
    return z


def gated_delta_chunks(q, k, v, g, beta, s0):
    B, T, H, _ = q.shape
    C = GDN_CHUNK
    N = T // C

    def chunks(a):
        return jnp.moveaxis(a.reshape(B, N, C, H, *a.shape[3:]), 3, 1)

    q, k, v, g, beta = chunks(q), chunks(k), chunks(v), chunks(g), chunks(beta)
    gc = jnp.cumsum(g, axis=-1)
    tri = jnp.tril(jnp.ones((C, C), bool))
    strict = jnp.tril(jnp.ones((C, C), bool), -1)
    decay = jnp.exp(jnp.where(tri, gc[..., :, None] - gc[..., None, :], -jnp.inf))
    kb = k * beta[..., None]
    A = jnp.where(strict, jnp.einsum('bhnid,bhnjd->bhnij', kb, k) * decay, 0.0)
    eye = jnp.eye(C, dtype=jnp.float32)
    Tm = lax.linalg.triangular_solve(eye + A, jnp.broadcast_to(eye, A.shape),
                                     left_side=True, lower=True, unit_diagonal=True)
    u_val = jnp.einsum('bhnij,bhnjd->bhnid', Tm, v * beta[..., None])
    w_val = jnp.einsum('bhnij,bhnjd->bhnid', Tm, kb * jnp.exp(gc)[..., None])
    intra = jnp.einsum('bhnid,bhnjd->bhnij', q, k) * decay
    q_dec = q * jnp.exp(gc)[..., None]
    g_last = gc[..., -1]
    k_dec = k * jnp.exp(g_last[..., None] - gc)[..., None]

    def step(S, xs):
        qd, w, u, att, kd, gl = xs
        v_new = u - jnp.einsum('bhcd,bhde->bhce', w, S)
        o = jnp.einsum('bhcd,bhde->bhce', qd, S) + jnp.einsum('bhij,bhje->bhie', att, v_new)
        S = S * jnp.exp(gl)[..., None, None] + jnp.einsum('bhcd,bhce->bhde', kd, v_new)
        return S, o

    xs = tuple(jnp.moveaxis(a, 2, 0) for a in (q_dec, w_val, u_val, intra, k_dec, g_last))
    s_fin, o = lax.scan(step, s0, xs)
    o = jnp.transpose(o, (1, 0, 3, 2, 4)).reshape(B, T, H, -1)
    return o, s_fin


def gdn_mix(qkv, z, a, b, conv_w, a_log, dt_bias, norm_w, init_states, with_output):
    B, T, _ = qkv.shape
    u = jax.nn.silu(centred_dwconv(qkv, conv_w))
    q, k, v = [t.reshape(B, T, GDN_H, HEAD_DIM) for t in jnp.split(u, 3, axis=-1)]
    q = l2_normalize(q) * HEAD_DIM ** -0.5
    k = l2_normalize(k)
    g = -jnp.exp(a_log) * jax.nn.softplus(a.reshape(B, T, 2, GDN_H) + dt_bias)
    beta = jax.nn.sigmoid(b.reshape(B, T, 2, GDN_H))
    outs, finals = [], []
    for d in range(2):
        rev = (lambda t: jnp.flip(t, 1)) if d == 1 else (lambda t: t)
        o, s_fin = gated_delta_chunks(rev(q), rev(k), rev(v), rev(g[:, :, d]), rev(beta[:, :, d]), init_states[d])
        finals.append(s_fin)
        if with_output:
            outs.append(rev(o))
    if not with_output:
        return None, (finals[0], finals[1])
    o = rms_norm(outs[0] + outs[1], norm_w) * jax.nn.silu(z.reshape(B, T, GDN_H, HEAD_DIM))
    return o.reshape(B, T, GDN_H * HEAD_DIM), (finals[0], finals[1])


def axial_rope_angles(n):
    t = jnp.arange(n)
    row = (t // GRID_W).astype(jnp.float32)
    col = (t % GRID_W).astype(jnp.float32)
    nf = HEAD_DIM // 4
    inv = ROPE_THETA ** (-jnp.arange(nf, dtype=jnp.float32) / nf)
    return row[:, None] * inv, col[:, None] * inv


def rope_half(x, ang):
    x1, x2 = jnp.split(x, 2, axis=-1)
    cos, sin = jnp.cos(ang), jnp.sin(ang)
    return jnp.concatenate([x1 * cos - x2 * sin, x1 * sin + x2 * cos], axis=-1)


def axial_rope(x, ang_r, ang_c):
    half = HEAD_DIM // 2
    ar = ang_r[None, :, None, None, :]
    ac = ang_c[None, :, None, None, :]
    return jnp.concatenate([rope_half(x[..., :half], ar), rope_half(x[..., half:], ac)], axis=-1)


def diff_attend(q, k, v, lam):
    s = jnp.einsum('bqhcd,bkhcd->bhcqk', q, k) * HEAD_DIM ** -0.5
    p = jax.nn.softmax(s, axis=-1)
    a = p[:, :, 0] - lam * p[:, :, 1]
    return jnp.einsum('bhqk,bkhd->bqhd', a, v)


def diff_latent(q, k, v, kc, vc, lam, ang_r, ang_c):
    B, n = q.shape[:2]
    q = axial_rope(q, ang_r, ang_c)
    k_all = jnp.concatenate([axial_rope(k, ang_r, ang_c), kc], axis=1)
    v_all = jnp.concatenate([v, vc], axis=1)
    nb = n // Q_BLOCK
    qb = jnp.swapaxes(q.reshape(B, nb, Q_BLOCK, *q.shape[2:]), 0, 1)
    o = lax.map(lambda qi: diff_attend(qi, k_all, v_all, lam), qb)
    return jnp.swapaxes(o, 0, 1).reshape(B, n, DIFF_H, DIFF_VD)


def diff_finish(o, norm_w, lam_init):
    B, T = o.shape[:2]
    return (rms_norm(o, norm_w) * (1.0 - lam_init)).reshape(B, T, DIFF_H * DIFF_VD)


def split_diff(p):
    B, T, _ = p.shape
    q, k, v = jnp.split(p, 3, axis=-1)
    return (q.reshape(B, T, DIFF_H, 2, HEAD_DIM), k.reshape(B, T, DIFF_H, 2, HEAD_DIM),
            v.reshape(B, T, DIFF_H, DIFF_VD))


def dense_attend(q, k, v):
    s = jnp.einsum('bqhd,bkhd->bhqk', q, k) * HEAD_DIM ** -0.5
    p = jax.nn.softmax(s, axis=-1)
    return jnp.einsum('bhqk,bkhd->bqhd', p, v)


def na_latent(q, k, v, kc, vc, rpb):
    B, n, H, dh = q.shape
    R = n // GRID_W
    kh, kw = min(NA_ROWS, R), NA_COLS
    grid = lambda t: t.reshape(B, R, GRID_W, H, dh)
    kg, vg = grid(k), grid(v)
    cols = jnp.arange(GRID_W)
    cstart = jnp.clip(cols - kw // 2, 0, GRID_W - kw)
    col_ok = (cols[None, :] >= cstart[:, None]) & (cols[None, :] < cstart[:, None] + kw)
    mask = jnp.broadcast_to(col_ok[:, None, :], (GRID_W, kh, GRID_W)).reshape(GRID_W, kh * GRID_W)
    dc = jnp.clip(cols[None, :] - cols[:, None], -(kw - 1), kw - 1) + (kw - 1)
    rpb_c = rpb[:, :, dc]
    scale = dh ** -0.5

    def row(args):
        r, qr = args
        rs = jnp.clip(r - kh // 2, 0, R - kh)
        kr = lax.dynamic_slice_in_dim(kg, rs, kh, axis=1).reshape(B, kh * GRID_W, H, dh)
        vr = lax.dynamic_slice_in_dim(vg, rs, kh, axis=1).reshape(B, kh * GRID_W, H, dh)
        dr = rs + jnp.arange(kh) - r + (NA_ROWS - 1)
        bias = jnp.transpose(rpb_c[:, dr], (0, 2, 1, 3)).reshape(H, GRID_W, kh * GRID_W)
        s_win = jnp.einsum('bqhd,bkhd->bhqk', qr, kr) * scale + bias[None]
        s_win = jnp.where(mask, s_win, -jnp.inf)
        s_ctx = jnp.einsum('bqhd,bkhd->bhqk', qr, kc) * scale
        p = jax.nn.softmax(jnp.concatenate([s_win, s_ctx], axis=-1), axis=-1)
        nw = kh * GRID_W
        return (jnp.einsum('bhqk,bkhd->bqhd', p[..., :nw], vr)
                + jnp.einsum('bhqk,bkhd->bqhd', p[..., nw:], vc))

    o = lax.map(row, (jnp.arange(R), jnp.swapaxes(grid(q), 0, 1)))
    return jnp.swapaxes(o, 0, 1).reshape(B, n, H * dh)


def merge_branches(ys, gate_cols, proj, w_o):
    br = jnp.einsum('btmc,mcd->btmd', jnp.stack(ys, axis=2), proj)
    g = jax.nn.sigmoid(gate_cols.reshape(gate_cols.shape[0], gate_cols.shape[1], N_BRANCH, -1))
    return jnp.sum(g * br, axis=2) @ w_o


def moe_ffn(u, router_w, router_b, w1, w3, w2):
    n_tok, d = u.shape
    s = jax.nn.sigmoid(u @ router_w)
    sel = s + router_b
    per = N_EXPERTS // N_GROUPS
    group_score = lax.top_k(sel.reshape(n_tok, N_GROUPS, per), TOP_K)[0].sum(-1)
    best_group = jnp.argmax(group_score, axis=-1)
    in_group = (jnp.arange(N_EXPERTS) // per)[None, :] == best_group[:, None]
    _, idx = lax.top_k(jnp.where(in_group, sel, -jnp.inf), TOP_K)
    wts = jnp.take_along_axis(s, idx, axis=-1)
    wts = wts / jnp.sum(wts, -1, keepdims=True)
    n_slot = n_tok * TOP_K
    e_flat = idx.reshape(-1).astype(jnp.int32)
    t_flat = jnp.repeat(jnp.arange(n_tok, dtype=jnp.int32), TOP_K)
    w_flat = wts.reshape(-1)
    order = jnp.argsort(e_flat)
    e_s, t_s, w_s = e_flat[order], t_flat[order], w_flat[order]
    counts = jax.ops.segment_sum(jnp.ones_like(e_flat), e_flat, num_segments=N_EXPERTS)
    padded = (counts + MOE_BLOCK - 1) // MOE_BLOCK * MOE_BLOCK
    start = jnp.cumsum(counts) - counts
    pend = jnp.cumsum(padded)
    pstart = pend - padded
    dest = pstart[e_s] + jnp.arange(n_slot, dtype=jnp.int32) - start[e_s]
    n_blk = -(-n_slot // MOE_BLOCK) + N_EXPERTS
    cap = n_blk * MOE_BLOCK
    buf_tok = jnp.full((cap,), n_tok, jnp.int32).at[dest].set(t_s)
    buf_w = jnp.zeros((cap,), jnp.float32).at[dest].set(w_s)
    blk_exp = jnp.minimum(jnp.searchsorted(pend, jnp.arange(n_blk, dtype=jnp.int32) * MOE_BLOCK, side='right'),
                          N_EXPERTS - 1)
    u_pad = jnp.concatenate([u, jnp.zeros((1, d), u.dtype)], axis=0)
    xb = u_pad[buf_tok].reshape(n_blk, MOE_BLOCK, d)

    def expert_block(args):
        xi, e = args
        return (jax.nn.silu(xi @ w1[e]) * (xi @ w3[e])) @ w2[e]

    yb = lax.map(expert_block, (xb, blk_exp)).reshape(cap, d)
    y = jax.ops.segment_sum(yb * buf_w[:, None], buf_tok, num_segments=n_tok + 1)
    return y[:n_tok]


def diff_branch(dqkv_x, dqkv_c, lam, norm_w, lam_init, ang_r, ang_c):
    B, n, _ = dqkv_x.shape
    m = dqkv_c.shape[1]
    q, k, v = jnp.split(dqkv_x, 3, axis=-1)
    kc, vc = dqkv_c[..., MIX_W:2 * MIX_W], dqkv_c[..., 2 * MIX_W:]
    rot = lambda t: axial_rope(t.reshape(B, n, DIFF_H, 2, HEAD_DIM), ang_r, ang_c).reshape(B, n, MIX_W)
    qr = (rot(q) * (HEAD_DIM ** -0.5 * math.log2(math.e))).astype(jnp.bfloat16)
    nk = n + m
    k_all = jnp.concatenate([rot(k), kc], axis=1).astype(jnp.bfloat16).reshape(B, nk, DIFF_H, 2, HEAD_DIM)
    k_all = jnp.transpose(k_all, (0, 2, 3, 1, 4))
    v_all = jnp.concatenate([v, vc], axis=1).astype(jnp.bfloat16).reshape(B, nk // DIFF_TK, DIFF_TK, DIFF_H, DIFF_VD)
    vt_all = jnp.transpose(v_all, (0, 3, 1, 4, 2))
    return diff_attention(qr, k_all, vt_all, lam, norm_w, 1.0 - lam_init)


def na_branch(nqkv_x, nqkv_c, rpb):
    n = nqkv_x.shape[1]
    q, k, v = jnp.split(nqkv_x, 3, axis=-1)
    kc, vc = nqkv_c[..., MIX_W:2 * MIX_W], nqkv_c[..., 2 * MIX_W:]
    bf = lambda t: t.astype(jnp.bfloat16)
    tabs = na_bias_tables(rpb, n // GRID_W)
    return na_attention(bf(q * HEAD_DIM ** -0.5), bf(k), bf(v), bf(kc), bf(vc), tabs)


def _pad_w_in(w):
    segs = jnp.split(w, SPLIT_IDX, axis=-1)
    ab = jnp.concatenate([segs[3], segs[4]], axis=-1)
    ab = jnp.pad(ab, ((0, 0), (0, LANE - ab.shape[-1])))
    return jnp.concatenate([segs[7], segs[0], segs[1], segs[2], ab, segs[5], segs[6]], axis=-1)


PAD_SIZES = (N_BRANCH * D_MODEL, 3 * HY_W, 3 * MIX_W, MIX_W, LANE, 3 * MIX_W, 3 * MIX_W)
PAD_IDX = tuple(sum(PAD_SIZES[:i + 1]) for i in range(len(PAD_SIZES) - 1))
IN_PAD = sum(PAD_SIZES)
IN_TN = 640


def kernel(x, c, ctx, c_ctx, w_mod, b_mod, w_in, hy_conv, hy_w1, hy_b1, hy_w2, hy_b2, hy_w3, hy_freq,
           hy_deltas, hy_skip, gdn_conv, gdn_a_log, gdn_dt_bias, gdn_norm, diff_lam, diff_norm, na_rpb,
           branch_proj, w_out, ln_g, ln_b, router_w, router_b, exp_w1, exp_w3, exp_w2):
    B, n, D = x.shape
    m = ctx.shape[1]
    depth = w_mod.shape[0]
    dn_alpha = (2 * depth) ** 0.25
    ang_r, ang_c = axial_rope_angles(n)
    zero_state = jnp.zeros((B, GDN_H, HEAD_DIM, HEAD_DIM), jnp.float32)
    bf16 = jnp.bfloat16
    n_lat = B * n
    dft_tables = _dft_tables(n)
    tok = jnp.concatenate([x.reshape(n_lat, D), ctx.reshape(B * m, D)], axis=0)
    for l in range(depth):
        ctx_out = l < depth - 1
        mx = jax.nn.silu(c) @ w_mod[l] + b_mod[l]
        mc = jax.nn.silu(c_ctx) @ w_mod[l] + b_mod[l]
        one_plus = jnp.array([0.0, 1.0, 0.0, 0.0, 1.0, 0.0], jnp.float32)[:, None]
        mod_x = mx.reshape(B, 6, D) + one_plus
        mod_c = mc.reshape(1, 6, D) + one_plus
        mod = jnp.concatenate([jnp.repeat(mod_x, n // MERGE_TILE, axis=0),
                               jnp.repeat(mod_c, B * m // MERGE_TILE, axis=0)], axis=0)

        proj = ln_mod_matmul(tok, mod, _pad_w_in(w_in[l]).astype(bf16), IN_TN)
        px = proj[:n_lat].reshape(B, n, IN_PAD)
        pc = proj[n_lat:].reshape(B, m, IN_PAD)
        _, hy_x, gqkv_x, gz_x, gab_x, dqkv_x, nqkv_x = jnp.split(px, PAD_IDX, axis=-1)
        gate_c, hy_c, gqkv_c, gz_c, gab_c, dqkv_c, nqkv_c = jnp.split(pc, PAD_IDX, axis=-1)
        ga_x, gb_x = gab_x[..., :2 * GDN_H], gab_x[..., 2 * GDN_H:4 * GDN_H]
        ga_c, gb_c = gab_c[..., :2 * GDN_H], gab_c[..., 2 * GDN_H:4 * GDN_H]
        filt = (hy_w1[l], hy_b1[l], hy_w2[l], hy_b2[l], hy_w3[l], hy_freq[l], hy_deltas[l])

        ya_x = hyena_latent(hy_x, hy_conv[l], hy_skip[l], filt, dft_tables)

        yb_x, yb_c = gdn_branch(gqkv_x, gz_x, ga_x, gb_x, gqkv_c, gz_c, ga_c, gb_c,
                                gdn_conv[l], gdn_a_log[l], gdn_dt_bias[l], gdn_norm[l])

        lq1, lk1, lq2, lk2 = diff_lam[l]
        lam_init = 0.8 - 0.6 * math.exp(-0.3 * l)
        lam = jnp.exp(jnp.sum(lq1 * lk1)) - jnp.exp(jnp.sum(lq2 * lk2)) + lam_init
        yc_x = diff_branch(dqkv_x, dqkv_c, lam, diff_norm[l], lam_init, ang_r, ang_c)
        yd_x = na_branch(nqkv_x, nqkv_c, na_rpb[l])

        ys = [ya_x, yb_x, yc_x, yd_x]
        ys = [t.reshape(n_lat, MIX_W) for t in ys]
        if ctx_out:
            dq_c, dk_c, dv_c = split_diff(dqkv_c)
            nq_c, nk_c, nv_c = [t.reshape(B, m, NA_H, HEAD_DIM) for t in jnp.split(nqkv_c, 3, axis=-1)]
            ya_c = hyena_mix(hy_c, hy_conv[l], hy_skip[l], hyena_filters(m, *filt))
            yc_c = diff_finish(diff_attend(dq_c, dk_c, dv_c, lam), diff_norm[l], lam_init)
            yd_c = dense_attend(nq_c, nk_c, nv_c)
            ys_c = [t.reshape(B * m, MIX_W) for t in (ya_c, yb_c, yc_c, yd_c)]
            ys = [jnp.concatenate([a, b], axis=0) for a, b in zip(ys, ys_c)]
        else:
            tok = tok[:n_lat]
        tok = merge_residual_ln(tok, ys, proj, mod, branch_proj[l].astype(bf16), w_out[l].astype(bf16),
                                ln_g[l, 0], ln_b[l, 0], dn_alpha)

        n_tok = tok.shape[0]
        h2, idx, wts = moe_route(tok, mod, router_w, router_b)
        buf_tok, dest, blk_exp, n_used = moe_dispatch(idx, n_tok)
        yb = expert_ffn(blk_exp, n_used, jnp.take(h2, buf_tok, axis=0),
                        exp_w1[l].astype(bf16), exp_w3[l].astype(bf16), exp_w2[l].astype(bf16))
        dest = dest.reshape(n_tok, TOP_K)
        y = (wts[0][:, None] * jnp.take(yb, dest[:, 0], axis=0)
             + wts[1][:, None] * jnp.take(yb, dest[:, 1], axis=0))
        tok = residual_ln(tok, y, mod, ln_g[l, 1], ln_b[l, 1], dn_alpha, 5)
    return tok[:n_lat].reshape(B, n, D)
```

```python
import functools
import math

import jax
import jax.numpy as jnp
from jax import lax
from jax.experimental import pallas as pl
from jax.experimental.pallas import tpu as pltpu

D_MODEL = 1024
GRID_W = 64
HEAD_DIM = 64
MIX_W = D_MODEL // 2
N_BRANCH = 4
HY_W = MIX_W
HY_ORDER = 2
HY_BANDS = 16
HY_MOD_SHIFT = 0.05
GDN_H = MIX_W // HEAD_DIM
GDN_CHUNK = 64
DIFF_VD = 2 * HEAD_DIM
DIFF_H = MIX_W // DIFF_VD
NA_H = MIX_W // HEAD_DIM
NA_ROWS = 8
NA_COLS = 16
N_EXPERTS = 16
N_GROUPS = 4
TOP_K = 2
MOE_BLOCK = 128
Q_BLOCK = 128
ROPE_THETA = 10000.0
LN_EPS = 1e-5
RMS_EPS = 1e-6
SPLIT_SIZES = (3 * HY_W, 3 * MIX_W, MIX_W, 2 * GDN_H, 2 * GDN_H, 3 * MIX_W, 3 * MIX_W, N_BRANCH * D_MODEL)
SPLIT_IDX = tuple(sum(SPLIT_SIZES[:i + 1]) for i in range(len(SPLIT_SIZES) - 1))

LANE = 128
ROW_TILE = 512
MERGE_TILE = 256
VMEM_LIMIT = 48 * 1024 * 1024


def _ln_mod_matmul_kernel(x_ref, mod_ref, w_ref, wg_ref, o_ref, og_ref, h_ref):
    @pl.when(pl.program_id(1) == 0)
    def _():
        x = x_ref[...]
        mu = jnp.mean(x, axis=-1, keepdims=True)
        xc = x - mu
        var = jnp.mean(xc * xc, axis=-1, keepdims=True)
        y = xc * lax.rsqrt(var + LN_EPS)
        h = (y * mod_ref[0, 1:2, :] + mod_ref[0, 0:1, :]).astype(h_ref.dtype)
        h_ref[...] = h
        og_ref[...] = jnp.dot(h, wg_ref[...], preferred_element_type=jnp.float32)

    o_ref[...] = jnp.dot(h_ref[...], w_ref[...], preferred_element_type=jnp.float32).astype(o_ref.dtype)


def ln_mod_matmul(x, mod, w, w_gates, tn):
    T, D = x.shape
    N = w.shape[1]
    return pl.pallas_call(
        _ln_mod_matmul_kernel,
        grid=(T // ROW_TILE, N // tn),
        in_specs=[
            pl.BlockSpec((ROW_TILE, D), lambda i, j: (i, 0)),
            pl.BlockSpec((1, 6, D), lambda i, j: (i * (ROW_TILE // MERGE_TILE), 0, 0)),
            pl.BlockSpec((D, tn), lambda i, j: (0, j)),
            pl.BlockSpec((D, LANE), lambda i, j: (0, 0)),
        ],
        out_specs=[pl.BlockSpec((ROW_TILE, tn), lambda i, j: (i, j)),
                   pl.BlockSpec((ROW_TILE, LANE), lambda i, j: (i, 0))],
        out_shape=[jax.ShapeDtypeStruct((T, N), jnp.bfloat16), jax.ShapeDtypeStruct((T, LANE), jnp.float32)],
        scratch_shapes=[pltpu.VMEM((ROW_TILE, D), jnp.bfloat16)],
        compiler_params=pltpu.CompilerParams(
            dimension_semantics=("arbitrary", "arbitrary"), vmem_limit_bytes=VMEM_LIMIT),
        name="ln_mod_matmul",
    )(x, mod, w, w_gates)


def _layer_norm_rows(x):
    mu = jnp.mean(x, axis=-1, keepdims=True)
    xc = x - mu
    var = jnp.mean(xc * xc, axis=-1, keepdims=True)
    return xc * lax.rsqrt(var + LN_EPS)


_NT_DIMS = (((1,), (1,)), ((), ()))


DIFF_TQ = 256
DIFF_TK = 384
DIFF_ROWS = 32


def _diff_attn_kernel(lam_ref, q_ref, k_ref, vt_ref, nw_ref, o_ref, s_ref, *, out_scale):
    tq = q_ref.shape[1]
    n_tiles = k_ref.shape[3] // DIFF_TK
    q = q_ref[0]
    qc = (q[:, :HEAD_DIM], q[:, HEAD_DIM:])

    sub = 8
    fold = lambda t, op: op(t.reshape(t.shape[0] // sub, sub, t.shape[1]), axis=0)

    def scores(j, slot):
        start = pl.multiple_of(j * DIFF_TK, DIFF_TK)
        mx = []
        for c in range(2):
            k = k_ref[0, 0, c, pl.ds(start, DIFF_TK), :]
            st = lax.dot_general(k, qc[c], _NT_DIMS, preferred_element_type=jnp.float32)
            s_ref[slot, c] = st
            mx.append(fold(st, jnp.max))
        return tuple(mx)

    def softmax_pv(j, slot, mx, carry):
        new, ps = [], []
        for c in range(2):
            m_prev, l_prev, acc = carry[c]
            m_new = jnp.maximum(m_prev, jnp.max(mx[c], axis=0, keepdims=True))
            alpha = jnp.exp2(m_prev - m_new)
            psum, chunks = None, []
            for r in range(DIFF_TK // DIFF_ROWS):
                p = jnp.exp2(s_ref[slot, c, r * DIFF_ROWS:(r + 1) * DIFF_ROWS, :] - m_new)
                part = fold(p, jnp.sum)
                psum = part if psum is None else psum + part
                chunks.append(p.astype(jnp.bfloat16))
            new.append((m_new, alpha * l_prev + jnp.sum(psum, axis=0, keepdims=True), alpha * acc))
            ps.append(jnp.concatenate(chunks, axis=0))
        pv = jnp.dot(vt_ref[0, 0, j], jnp.concatenate(ps, axis=1), preferred_element_type=jnp.float32)
        return tuple((new[c][0], new[c][1], new[c][2] + pv[:, c * tq:(c + 1) * tq]) for c in range(2))

    def pair(jj, state):
        carry, mx0 = state
        j0 = 2 * jj
        mx1 = scores(j0 + 1, 1)
        carry = softmax_pv(j0, 0, mx0, carry)
        mx0 = scores(j0 + 2, 0)
        return softmax_pv(j0 + 1, 1, mx1, carry), mx0

    carry = tuple((jnp.full((1, tq), -jnp.inf, jnp.float32), jnp.zeros((1, tq), jnp.float32),
                   jnp.zeros((DIFF_VD, tq), jnp.float32)) for _ in range(2))
    mx0 = scores(0, 0)
    n_pairs = (n_tiles - 1) // 2
    carry, mx0 = lax.fori_loop(0, n_pairs, pair, (carry, mx0))
    if n_tiles % 2 == 0:
        mx1 = scores(n_tiles - 1, 1)
        carry = softmax_pv(n_tiles - 2, 0, mx0, carry)
        carry = softmax_pv(n_tiles - 1, 1, mx1, carry)
    else:
        carry = softmax_pv(n_tiles - 1, 0, mx0, carry)
    (_, l0, a0), (_, l1, a1) = carry
    o = a0 / l0 - lam_ref[0] * (a1 / l1)
    o = o * lax.rsqrt(jnp.mean(o * o, axis=0, keepdims=True) + RMS_EPS) * (nw_ref[...] * out_scale)
    o_ref[0] = o.T.astype(o_ref.dtype)


def diff_attention(q, k, vt, lam, norm_w, out_scale, out_dtype=jnp.float32):
    B, nq, W = q.shape
    nk = k.shape[3]
    assert nq % DIFF_TQ == 0 and nk % DIFF_TK == 0
    H = W // DIFF_VD
    return pl.pallas_call(
        functools.partial(_diff_attn_kernel, out_scale=out_scale),
        grid=(B, H, nq // DIFF_TQ),
        in_specs=[
            pl.BlockSpec(memory_space=pltpu.SMEM),
            pl.BlockSpec((1, DIFF_TQ, DIFF_VD), lambda b, h, i: (b, i, h)),
            pl.BlockSpec((1, 1, 2, nk, HEAD_DIM), lambda b, h, i: (b, h, 0, 0, 0)),
            pl.BlockSpec((1, 1, nk // DIFF_TK, DIFF_VD, DIFF_TK), lambda b, h, i: (b, h, 0, 0, 0)),
            pl.BlockSpec((DIFF_VD, 1), lambda b, h, i: (0, 0)),
        ],
        out_specs=pl.BlockSpec((1, DIFF_TQ, DIFF_VD), lambda b, h, i: (b, i, h)),
        out_shape=jax.ShapeDtypeStruct((B, nq, W), out_dtype),
        scratch_shapes=[pltpu.VMEM((2, 2, DIFF_TK, DIFF_TQ), jnp.float32)],
        compiler_params=pltpu.CompilerParams(
            dimension_semantics=("arbitrary",) * 3, vmem_limit_bytes=VMEM_LIMIT),
        name="diff_attention",
    )(lam.reshape(1).astype(jnp.float32), q, k, vt, norm_w.reshape(DIFF_VD, 1))


NA_QROWS = 8
NA_KROWS = 16
NA_NEG = -1e30


def _na_kernel(q_ref, k_ref, v_ref, kc_ref, vc_ref, tab_ref, o_ref, *, n_kblocks):
    g = pl.program_id(2)
    kb = (NA_KROWS // 4) * GRID_W
    base = jnp.clip(2 * g - 1, 0, n_kblocks - 4)
    start = pl.multiple_of(base * kb, kb)
    nwin = NA_KROWS * GRID_W
    q = q_ref[0]
    kw = k_ref[0, pl.ds(start, nwin), :]
    vw = v_ref[0, pl.ds(start, nwin), :]
    kc = kc_ref[0]
    vc = vc_ref[0]
    outs = []
    for hh in range(2):
        sl = slice(hh * HEAD_DIM, (hh + 1) * HEAD_DIM)
        qh = q[:, sl]
        s = lax.dot_general(qh, kw[:, sl], _NT_DIMS, preferred_element_type=jnp.float32) + tab_ref[0, hh]
        sc = lax.dot_general(qh, kc[:, sl], _NT_DIMS, preferred_element_type=jnp.float32)
        m = jnp.maximum(jnp.max(s, axis=-1, keepdims=True), jnp.max(sc, axis=-1, keepdims=True))
        p = jnp.exp(s - m)
        pc = jnp.exp(sc - m)
        l = jnp.sum(p, axis=-1, keepdims=True) + jnp.sum(pc, axis=-1, keepdims=True)
        o = (jnp.dot(p.astype(vw.dtype), vw[:, sl], preferred_element_type=jnp.float32)
             + jnp.dot(pc.astype(vc.dtype), vc[:, sl], preferred_element_type=jnp.float32))
        outs.append(o / l)
    o_ref[0] = jnp.concatenate(outs, axis=-1).astype(o_ref.dtype)


def na_bias_tables(rpb, R):
    H = rpb.shape[0]
    G = R // NA_QROWS
    cols = jnp.arange(GRID_W)
    cstart = jnp.clip(cols - NA_COLS // 2, 0, GRID_W - NA_COLS)
    col_ok = (cols[None, :] >= cstart[:, None]) & (cols[None, :] < cstart[:, None] + NA_COLS)
    dc = jnp.clip(cols[None, :] - cols[:, None], -(NA_COLS - 1), NA_COLS - 1) + (NA_COLS - 1)
    rpb_c = rpb[:, :, dc]
    tabs = []
    for g in (0, 1, G - 1):
        base = min(max(2 * g - 1, 0), R // 4 - 4)
        r = NA_QROWS * g + jnp.arange(NA_QROWS)
        rs = jnp.clip(r - NA_ROWS // 2, 0, R - NA_ROWS)
        kr = 4 * base + jnp.arange(NA_KROWS)
        valid = (kr[None, :] >= rs[:, None]) & (kr[None, :] < rs[:, None] + NA_ROWS)
        dr = jnp.clip(kr[None, :] - r[:, None] + (NA_ROWS - 1), 0, 2 * NA_ROWS - 2)
        tab = rpb_c[:, dr]
        ok = valid[None, :, :, None, None] & col_ok[None, None, None, :, :]
        tab = jnp.where(ok, tab, NA_NEG)
        tabs.append(jnp.transpose(tab, (0, 1, 3, 2, 4)).reshape(H, NA_QROWS * GRID_W, NA_KROWS * GRID_W))
    return jnp.stack(tabs, axis=0)


def na_attention(q, k, v, kc, vc, tabs, out_dtype=jnp.float32):
    B, n, W = q.shape
    m = kc.shape[1]
    tq = NA_QROWS * GRID_W
    G = n // tq
    n_kblocks = n // ((NA_KROWS // 4) * GRID_W)

    def tab_index(b, hp, g):
        return (jnp.where(g == 0, 0, jnp.where(g == G - 1, 2, 1)), hp, 0, 0)

    return pl.pallas_call(
        functools.partial(_na_kernel, n_kblocks=n_kblocks),
        grid=(B, W // LANE, G),
        in_specs=[
            pl.BlockSpec((1, tq, LANE), lambda b, hp, g: (b, g, hp)),
            pl.BlockSpec((1, n, LANE), lambda b, hp, g: (b, 0, hp)),
            pl.BlockSpec((1, n, LANE), lambda b, hp, g: (b, 0, hp)),
            pl.BlockSpec((1, m, LANE), lambda b, hp, g: (b, 0, hp)),
            pl.BlockSpec((1, m, LANE), lambda b, hp, g: (b, 0, hp)),
            pl.BlockSpec((1, 2, tq, NA_KROWS * GRID_W), tab_index),
        ],
        out_specs=pl.BlockSpec((1, tq, LANE), lambda b, hp, g: (b, g, hp)),
        out_shape=jax.ShapeDtypeStruct((B, n, W), out_dtype),
        compiler_params=pltpu.CompilerParams(
            dimension_semantics=("arbitrary",) * 3, vmem_limit_bytes=VMEM_LIMIT),
        name="na_attention",
    )(q, k, v, kc, vc, tabs)


def _merge_kernel(x_ref, ya_ref, of_ref, ob_ref, yc_ref, yd_ref, gc_ref, gz_ref, mod_ref, p_ref, wo_ref,
                  gn_ref, hs_ref, lng_ref, lnb_ref, o_ref, *, alpha):
    o = of_ref[0] + ob_ref[0]
    sq = o * o
    hi = sq.astype(jnp.bfloat16)
    lo = (sq - hi.astype(jnp.float32)).astype(jnp.bfloat16)
    ms = (jnp.dot(hi, hs_ref[...], preferred_element_type=jnp.float32)
          + jnp.dot(lo, hs_ref[...], preferred_element_type=jnp.float32)) * (1.0 / HEAD_DIM)
    gz = gz_ref[...].astype(jnp.float32)
    yb = o * lax.rsqrt(ms + RMS_EPS) * gn_ref[...] * (gz * jax.nn.sigmoid(gz))
    acc = None
    for mi, y in enumerate((ya_ref[...], yb, yc_ref[...], yd_ref[...])):
        br = jnp.dot(y.astype(jnp.bfloat16), p_ref[mi], preferred_element_type=jnp.float32)
        gm = jax.nn.sigmoid(gc_ref[:, mi * D_MODEL:(mi + 1) * D_MODEL].astype(jnp.float32))
        acc = gm * br if acc is None else acc + gm * br
    mix = jnp.dot(acc.astype(jnp.bfloat16), wo_ref[...], preferred_element_type=jnp.float32)
    z = alpha * x_ref[...] + mod_ref[0, 2:3, :] * mix
    o_ref[...] = _layer_norm_rows(z) * lng_ref[...] + lnb_ref[...]


def merge_residual_ln(x, ya, o_f, o_b, yc, yd, proj_main, mod, proj, w_o, gdn_norm, ln_g, ln_b, alpha, n, m):
    T, D = x.shape
    R = MERGE_TILE
    lat_tiles, ctx_tiles = n // R, m // R
    n_lat_tiles = o_f.shape[0] * lat_tiles

    def scan_rows(i):
        ic = i - n_lat_tiles
        return (jnp.where(i < n_lat_tiles, i // lat_tiles, ic // ctx_tiles),
                jnp.where(i < n_lat_tiles, ctx_tiles + i % lat_tiles, ic % ctx_tiles), 0)

    row = lambda i: (i, 0)
    const2 = lambda i: (0, 0)
    branch = pl.BlockSpec((R, MIX_W), row)
    scan = pl.BlockSpec((1, R, MIX_W), scan_rows)
    head_sum = (jnp.arange(MIX_W)[:, None] // HEAD_DIM == jnp.arange(MIX_W)[None, :] // HEAD_DIM).astype(jnp.bfloat16)
    return pl.pallas_call(
        functools.partial(_merge_kernel, alpha=alpha),
        grid=(T // R,),
        in_specs=[pl.BlockSpec((R, D), row), branch, scan, scan, branch, branch,
                  pl.BlockSpec((R, N_BRANCH * D), row),
                  pl.BlockSpec((R, MIX_W), lambda i: (i, MAIN_OFF[3] // MIX_W)),
                  pl.BlockSpec((1, 6, D), lambda i: (i, 0, 0)),
                  pl.BlockSpec((N_BRANCH, MIX_W, D), lambda i: (0, 0, 0)),
                  pl.BlockSpec((D, D), const2),
                  pl.BlockSpec((1, MIX_W), const2),
                  pl.BlockSpec((MIX_W, MIX_W), const2),
                  pl.BlockSpec((1, D), const2),
                  pl.BlockSpec((1, D), const2)],
        out_specs=pl.BlockSpec((R, D), row),
        out_shape=jax.ShapeDtypeStruct((T, D), jnp.float32),
        compiler_params=pltpu.CompilerParams(dimension_semantics=("arbitrary",), vmem_limit_bytes=VMEM_LIMIT),
        name="merge_residual_ln",
    )(x, ya, o_f, o_b, yc, yd, proj_main, proj_main, mod, proj, w_o,
      jnp.tile(gdn_norm, GDN_H).reshape(1, MIX_W), head_sum, ln_g.reshape(1, D), ln_b.reshape(1, D))


def _route_kernel(x_ref, mod_ref, rw_ref, rb_ref, h_ref, idx_ref, wts_ref):
    h = (_layer_norm_rows(x_ref[...]) * mod_ref[0, 4:5, :] + mod_ref[0, 3:4, :]).astype(jnp.bfloat16)
    h_ref[...] = h
    logits = lax.dot_general(rw_ref[...], h, _NT_DIMS, preferred_element_type=jnp.float32)
    s = jax.nn.sigmoid(logits)
    sel = s + rb_ref[...]
    per = N_EXPERTS // N_GROUPS
    srow = [s[e:e + 1, :] for e in range(N_EXPERTS)]
    vrow = [sel[e:e + 1, :] for e in range(N_EXPERTS)]
    best = None
    for gi in range(N_GROUPS):
        grp = vrow[gi * per:(gi + 1) * per]
        gs = None
        for a in range(per):
            for b in range(a + 1, per):
                ps = grp[a] + grp[b]
                gs = ps if gs is None else jnp.maximum(gs, ps)
        if best is None:
            best, bg = gs, jnp.zeros(gs.shape, jnp.int32)
        else:
            upd = gs > best
            bg = jnp.where(upd, gi, bg)
            best = jnp.where(upd, gs, best)
    cv, cs = [], []
    for j in range(per):
        v_j, s_j = vrow[j], srow[j]
        for gi in range(1, N_GROUPS):
            v_j = jnp.where(bg == gi, vrow[gi * per + j], v_j)
            s_j = jnp.where(bg == gi, srow[gi * per + j], s_j)
        cv.append(v_j)
        cs.append(s_j)

    def first_argmax(vals):
        bv, bi = vals[0], jnp.zeros(vals[0].shape, jnp.int32)
        for j in range(1, per):
            upd = vals[j] > bv
            bi = jnp.where(upd, j, bi)
            bv = jnp.where(upd, vals[j], bv)
        return bi

    i1 = first_argmax(cv)
    i2 = first_argmax([jnp.where(i1 == j, -jnp.inf, cv[j]) for j in range(per)])
    w1 = cs[0]
    w2 = cs[0]
    for j in range(1, per):
        w1 = jnp.where(i1 == j, cs[j], w1)
        w2 = jnp.where(i2 == j, cs[j], w2)
    tot = w1 + w2
    idx_ref[...] = jnp.concatenate([bg * per + i1, bg * per + i2], axis=0)
    wts_ref[...] = jnp.concatenate([w1 / tot, w2 / tot], axis=0)


def moe_route(x, mod, router_w, router_b):
    T, D = x.shape
    return pl.pallas_call(
        _route_kernel,
        grid=(T // MERGE_TILE,),
        in_specs=[pl.BlockSpec((MERGE_TILE, D), lambda i: (i, 0)),
                  pl.BlockSpec((1, 6, D), lambda i: (i, 0, 0)),
                  pl.BlockSpec((N_EXPERTS, D), lambda i: (0, 0)),
                  pl.BlockSpec((N_EXPERTS, 1), lambda i: (0, 0))],
        out_specs=[pl.BlockSpec((MERGE_TILE, D), lambda i: (i, 0)),
                   pl.BlockSpec((TOP_K, MERGE_TILE), lambda i: (0, i)),
                   pl.BlockSpec((TOP_K, MERGE_TILE), lambda i: (0, i))],
        out_shape=[jax.ShapeDtypeStruct((T, D), jnp.bfloat16),
                   jax.ShapeDtypeStruct((TOP_K, T), jnp.int32),
                   jax.ShapeDtypeStruct((TOP_K, T), jnp.float32)],
        compiler_params=pltpu.CompilerParams(dimension_semantics=("arbitrary",), vmem_limit_bytes=VMEM_LIMIT),
        name="moe_route",
    )(x, mod, router_w.T.astype(jnp.bfloat16), router_b.reshape(N_EXPERTS, 1).astype(jnp.float32))


EXPERT_BLOCK = 256


def _expert_kernel(be_ref, nu_ref, x_ref, w1_ref, w3_ref, w2_ref, o_ref):
    i = pl.program_id(0)

    @pl.when(i < nu_ref[0])
    def _():
        x = x_ref[...]
        bf = lambda ref: ref[0].astype(jnp.bfloat16)
        a = jnp.dot(x, bf(w1_ref), preferred_element_type=jnp.float32)
        b = jnp.dot(x, bf(w3_ref), preferred_element_type=jnp.float32)
        hmid = (a * jax.nn.sigmoid(a) * b).astype(jnp.bfloat16)
        o_ref[...] = jnp.dot(hmid, bf(w2_ref), preferred_element_type=jnp.float32).astype(o_ref.dtype)

    @pl.when(i >= nu_ref[0])
    def _():
        o_ref[...] = jnp.zeros(o_ref.shape, o_ref.dtype)


def expert_ffn(blk_exp, n_used, xb, w1, w3, w2, out_dtype=jnp.float32):
    cap, D = xb.shape
    F = w1.shape[-1]
    n_blk = cap // EXPERT_BLOCK
    return pl.pallas_call(
        _expert_kernel,
        grid_spec=pltpu.PrefetchScalarGridSpec(
            num_scalar_prefetch=2,
            grid=(n_blk,),
            in_specs=[pl.BlockSpec((EXPERT_BLOCK, D), lambda i, be, nu: (i, 0)),
                      pl.BlockSpec((1, D, F), lambda i, be, nu: (be[i], 0, 0)),
                      pl.BlockSpec((1, D, F), lambda i, be, nu: (be[i], 0, 0)),
                      pl.BlockSpec((1, F, D), lambda i, be, nu: (be[i], 0, 0))],
            out_specs=pl.BlockSpec((EXPERT_BLOCK, D), lambda i, be, nu: (i, 0)),
        ),
        out_shape=jax.ShapeDtypeStruct((cap, D), out_dtype),
        compiler_params=pltpu.CompilerParams(dimension_semantics=("arbitrary",), vmem_limit_bytes=VMEM_LIMIT),
        name="expert_ffn",
    )(blk_exp, n_used, xb, w1, w3, w2)


def _gdn_kernel(qf_ref, kf_ref, vf_ref, gf_ref, bf_ref, qb_ref, kb_ref, vb_ref, gb_ref, bb_ref,
                of_ref, ob_ref, s_ref):
    C = GDN_CHUNK
    nb = qf_ref.shape[0]

    @pl.when(pl.program_id(0) == 0)
    def _():
        s_ref[...] = jnp.zeros(s_ref.shape, jnp.float32)

    bf = lambda t: t.astype(jnp.bfloat16)
    mm = lambda a, b: jnp.dot(bf(a), bf(b), preferred_element_type=jnp.float32)
    nt = lambda a, b: lax.dot_general(bf(a), bf(b), _NT_DIMS, preferred_element_type=jnp.float32)
    tn = lambda a, b: lax.dot_general(bf(a), bf(b), (((0,), (0,)), ((), ())), preferred_element_type=jnp.float32)
    col = lambda t, h: t[:, h:h + 1]
    rel = lax.broadcasted_iota(jnp.int32, (C, C), 0) - lax.broadcasted_iota(jnp.int32, (C, C), 1)

    qs, ks, vs, decay, beta_c, e_gc, e_rest, e_tot, strict = [], [], [], [], [], [], [], [], []
    for bi in range(nb):
        for d, refs in enumerate(((qf_ref, kf_ref, vf_ref, gf_ref, bf_ref), (qb_ref, kb_ref, vb_ref, gb_ref, bb_ref))):
            q_ref, k_ref, v_ref, g_ref, b_ref = refs
            incl = rel >= 0 if d == 0 else rel <= 0
            g = g_ref[bi, 0]
            beta = b_ref[bi, 0]
            gc = jnp.dot(incl.astype(jnp.float32), g, precision=lax.Precision.HIGHEST,
                         preferred_element_type=jnp.float32)
            gc_t = gc.T
            tot = jnp.sum(g, axis=0, keepdims=True)
            eg, er, et = jnp.exp(gc), jnp.exp(tot - gc), jnp.exp(tot)
            q, k, v = q_ref[bi], k_ref[bi], v_ref[bi]
            for h in range(GDN_H):
                sl = slice(h * HEAD_DIM, (h + 1) * HEAD_DIM)
                qs.append(q[:, sl])
                ks.append(k[:, sl])
                vs.append(v[:, sl])
                decay.append(jnp.where(incl, jnp.exp(col(gc, h) - gc_t[h:h + 1, :]), 0.0))
                strict.append(rel > 0 if d == 0 else rel < 0)
                beta_c.append(col(beta, h))
                e_gc.append(col(eg, h))
                e_rest.append(col(er, h))
                e_tot.append(col(et, h))
    chains = range(len(qs))
    kb = [ks[c] * beta_c[c] for c in chains]
    a_mat = [jnp.where(strict[c], nt(kb[c], ks[c]) * decay[c], 0.0) for c in chains]
    intra = [nt(qs[c], ks[c]) * decay[c] for c in chains]
    ri = lax.broadcasted_iota(jnp.int32, (C, C), 0)
    ci = lax.broadcasted_iota(jnp.int32, (C, C), 1)

    def level_mask(s, upper):
        sh = s.bit_length() - 1
        same = (ri >> (sh + 1)) == (ci >> (sh + 1))
        r_hi = ((ri >> sh) & 1) == 1
        c_hi = ((ci >> sh) & 1) == 1
        return same & c_hi & ~r_hi if upper else same & r_hi & ~c_hi

    masks = {s: (level_mask(s, False), level_mask(s, True)) for s in (1, 2, 4, 8, 16, 32)}
    upper = [(c // GDN_H) % 2 == 1 for c in chains]
    eye = (ri == ci).astype(jnp.float32)
    t_inv = [eye - jnp.where(masks[1][upper[c]], a_mat[c], 0.0) for c in chains]
    for s in (2, 4, 8, 16, 32):
        tm = [mm(t_inv[c], jnp.where(masks[s][upper[c]], a_mat[c], 0.0)) for c in chains]
        t_inv = [t_inv[c] - mm(tm[c], t_inv[c]) for c in chains]
    xs = [mm(t_inv[c], jnp.concatenate([vs[c] * beta_c[c], kb[c] * e_gc[c]], axis=1)) for c in chains]
    st = [s_ref[c] for c in chains]
    v_new = [xs[c][:, :HEAD_DIM] - mm(xs[c][:, HEAD_DIM:], st[c]) for c in chains]
    o_st = [mm(qs[c] * e_gc[c], st[c]) for c in chains]
    o_in = [mm(intra[c], v_new[c]) for c in chains]
    s_up = [tn(ks[c] * e_rest[c], v_new[c]) for c in chains]
    for c in chains:
        s_ref[c] = st[c] * e_tot[c] + s_up[c]
    outs = [o_st[c] + o_in[c] for c in chains]
    for bi in range(nb):
        for d, o_ref in enumerate((of_ref, ob_ref)):
            c0 = (bi * 2 + d) * GDN_H
            o_ref[bi] = jnp.concatenate(outs[c0:c0 + GDN_H], axis=1)


def gdn_scan(q, k, v, g, beta, n_ctx_chunks):
    B, Tt, W = q.shape
    C = GDN_CHUNK
    NC = Tt // C

    def bwd(s):
        return jnp.where(s < n_ctx_chunks, n_ctx_chunks - 1 - s, NC - 1 - (s - n_ctx_chunks))

    tok_f = pl.BlockSpec((B, C, W), lambda s: (0, s, 0))
    tok_b = pl.BlockSpec((B, C, W), lambda s: (0, bwd(s), 0))
    gate_f = pl.BlockSpec((B, 1, C, LANE), lambda s: (0, 0, s, 0))
    gate_b = pl.BlockSpec((B, 1, C, LANE), lambda s: (0, 1, bwd(s), 0))
    return pl.pallas_call(
        _gdn_kernel,
        grid=(NC,),
        in_specs=[tok_f, tok_f, tok_f, gate_f, gate_f, tok_b, tok_b, tok_b, gate_b, gate_b],
        out_specs=[tok_f, tok_b],
        out_shape=[jax.ShapeDtypeStruct((B, Tt, W), jnp.float32)] * 2,
        scratch_shapes=[pltpu.VMEM((B * 2 * GDN_H, HEAD_DIM, HEAD_DIM), jnp.float32)],
        compiler_params=pltpu.CompilerParams(dimension_semantics=("arbitrary",), vmem_limit_bytes=VMEM_LIMIT),
        name="gdn_scan",
    )(q, k, v, g, beta, q, k, v, g, beta)


def gdn_pre(qkv, a, b, conv_w, a_log, dt_bias):
    B, T, _ = qkv.shape
    u = jax.nn.silu(centred_dwconv(qkv, conv_w))
    q, k, v = [t.reshape(B, T, GDN_H, HEAD_DIM) for t in jnp.split(u, 3, axis=-1)]
    q = (l2_normalize(q) * HEAD_DIM ** -0.5).reshape(B, T, MIX_W)
    k = l2_normalize(k).reshape(B, T, MIX_W)
    g = -jnp.exp(a_log) * jax.nn.softplus(a.reshape(B, T, 2, GDN_H) + dt_bias)
    beta = jax.nn.sigmoid(b.reshape(B, T, 2, GDN_H))
    lanes = lambda t: jnp.pad(jnp.swapaxes(t, 1, 2), ((0, 0), (0, 0), (0, 0), (0, LANE - GDN_H)))
    return q, k, v.reshape(B, T, MIX_W), lanes(g), lanes(beta)


def gdn_branch(gqkv_x, ga_x, gb_x, gqkv_c, ga_c, gb_c, conv_w, a_log, dt_bias):
    m = gqkv_c.shape[1]
    pre_c = gdn_pre(gqkv_c, ga_c, gb_c, conv_w, a_log, dt_bias)
    pre_x = gdn_pre(gqkv_x, ga_x, gb_x, conv_w, a_log, dt_bias)
    q, k, v = [jnp.concatenate([pc, px], axis=1) for pc, px in zip(pre_c[:3], pre_x[:3])]
    g, beta = [jnp.concatenate([pc, px], axis=2) for pc, px in zip(pre_c[3:], pre_x[3:])]
    return gdn_scan(q, k, v, g, beta, m // GDN_CHUNK)


FFT_R = 128
FFT_COLS = 4096
FFT_K1_STEP = 2


def _dft_tables(L):
    R = FFT_R
    N = R * R
    half = L // R
    idx = jnp.arange(R, dtype=jnp.int32)
    ang1 = (-2.0 * math.pi / R) * ((idx[:, None] * idx[None, :]) % R).astype(jnp.float32)
    fr, fi = jnp.cos(ang1), jnp.sin(ang1)
    blk = lambda re, im: jnp.concatenate([jnp.concatenate([re, -im], axis=1),
                                          jnp.concatenate([im, re], axis=1)], axis=0)
    m1 = blk(fr[:, :half], fi[:, :half])
    m1_real = jnp.concatenate([fr, fi], axis=0)
    m3 = blk(fr.T[:half], -fi.T[:half]) * (1.0 / N)
    k = idx[:, None, None] + R * idx[None, :, None]
    ang2 = (-2.0 * math.pi / N) * ((idx[None, None, :] * k) % N).astype(jnp.float32)
    gr, gi = jnp.cos(ang2), jnp.sin(ang2)
    g_fwd = jnp.concatenate([jnp.concatenate([gr, -gi], axis=2),
                             jnp.concatenate([gi, gr], axis=2)], axis=1)
    g_inv = jnp.swapaxes(g_fwd, 1, 2)
    bf = lambda t: t.astype(jnp.bfloat16)
    return bf(m1), bf(m1_real), bf(m3), bf(g_fwd), bf(g_inv)


def _colmm_kernel(m_ref, x_ref, o_ref):
    o_ref[...] = jnp.dot(m_ref[...], x_ref[...].astype(jnp.bfloat16),
                         preferred_element_type=jnp.float32).astype(o_ref.dtype)


def colmm(mat, x, out_dtype=jnp.bfloat16):
    M, K = mat.shape
    n_cols = x.shape[1]
    return pl.pallas_call(
        _colmm_kernel,
        grid=(n_cols // FFT_COLS,),
        in_specs=[pl.BlockSpec((M, K), lambda j: (0, 0)), pl.BlockSpec((K, FFT_COLS), lambda j: (0, j))],
        out_specs=pl.BlockSpec((M, FFT_COLS), lambda j: (0, j)),
        out_shape=jax.ShapeDtypeStruct((M, n_cols), out_dtype),
        compiler_params=pltpu.CompilerParams(dimension_semantics=("arbitrary",), vmem_limit_bytes=VMEM_LIMIT),
        name="hyena_colmm",
    )(mat, x)


def _colmm_gate_kernel(m_ref, r_ref, z_ref, x_ref, skip_ref, o_ref):
    y = jnp.dot(m_ref[...], r_ref[...], preferred_element_type=jnp.float32)
    o_ref[...] = x_ref[...] * (y + z_ref[...] * skip_ref[...])


def colmm_gate(mat, r, z, gate, skip_cols):
    M, K = mat.shape
    n_cols = r.shape[1]
    col = lambda j: (0, j)
    return pl.pallas_call(
        _colmm_gate_kernel,
        grid=(n_cols // FFT_COLS,),
        in_specs=[pl.BlockSpec((M, K), lambda j: (0, 0)), pl.BlockSpec((K, FFT_COLS), col),
                  pl.BlockSpec((M, FFT_COLS), col), pl.BlockSpec((M, FFT_COLS), col),
                  pl.BlockSpec((1, FFT_COLS), col)],
        out_specs=pl.BlockSpec((M, FFT_COLS), col),
        out_shape=jax.ShapeDtypeStruct((M, n_cols), jnp.float32),
        compiler_params=pltpu.CompilerParams(dimension_semantics=("arbitrary",), vmem_limit_bytes=VMEM_LIMIT),
        name="hyena_colmm_gate",
    )(mat, r, z, gate, skip_cols)


def _spectrum_kernel(p_ref, g_ref, o_ref):
    R = FFT_R
    for j in range(FFT_K1_STEP):
        p = jnp.concatenate([p_ref[0, j], p_ref[1, j]], axis=0)
        q = jnp.dot(g_ref[j], p, preferred_element_type=jnp.float32)
        o_ref[0, j] = q[:R]
        o_ref[1, j] = q[R:]


def _freq_kernel(p_ref, g_ref, gi_ref, h_ref, o_ref):
    R = FFT_R
    for j in range(FFT_K1_STEP):
        p = jnp.concatenate([p_ref[0, j], p_ref[1, j]], axis=0)
        q = jnp.dot(g_ref[j], p, preferred_element_type=jnp.float32)
        qr, qi = q[:R], q[R:]
        hr, hi = h_ref[0, j], h_ref[1, j]
        y = jnp.concatenate([qr * hr - qi * hi, qr * hi + qi * hr], axis=0).astype(jnp.bfloat16)
        r = jnp.dot(gi_ref[j], y, preferred_element_type=jnp.float32).astype(o_ref.dtype)
        o_ref[0, j] = r[:R]
        o_ref[1, j] = r[R:]


def hyena_spectrum(p, g_fwd):
    C = p.shape[-1]
    R, S = FFT_R, FFT_K1_STEP
    blk = pl.BlockSpec((2, S, R, LANE * 4), lambda i, c: (0, i, 0, c))
    return pl.pallas_call(
        _spectrum_kernel,
        grid=(R // S, C // (LANE * 4)),
        in_specs=[blk, pl.BlockSpec((S, 2 * R, 2 * R), lambda i, c: (i, 0, 0))],
        out_specs=blk,
        out_shape=jax.ShapeDtypeStruct((2, R, R, C), jnp.float32),
        compiler_params=pltpu.CompilerParams(dimension_semantics=("arbitrary",) * 2, vmem_limit_bytes=VMEM_LIMIT),
        name="hyena_spectrum",
    )(p, g_fwd)


def hyena_freq(p, g_fwd, g_inv, spec, order):
    C = p.shape[-1]
    R, S = FFT_R, FFT_K1_STEP
    blk = pl.BlockSpec((2, S, R, C), lambda i: (0, i, 0, 0))
    gspec = pl.BlockSpec((S, 2 * R, 2 * R), lambda i: (i, 0, 0))
    return pl.pallas_call(
        _freq_kernel,
        grid=(R // S,),
        in_specs=[blk, gspec, gspec, pl.BlockSpec((2, S, R, C), lambda i: (0, i, 0, order))],
        out_specs=blk,
        out_shape=jax.ShapeDtypeStruct((2, R, R, C), jnp.bfloat16),
        compiler_params=pltpu.CompilerParams(dimension_semantics=("arbitrary",), vmem_limit_bytes=VMEM_LIMIT),
        name="hyena_freq",
    )(p, g_fwd, g_inv, spec)


def hyena_taps(L, w1, b1, w2, b2, w3, freq, deltas):
    f32 = jnp.float32
    pos = jnp.arange(L, dtype=f32)
    pos_b = L - pos
    bands = jnp.linspace(1e-4, HY_BANDS - 1, HY_BANDS, dtype=f32)[None, :]

    def mlp(j, w3_dir, delta_dir):
        t = (j / (L - 1))[:, None]
        ang = (2.0 * math.pi / L) * j[:, None]
        feats = jnp.concatenate([t, jnp.cos(bands * ang), -jnp.sin(bands * ang)], axis=-1)
        h = jnp.sin(freq * (feats @ w1 + b1))
        h = jnp.sin(freq * (h @ w2 + b2))
        h = h @ w3_dir
        return h * (jnp.exp(-t * jnp.abs(delta_dir).reshape(1, -1)) + HY_MOD_SHIFT)

    w3d = w3.reshape(w3.shape[0], 2, HY_ORDER * HY_W)
    h_f = mlp(pos, w3d[:, 0], deltas[0])
    h_b = mlp(pos_b, w3d[:, 1], deltas[1])
    h_b = jnp.where((pos > 0)[:, None], h_b, 0.0)
    taps = jnp.concatenate([h_f, h_b], axis=0)
    return taps / jnp.sum(jnp.abs(taps), axis=0, keepdims=True)


def hyena_latent(proj, conv_w, skip, filt, tables):
    B, L, _ = proj.shape
    assert B == 2 and L % FFT_R == 0 and 2 * L == FFT_R * FFT_R
    R, C = FFT_R, HY_W
    m1, m1_real, m3, g_fwd, g_inv = tables
    taps = hyena_taps(L, *filt)
    spec = hyena_spectrum(colmm(m1_real, taps.reshape(R, R * HY_ORDER * C)).reshape(2, R, R, HY_ORDER * C), g_fwd)
    u = centred_dwconv(proj, conv_w)
    v, x1, x2 = jnp.split(u, 3, axis=-1)
    rows = B * (L // R)
    z = v.reshape(rows, R * C)
    for o, gate in enumerate((x1, x2)):
        p = colmm(m1, z).reshape(2, R, R, C)
        r = hyena_freq(p, g_fwd, g_inv, spec, o).reshape(2 * R, R * C)
        z = colmm_gate(m3, r, z, gate.reshape(rows, R * C), jnp.tile(skip[o], R).reshape(1, R * C))
    return z.reshape(B, L, C)


def _combine_ln_kernel(dest_ref, x_ref, w_ref, mod_ref, lng_ref, lnb_ref, yb_ref, o_ref, buf_ref, sem, *, alpha):
    R = x_ref.shape[0]

    def row_copy(k, r):
        row = dest_ref[0, 0, k * R + r]
        return pltpu.make_async_copy(yb_ref.at[pl.ds(row, 1)], buf_ref.at[k, pl.ds(r, 1)], sem.at[k])

    def issue(r, carry):
        row_copy(0, r).start()
        row_copy(1, r).start()
        return carry

    def drain(r, carry):
        row_copy(0, r).wait()
        row_copy(1, r).wait()
        return carry

    lax.fori_loop(0, R, issue, 0)
    lax.fori_loop(0, R, drain, 0)
    y = w_ref[:, 0:1] * buf_ref[0] + w_ref[:, 1:2] * buf_ref[1]
    z = alpha * x_ref[...] + mod_ref[0, 5:6, :] * y
    o_ref[...] = _layer_norm_rows(z) * lng_ref[...] + lnb_ref[...]


def moe_combine_ln(x, yb, dest, wts, mod, ln_g, ln_b, alpha):
    T, D = x.shape
    R = MERGE_TILE
    dest_tiles = jnp.swapaxes(dest.reshape(T // R, R, TOP_K), 1, 2).reshape(T // R, 1, TOP_K * R)
    row = lambda i: (i, 0)
    return pl.pallas_call(
        functools.partial(_combine_ln_kernel, alpha=alpha),
        grid=(T // R,),
        in_specs=[pl.BlockSpec((1, 1, TOP_K * R), lambda i: (i, 0, 0), memory_space=pltpu.SMEM),
                  pl.BlockSpec((R, D), row),
                  pl.BlockSpec((R, TOP_K), row),
                  pl.BlockSpec((1, 6, D), lambda i: (i, 0, 0)),
                  pl.BlockSpec((1, D), lambda i: (0, 0)),
                  pl.BlockSpec((1, D), lambda i: (0, 0)),
                  pl.BlockSpec(memory_space=pl.ANY)],
        out_specs=pl.BlockSpec((R, D), row),
        out_shape=jax.ShapeDtypeStruct((T, D), jnp.float32),
        scratch_shapes=[pltpu.VMEM((TOP_K, R, D), jnp.float32), pltpu.SemaphoreType.DMA((TOP_K,))],
        compiler_params=pltpu.CompilerParams(dimension_semantics=("arbitrary",), vmem_limit_bytes=VMEM_LIMIT),
        name="moe_combine_ln",
    )(dest_tiles, x, wts.T, mod, ln_g.reshape(1, D), ln_b.reshape(1, D), yb)


def moe_dispatch(idx, n_tok):
    n_slot = n_tok * TOP_K
    e_flat = idx.T.reshape(-1)
    onehot = (e_flat[:, None] == jnp.arange(N_EXPERTS, dtype=jnp.int32)[None, :]).astype(jnp.int32)
    csum = jnp.cumsum(onehot, axis=0)
    rank = jnp.sum(csum * onehot, axis=1) - 1
    counts = csum[-1]
    padded = (counts + EXPERT_BLOCK - 1) // EXPERT_BLOCK * EXPERT_BLOCK
    pend = jnp.cumsum(padded)
    pstart = pend - padded
    dest = jnp.sum(onehot * pstart[None, :], axis=1) + rank
    n_blk = -(-n_slot // EXPERT_BLOCK) + N_EXPERTS
    cap = n_blk * EXPERT_BLOCK
    t_flat = jnp.arange(n_slot, dtype=jnp.int32) // TOP_K
    buf_tok = jnp.zeros((cap,), jnp.int32).at[dest].set(t_flat)
    blk_exp = jnp.minimum(jnp.searchsorted(pend, jnp.arange(n_blk, dtype=jnp.int32) * EXPERT_BLOCK, side='right'),
                          N_EXPERTS - 1).astype(jnp.int32)
    n_used = (pend[-1] // EXPERT_BLOCK).astype(jnp.int32).reshape(1)
    return buf_tok, dest, blk_exp, n_used


def layer_norm(x, g=None, b=None):
    mu = jnp.mean(x, -1, keepdims=True)
    var = jnp.mean(jnp.square(x - mu), -1, keepdims=True)
    y = (x - mu) * lax.rsqrt(var + LN_EPS)
    if g is not None:
        y = y * g + b
    return y


def rms_norm(x, w):
    return x * lax.rsqrt(jnp.mean(jnp.square(x), -1, keepdims=True) + RMS_EPS) * w


def l2_normalize(x):
    return x * lax.rsqrt(jnp.sum(jnp.square(x), -1, keepdims=True) + RMS_EPS)


def modulate(x, shift, scale):
    return layer_norm(x) * (1.0 + scale) + shift


def centred_dwconv(u, w):
    K = w.shape[-1]
    T = u.shape[1]
    up = jnp.pad(u, ((0, 0), (K // 2, K // 2), (0, 0)))
    out = up[:, 0:T, :] * w[:, 0]
    for j in range(1, K):
        out = out + up[:, j:j + T, :] * w[:, j]
    return out


def hyena_filters(L, w1, b1, w2, b2, w3, freq, deltas):
    f32 = jnp.float32
    t = jnp.linspace(0.0, 1.0, L, dtype=f32)[:, None]
    ang = 2.0 * math.pi * jnp.arange(L, dtype=f32)[:, None] / L
    bands = jnp.linspace(1e-4, HY_BANDS - 1, HY_BANDS, dtype=f32)[None, :]
    feats = jnp.concatenate([t, jnp.cos(bands * ang), -jnp.sin(bands * ang)], axis=-1)
    h = jnp.sin(freq * (feats @ w1 + b1))
    h = jnp.sin(freq * (h @ w2 + b2))
    h = (h @ w3).reshape(L, 2, HY_ORDER, HY_W)
    window = jnp.exp(-t[:, :, None, None] * jnp.abs(deltas)) + HY_MOD_SHIFT
    h = h * window
    h_fwd, h_bwd = h[:, 0], h[:, 1]
    taps = jnp.concatenate([h_fwd, jnp.zeros_like(h_fwd[:1]), jnp.flip(h_bwd[1:], 0)], axis=0)
    taps = taps / jnp.sum(jnp.abs(taps), axis=0, keepdims=True)
    return jnp.fft.rfft(taps, axis=0)


def hyena_mix(proj, conv_w, skip, filt_f):
    L = proj.shape[1]
    u = centred_dwconv(proj, conv_w)
    v, x1, x2 = jnp.split(u, 3, axis=-1)
    z = v
    for o, gate in enumerate((x1, x2)):
        zf = jnp.fft.rfft(z, n=2 * L, axis=1)
        y = jnp.fft.irfft(zf * filt_f[:, o], n=2 * L, axis=1)[:, :L]
        z = gate * (y + z * skip[o])
    return z


def gated_delta_chunks(q, k, v, g, beta, s0):
    B, T, H, _ = q.shape
    C = GDN_CHUNK
    N = T // C

    def chunks(a):
        return jnp.moveaxis(a.reshape(B, N, C, H, *a.shape[3:]), 3, 1)

    q, k, v, g, beta = chunks(q), chunks(k), chunks(v), chunks(g), chunks(beta)
    gc = jnp.cumsum(g, axis=-1)
    tri = jnp.tril(jnp.ones((C, C), bool))
    strict = jnp.tril(jnp.ones((C, C), bool), -1)
    decay = jnp.exp(jnp.where(tri, gc[..., :, None] - gc[..., None, :], -jnp.inf))
    kb = k * beta[..., None]
    A = jnp.where(strict, jnp.einsum('bhnid,bhnjd->bhnij', kb, k) * decay, 0.0)
    eye = jnp.eye(C, dtype=jnp.float32)
    Tm = lax.linalg.triangular_solve(eye + A, jnp.broadcast_to(eye, A.shape),
                                     left_side=True, lower=True, unit_diagonal=True)
    u_val = jnp.einsum('bhnij,bhnjd->bhnid', Tm, v * beta[..., None])
    w_val = jnp.einsum('bhnij,bhnjd->bhnid', Tm, kb * jnp.exp(gc)[..., None])
    intra = jnp.einsum('bhnid,bhnjd->bhnij', q, k) * decay
    q_dec = q * jnp.exp(gc)[..., None]
    g_last = gc[..., -1]
    k_dec = k * jnp.exp(g_last[..., None] - gc)[..., None]

    def step(S, xs):
        qd, w, u, att, kd, gl = xs
        v_new = u - jnp.einsum('bhcd,bhde->bhce', w, S)
        o = jnp.einsum('bhcd,bhde->bhce', qd, S) + jnp.einsum('bhij,bhje->bhie', att, v_new)
        S = S * jnp.exp(gl)[..., None, None] + jnp.einsum('bhcd,bhce->bhde', kd, v_new)
        return S, o

    xs = tuple(jnp.moveaxis(a, 2, 0) for a in (q_dec, w_val, u_val, intra, k_dec, g_last))
    s_fin, o = lax.scan(step, s0, xs)
    o = jnp.transpose(o, (1, 0, 3, 2, 4)).reshape(B, T, H, -1)
    return o, s_fin


def gdn_mix(qkv, z, a, b, conv_w, a_log, dt_bias, norm_w, init_states, with_output):
    B, T, _ = qkv.shape
    u = jax.nn.silu(centred_dwconv(qkv, conv_w))
    q, k, v = [t.reshape(B, T, GDN_H, HEAD_DIM) for t in jnp.split(u, 3, axis=-1)]
    q = l2_normalize(q) * HEAD_DIM ** -0.5
    k = l2_normalize(k)
    g = -jnp.exp(a_log) * jax.nn.softplus(a.reshape(B, T, 2, GDN_H) + dt_bias)
    beta = jax.nn.sigmoid(b.reshape(B, T, 2, GDN_H))
    outs, finals = [], []
    for d in range(2):
        rev = (lambda t: jnp.flip(t, 1)) if d == 1 else (lambda t: t)
        o, s_fin = gated_delta_chunks(rev(q), rev(k), rev(v), rev(g[:, :, d]), rev(beta[:, :, d]), init_states[d])
        finals.append(s_fin)
        if with_output:
            outs.append(rev(o))
    if not with_output:
        return None, (finals[0], finals[1])
    o = rms_norm(outs[0] + outs[1], norm_w) * jax.nn.silu(z.reshape(B, T, GDN_H, HEAD_DIM))
    return o.reshape(B, T, GDN_H * HEAD_DIM), (finals[0], finals[1])


def axial_rope_angles(n):
    t = jnp.arange(n)
    row = (t // GRID_W).astype(jnp.float32)
    col = (t % GRID_W).astype(jnp.float32)
    nf = HEAD_DIM // 4
    inv = ROPE_THETA ** (-jnp.arange(nf, dtype=jnp.float32) / nf)
    return row[:, None] * inv, col[:, None] * inv


def rope_half(x, ang):
    x1, x2 = jnp.split(x, 2, axis=-1)
    cos, sin = jnp.cos(ang), jnp.sin(ang)
    return jnp.concatenate([x1 * cos - x2 * sin, x1 * sin + x2 * cos], axis=-1)


def axial_rope(x, ang_r, ang_c):
    half = HEAD_DIM // 2
    ar = ang_r[None, :, None, None, :]
    ac = ang_c[None, :, None, None, :]
    return jnp.concatenate([rope_half(x[..., :half], ar), rope_half(x[..., half:], ac)], axis=-1)


def diff_attend(q, k, v, lam):
    s = jnp.einsum('bqhcd,bkhcd->bhcqk', q, k) * HEAD_DIM ** -0.5
    p = jax.nn.softmax(s, axis=-1)
    a = p[:, :, 0] - lam * p[:, :, 1]
    return jnp.einsum('bhqk,bkhd->bqhd', a, v)


def diff_latent(q, k, v, kc, vc, lam, ang_r, ang_c):
    B, n = q.shape[:2]
    q = axial_rope(q, ang_r, ang_c)
    k_all = jnp.concatenate([axial_rope(k, ang_r, ang_c), kc], axis=1)
    v_all = jnp.concatenate([v, vc], axis=1)
    nb = n // Q_BLOCK
    qb = jnp.swapaxes(q.reshape(B, nb, Q_BLOCK, *q.shape[2:]), 0, 1)
    o = lax.map(lambda qi: diff_attend(qi, k_all, v_all, lam), qb)
    return jnp.swapaxes(o, 0, 1).reshape(B, n, DIFF_H, DIFF_VD)


def diff_finish(o, norm_w, lam_init):
    B, T = o.shape[:2]
    return (rms_norm(o, norm_w) * (1.0 - lam_init)).reshape(B, T, DIFF_H * DIFF_VD)


def split_diff(p):
    B, T, _ = p.shape
    q, k, v = jnp.split(p, 3, axis=-1)
    return (q.reshape(B, T, DIFF_H, 2, HEAD_DIM), k.reshape(B, T, DIFF_H, 2, HEAD_DIM),
            v.reshape(B, T, DIFF_H, DIFF_VD))


def dense_attend(q, k, v):
    s = jnp.einsum('bqhd,bkhd->bhqk', q, k) * HEAD_DIM ** -0.5
    p = jax.nn.softmax(s, axis=-1)
    return jnp.einsum('bhqk,bkhd->bqhd', p, v)


def na_latent(q, k, v, kc, vc, rpb):
    B, n, H, dh = q.shape
    R = n // GRID_W
    kh, kw = min(NA_ROWS, R), NA_COLS
    grid = lambda t: t.reshape(B, R, GRID_W, H, dh)
    kg, vg = grid(k), grid(v)
    cols = jnp.arange(GRID_W)
    cstart = jnp.clip(cols - kw // 2, 0, GRID_W - kw)
    col_ok = (cols[None, :] >= cstart[:, None]) & (cols[None, :] < cstart[:, None] + kw)
    mask = jnp.broadcast_to(col_ok[:, None, :], (GRID_W, kh, GRID_W)).reshape(GRID_W, kh * GRID_W)
    dc = jnp.clip(cols[None, :] - cols[:, None], -(kw - 1), kw - 1) + (kw - 1)
    rpb_c = rpb[:, :, dc]
    scale = dh ** -0.5

    def row(args):
        r, qr = args
        rs = jnp.clip(r - kh // 2, 0, R - kh)
        kr = lax.dynamic_slice_in_dim(kg, rs, kh, axis=1).reshape(B, kh * GRID_W, H, dh)
        vr = lax.dynamic_slice_in_dim(vg, rs, kh, axis=1).reshape(B, kh * GRID_W, H, dh)
        dr = rs + jnp.arange(kh) - r + (NA_ROWS - 1)
        bias = jnp.transpose(rpb_c[:, dr], (0, 2, 1, 3)).reshape(H, GRID_W, kh * GRID_W)
        s_win = jnp.einsum('bqhd,bkhd->bhqk', qr, kr) * scale + bias[None]
        s_win = jnp.where(mask, s_win, -jnp.inf)
        s_ctx = jnp.einsum('bqhd,bkhd->bhqk', qr, kc) * scale
        p = jax.nn.softmax(jnp.concatenate([s_win, s_ctx], axis=-1), axis=-1)
        nw = kh * GRID_W
        return (jnp.einsum('bhqk,bkhd->bqhd', p[..., :nw], vr)
                + jnp.einsum('bhqk,bkhd->bqhd', p[..., nw:], vc))

    o = lax.map(row, (jnp.arange(R), jnp.swapaxes(grid(q), 0, 1)))
    return jnp.swapaxes(o, 0, 1).reshape(B, n, H * dh)


def merge_branches(ys, gate_cols, proj, w_o):
    br = jnp.einsum('btmc,mcd->btmd', jnp.stack(ys, axis=2), proj)
    g = jax.nn.sigmoid(gate_cols.reshape(gate_cols.shape[0], gate_cols.shape[1], N_BRANCH, -1))
    return jnp.sum(g * br, axis=2) @ w_o


def moe_ffn(u, router_w, router_b, w1, w3, w2):
    n_tok, d = u.shape
    s = jax.nn.sigmoid(u @ router_w)
    sel = s + router_b
    per = N_EXPERTS // N_GROUPS
    group_score = lax.top_k(sel.reshape(n_tok, N_GROUPS, per), TOP_K)[0].sum(-1)
    best_group = jnp.argmax(group_score, axis=-1)
    in_group = (jnp.arange(N_EXPERTS) // per)[None, :] == best_group[:, None]
    _, idx = lax.top_k(jnp.where(in_group, sel, -jnp.inf), TOP_K)
    wts = jnp.take_along_axis(s, idx, axis=-1)
    wts = wts / jnp.sum(wts, -1, keepdims=True)
    n_slot = n_tok * TOP_K
    e_flat = idx.reshape(-1).astype(jnp.int32)
    t_flat = jnp.repeat(jnp.arange(n_tok, dtype=jnp.int32), TOP_K)
    w_flat = wts.reshape(-1)
    order = jnp.argsort(e_flat)
    e_s, t_s, w_s = e_flat[order], t_flat[order], w_flat[order]
    counts = jax.ops.segment_sum(jnp.ones_like(e_flat), e_flat, num_segments=N_EXPERTS)
    padded = (counts + MOE_BLOCK - 1) // MOE_BLOCK * MOE_BLOCK
    start = jnp.cumsum(counts) - counts
    pend = jnp.cumsum(padded)
    pstart = pend - padded
    dest = pstart[e_s] + jnp.arange(n_slot, dtype=jnp.int32) - start[e_s]
    n_blk = -(-n_slot // MOE_BLOCK) + N_EXPERTS
    cap = n_blk * MOE_BLOCK
    buf_tok = jnp.full((cap,), n_tok, jnp.int32).at[dest].set(t_s)
    buf_w = jnp.zeros((cap,), jnp.float32).at[dest].set(w_s)
    blk_exp = jnp.minimum(jnp.searchsorted(pend, jnp.arange(n_blk, dtype=jnp.int32) * MOE_BLOCK, side='right'),
                          N_EXPERTS - 1)
    u_pad = jnp.concatenate([u, jnp.zeros((1, d), u.dtype)], axis=0)
    xb = u_pad[buf_tok].reshape(n_blk, MOE_BLOCK, d)

    def expert_block(args):
        xi, e = args
        return (jax.nn.silu(xi @ w1[e]) * (xi @ w3[e])) @ w2[e]

    yb = lax.map(expert_block, (xb, blk_exp)).reshape(cap, d)
    y = jax.ops.segment_sum(yb * buf_w[:, None], buf_tok, num_segments=n_tok + 1)
    return y[:n_tok]


def diff_branch(dqkv_x, dqkv_c, lam, norm_w, lam_init, ang_r, ang_c):
    B, n, _ = dqkv_x.shape
    m = dqkv_c.shape[1]
    q, k, v = jnp.split(dqkv_x, 3, axis=-1)
    kc, vc = dqkv_c[..., MIX_W:2 * MIX_W], dqkv_c[..., 2 * MIX_W:]
    rot = lambda t: axial_rope(t.reshape(B, n, DIFF_H, 2, HEAD_DIM), ang_r, ang_c).reshape(B, n, MIX_W)
    qr = (rot(q) * (HEAD_DIM ** -0.5 * math.log2(math.e))).astype(jnp.bfloat16)
    nk = n + m
    k_all = jnp.concatenate([rot(k), kc], axis=1).astype(jnp.bfloat16).reshape(B, nk, DIFF_H, 2, HEAD_DIM)
    k_all = jnp.transpose(k_all, (0, 2, 3, 1, 4))
    v_all = jnp.concatenate([v, vc], axis=1).astype(jnp.bfloat16).reshape(B, nk // DIFF_TK, DIFF_TK, DIFF_H, DIFF_VD)
    vt_all = jnp.transpose(v_all, (0, 3, 1, 4, 2))
    return diff_attention(qr, k_all, vt_all, lam, norm_w, 1.0 - lam_init)


def na_branch(nqkv_x, nqkv_c, rpb):
    n = nqkv_x.shape[1]
    q, k, v = jnp.split(nqkv_x, 3, axis=-1)
    kc, vc = nqkv_c[..., MIX_W:2 * MIX_W], nqkv_c[..., 2 * MIX_W:]
    bf = lambda t: t.astype(jnp.bfloat16)
    tabs = na_bias_tables(rpb, n // GRID_W)
    return na_attention(bf(q * HEAD_DIM ** -0.5), bf(k), bf(v), bf(kc), bf(vc), tabs)


def _split_w_in(w):
    segs = jnp.split(w, SPLIT_IDX, axis=-1)
    ab = jnp.concatenate([segs[3], segs[4]], axis=-1)
    ab = jnp.pad(ab, ((0, 0), (0, LANE - ab.shape[-1])))
    main = jnp.concatenate([segs[7], segs[0], segs[1], segs[2], segs[5], segs[6]], axis=-1)
    return main.astype(jnp.bfloat16), ab.astype(jnp.bfloat16)


MAIN_SIZES = (N_BRANCH * D_MODEL, 3 * HY_W, 3 * MIX_W, MIX_W, 3 * MIX_W, 3 * MIX_W)
MAIN_OFF = tuple(sum(MAIN_SIZES[:i]) for i in range(len(MAIN_SIZES) + 1))
IN_TN = 1792


def kernel(x, c, ctx, c_ctx, w_mod, b_mod, w_in, hy_conv, hy_w1, hy_b1, hy_w2, hy_b2, hy_w3, hy_freq,
           hy_deltas, hy_skip, gdn_conv, gdn_a_log, gdn_dt_bias, gdn_norm, diff_lam, diff_norm, na_rpb,
           branch_proj, w_out, ln_g, ln_b, router_w, router_b, exp_w1, exp_w3, exp_w2):
    B, n, D = x.shape
    m = ctx.shape[1]
    depth = w_mod.shape[0]
    dn_alpha = (2 * depth) ** 0.25
    ang_r, ang_c = axial_rope_angles(n)
    zero_state = jnp.zeros((B, GDN_H, HEAD_DIM, HEAD_DIM), jnp.float32)
    bf16 = jnp.bfloat16
    n_lat = B * n
    dft_tables = _dft_tables(n)
    tok = jnp.concatenate([x.reshape(n_lat, D), ctx.reshape(B * m, D)], axis=0)
    for l in range(depth):
        ctx_out = l < depth - 1
        mx = jax.nn.silu(c) @ w_mod[l] + b_mod[l]
        mc = jax.nn.silu(c_ctx) @ w_mod[l] + b_mod[l]
        one_plus = jnp.array([0.0, 1.0, 0.0, 0.0, 1.0, 0.0], jnp.float32)[:, None]
        mod_x = mx.reshape(B, 6, D) + one_plus
        mod_c = mc.reshape(1, 6, D) + one_plus
        mod = jnp.concatenate([jnp.repeat(mod_x, n // MERGE_TILE, axis=0),
                               jnp.repeat(mod_c, B * m // MERGE_TILE, axis=0)], axis=0)

        proj, gab = ln_mod_matmul(tok, mod, *_split_w_in(w_in[l]), IN_TN)

        def seg(i, latent):
            rows = slice(0, n_lat) if latent else slice(n_lat, None)
            return proj[rows, MAIN_OFF[i]:MAIN_OFF[i + 1]].reshape(B, -1, MAIN_SIZES[i]).astype(jnp.float32)

        hy_x, gqkv_x, gz_x, dqkv_x, nqkv_x = [seg(i, True) for i in range(1, 6)]
        hy_c, gqkv_c, gz_c, dqkv_c, nqkv_c = [seg(i, False) for i in range(1, 6)]
        gab_x, gab_c = gab[:n_lat].reshape(B, n, LANE), gab[n_lat:].reshape(B, m, LANE)
        ga_x, gb_x = gab_x[..., :2 * GDN_H], gab_x[..., 2 * GDN_H:4 * GDN_H]
        ga_c, gb_c = gab_c[..., :2 * GDN_H], gab_c[..., 2 * GDN_H:4 * GDN_H]
        filt = (hy_w1[l], hy_b1[l], hy_w2[l], hy_b2[l], hy_w3[l], hy_freq[l], hy_deltas[l])

        ya_x = hyena_latent(hy_x, hy_conv[l], hy_skip[l], filt, dft_tables)

        o_f, o_b = gdn_branch(gqkv_x, ga_x, gb_x, gqkv_c, ga_c, gb_c, gdn_conv[l], gdn_a_log[l], gdn_dt_bias[l])

        lq1, lk1, lq2, lk2 = diff_lam[l]
        lam_init = 0.8 - 0.6 * math.exp(-0.3 * l)
        lam = jnp.exp(jnp.sum(lq1 * lk1)) - jnp.exp(jnp.sum(lq2 * lk2)) + lam_init
        yc_x = diff_branch(dqkv_x, dqkv_c, lam, diff_norm[l], lam_init, ang_r, ang_c)
        yd_x = na_branch(nqkv_x, nqkv_c, na_rpb[l])

        ys = [t.reshape(n_lat, MIX_W) for t in (ya_x, yc_x, yd_x)]
        if ctx_out:
            dq_c, dk_c, dv_c = split_diff(dqkv_c)
            nq_c, nk_c, nv_c = [t.reshape(B, m, NA_H, HEAD_DIM) for t in jnp.split(nqkv_c, 3, axis=-1)]
            ya_c = hyena_mix(hy_c, hy_conv[l], hy_skip[l], hyena_filters(m, *filt))
            yc_c = diff_finish(diff_attend(dq_c, dk_c, dv_c, lam), diff_norm[l], lam_init)
            yd_c = dense_attend(nq_c, nk_c, nv_c)
            ys_c = [t.reshape(B * m, MIX_W) for t in (ya_c, yc_c, yd_c)]
            ys = [jnp.concatenate([a, b], axis=0) for a, b in zip(ys, ys_c)]
        else:
            tok = tok[:n_lat]
        tok = merge_residual_ln(tok, ys[0], o_f, o_b, ys[1], ys[2], proj, mod, branch_proj[l].astype(bf16),
                                w_out[l].astype(bf16), gdn_norm[l], ln_g[l, 0], ln_b[l, 0], dn_alpha, n, m)

        n_tok = tok.shape[0]
        h2, idx, wts = moe_route(tok, mod, router_w, router_b)
        buf_tok, dest, blk_exp, n_used = moe_dispatch(idx, n_tok)
        yb = expert_ffn(blk_exp, n_used, jnp.take(h2, buf_tok, axis=0), exp_w1[l], exp_w3[l], exp_w2[l])
        tok = moe_combine_ln(tok, yb, dest.reshape(n_tok, TOP_K), wts, mod, ln_g[l, 1], ln_b[l, 1], dn_alpha)
    return tok[:n_lat].reshape(B, n, D)
```

```python
import functools
import math

import jax
import jax.numpy as jnp
from jax import lax
from jax.experimental import pallas as pl
from jax.experimental.pallas import tpu as pltpu

D_MODEL = 1024
GRID_W = 64
HEAD_DIM = 64
MIX_W = D_MODEL // 2
N_BRANCH = 4
HY_W = MIX_W
HY_ORDER = 2
HY_BANDS = 16
HY_MOD_SHIFT = 0.05
GDN_H = MIX_W // HEAD_DIM
GDN_CHUNK = 64
GDN_CONV = 3
DIFF_VD = 2 * HEAD_DIM
DIFF_H = MIX_W // DIFF_VD
NA_H = MIX_W // HEAD_DIM
NA_ROWS = 8
NA_COLS = 16
N_EXPERTS = 16
N_GROUPS = 4
TOP_K = 2
MOE_BLOCK = 128
Q_BLOCK = 128
ROPE_THETA = 10000.0
LN_EPS = 1e-5
RMS_EPS = 1e-6
SPLIT_SIZES = (3 * HY_W, 3 * MIX_W, MIX_W, 2 * GDN_H, 2 * GDN_H, 3 * MIX_W, 3 * MIX_W, N_BRANCH * D_MODEL)
SPLIT_IDX = tuple(sum(SPLIT_SIZES[:i + 1]) for i in range(len(SPLIT_SIZES) - 1))

LANE = 128
ROW_TILE = 512
MERGE_TILE = 256
VMEM_LIMIT = 48 * 1024 * 1024


def _ln_mod_matmul_kernel(x_ref, mod_ref, w_ref, wg_ref, o_ref, og_ref, h_ref):
    @pl.when(pl.program_id(1) == 0)
    def _():
        x = x_ref[...]
        mu = jnp.mean(x, axis=-1, keepdims=True)
        xc = x - mu
        var = jnp.mean(xc * xc, axis=-1, keepdims=True)
        y = xc * lax.rsqrt(var + LN_EPS)
        h = (y * mod_ref[0, 1:2, :] + mod_ref[0, 0:1, :]).astype(h_ref.dtype)
        h_ref[...] = h
        og_ref[...] = jnp.dot(h, wg_ref[...], preferred_element_type=jnp.float32)

    o_ref[...] = jnp.dot(h_ref[...], w_ref[...], preferred_element_type=jnp.float32).astype(o_ref.dtype)


def ln_mod_matmul(x, mod, w, w_gates, tn):
    T, D = x.shape
    N = w.shape[1]
    return pl.pallas_call(
        _ln_mod_matmul_kernel,
        grid=(T // ROW_TILE, N // tn),
        in_specs=[
            pl.BlockSpec((ROW_TILE, D), lambda i, j: (i, 0)),
            pl.BlockSpec((1, 6, D), lambda i, j: (i * (ROW_TILE // MERGE_TILE), 0, 0)),
            pl.BlockSpec((D, tn), lambda i, j: (0, j)),
            pl.BlockSpec((D, LANE), lambda i, j: (0, 0)),
        ],
        out_specs=[pl.BlockSpec((ROW_TILE, tn), lambda i, j: (i, j)),
                   pl.BlockSpec((ROW_TILE, LANE), lambda i, j: (i, 0))],
        out_shape=[jax.ShapeDtypeStruct((T, N), jnp.bfloat16), jax.ShapeDtypeStruct((T, LANE), jnp.float32)],
        scratch_shapes=[pltpu.VMEM((ROW_TILE, D), jnp.bfloat16)],
        compiler_params=pltpu.CompilerParams(
            dimension_semantics=("arbitrary", "arbitrary"), vmem_limit_bytes=VMEM_LIMIT),
        name="ln_mod_matmul",
    )(x, mod, w, w_gates)


def _layer_norm_rows(x):
    mu = jnp.mean(x, axis=-1, keepdims=True)
    xc = x - mu
    var = jnp.mean(xc * xc, axis=-1, keepdims=True)
    return xc * lax.rsqrt(var + LN_EPS)


_NT_DIMS = (((1,), (1,)), ((), ()))


DIFF_TQ = 256
DIFF_TK = 384
DIFF_ROWS = 32


def _diff_attn_kernel(lam_ref, q_ref, k_ref, vt_ref, nw_ref, o_ref, s_ref, *, out_scale):
    tq = q_ref.shape[1]
    n_tiles = k_ref.shape[3] // DIFF_TK
    q = q_ref[0]
    qc = (q[:, :HEAD_DIM], q[:, HEAD_DIM:])

    sub = 8
    fold = lambda t, op: op(t.reshape(t.shape[0] // sub, sub, t.shape[1]), axis=0)

    def scores(j, slot):
        start = pl.multiple_of(j * DIFF_TK, DIFF_TK)
        mx = []
        for c in range(2):
            k = k_ref[0, 0, c, pl.ds(start, DIFF_TK), :]
            st = lax.dot_general(k, qc[c], _NT_DIMS, preferred_element_type=jnp.float32)
            s_ref[slot, c] = st
            mx.append(fold(st, jnp.max))
        return tuple(mx)

    def softmax_pv(j, slot, mx, carry):
        new, ps = [], []
        for c in range(2):
            m_prev, l_prev, acc = carry[c]
            m_new = jnp.maximum(m_prev, jnp.max(mx[c], axis=0, keepdims=True))
            alpha = jnp.exp2(m_prev - m_new)
            psum, chunks = None, []
            for r in range(DIFF_TK // DIFF_ROWS):
                p = jnp.exp2(s_ref[slot, c, r * DIFF_ROWS:(r + 1) * DIFF_ROWS, :] - m_new)
                part = fold(p, jnp.sum)
                psum = part if psum is None else psum + part
                chunks.append(p.astype(jnp.bfloat16))
            new.append((m_new, alpha * l_prev + jnp.sum(psum, axis=0, keepdims=True), alpha * acc))
            ps.append(jnp.concatenate(chunks, axis=0))
        pv = jnp.dot(vt_ref[0, 0, j], jnp.concatenate(ps, axis=1), preferred_element_type=jnp.float32)
        return tuple((new[c][0], new[c][1], new[c][2] + pv[:, c * tq:(c + 1) * tq]) for c in range(2))

    def pair(jj, state):
        carry, mx0 = state
        j0 = 2 * jj
        mx1 = scores(j0 + 1, 1)
        carry = softmax_pv(j0, 0, mx0, carry)
        mx0 = scores(j0 + 2, 0)
        return softmax_pv(j0 + 1, 1, mx1, carry), mx0

    carry = tuple((jnp.full((1, tq), -jnp.inf, jnp.float32), jnp.zeros((1, tq), jnp.float32),
                   jnp.zeros((DIFF_VD, tq), jnp.float32)) for _ in range(2))
    mx0 = scores(0, 0)
    n_pairs = (n_tiles - 1) // 2
    carry, mx0 = lax.fori_loop(0, n_pairs, pair, (carry, mx0))
    if n_tiles % 2 == 0:
        mx1 = scores(n_tiles - 1, 1)
        carry = softmax_pv(n_tiles - 2, 0, mx0, carry)
        carry = softmax_pv(n_tiles - 1, 1, mx1, carry)
    else:
        carry = softmax_pv(n_tiles - 1, 0, mx0, carry)
    (_, l0, a0), (_, l1, a1) = carry
    o = a0 / l0 - lam_ref[0] * (a1 / l1)
    o = o * lax.rsqrt(jnp.mean(o * o, axis=0, keepdims=True) + RMS_EPS) * (nw_ref[...] * out_scale)
    o_ref[0] = o.T.astype(o_ref.dtype)


def diff_attention(q, k, vt, lam, norm_w, out_scale, out_dtype=jnp.float32):
    B, nq, W = q.shape
    nk = k.shape[3]
    assert nq % DIFF_TQ == 0 and nk % DIFF_TK == 0
    H = W // DIFF_VD
    return pl.pallas_call(
        functools.partial(_diff_attn_kernel, out_scale=out_scale),
        grid=(B, H, nq // DIFF_TQ),
        in_specs=[
            pl.BlockSpec(memory_space=pltpu.SMEM),
            pl.BlockSpec((1, DIFF_TQ, DIFF_VD), lambda b, h, i: (b, i, h)),
            pl.BlockSpec((1, 1, 2, nk, HEAD_DIM), lambda b, h, i: (b, h, 0, 0, 0)),
            pl.BlockSpec((1, 1, nk // DIFF_TK, DIFF_VD, DIFF_TK), lambda b, h, i: (b, h, 0, 0, 0)),
            pl.BlockSpec((DIFF_VD, 1), lambda b, h, i: (0, 0)),
        ],
        out_specs=pl.BlockSpec((1, DIFF_TQ, DIFF_VD), lambda b, h, i: (b, i, h)),
        out_shape=jax.ShapeDtypeStruct((B, nq, W), out_dtype),
        scratch_shapes=[pltpu.VMEM((2, 2, DIFF_TK, DIFF_TQ), jnp.float32)],
        compiler_params=pltpu.CompilerParams(
            dimension_semantics=("arbitrary",) * 3, vmem_limit_bytes=VMEM_LIMIT),
        name="diff_attention",
    )(lam.reshape(1).astype(jnp.float32), q, k, vt, norm_w.reshape(DIFF_VD, 1))


NA_QROWS = 8
NA_KROWS = 16
NA_NEG = -1e30


def _na_kernel(q_ref, k_ref, v_ref, kc_ref, vc_ref, tab_ref, o_ref, *, n_kblocks):
    g = pl.program_id(2)
    kb = (NA_KROWS // 4) * GRID_W
    base = jnp.clip(2 * g - 1, 0, n_kblocks - 4)
    start = pl.multiple_of(base * kb, kb)
    nwin = NA_KROWS * GRID_W
    q = q_ref[...] * HEAD_DIM ** -0.5
    kw = k_ref[pl.ds(start, nwin), :]
    vw = v_ref[pl.ds(start, nwin), :]
    kc = kc_ref[...]
    vc = vc_ref[...]
    outs = []
    for hh in range(2):
        sl = slice(hh * HEAD_DIM, (hh + 1) * HEAD_DIM)
        qh = q[:, sl]
        s = lax.dot_general(qh, kw[:, sl], _NT_DIMS, preferred_element_type=jnp.float32) + tab_ref[0, hh]
        sc = lax.dot_general(qh, kc[:, sl], _NT_DIMS, preferred_element_type=jnp.float32)
        m = jnp.maximum(jnp.max(s, axis=-1, keepdims=True), jnp.max(sc, axis=-1, keepdims=True))
        p = jnp.exp(s - m)
        pc = jnp.exp(sc - m)
        l = jnp.sum(p, axis=-1, keepdims=True) + jnp.sum(pc, axis=-1, keepdims=True)
        o = (jnp.dot(p.astype(vw.dtype), vw[:, sl], preferred_element_type=jnp.float32)
             + jnp.dot(pc.astype(vc.dtype), vc[:, sl], preferred_element_type=jnp.float32))
        outs.append(o / l)
    o_ref[...] = jnp.concatenate(outs, axis=-1).astype(o_ref.dtype)


def na_bias_tables(rpb, R):
    H = rpb.shape[0]
    G = R // NA_QROWS
    cols = jnp.arange(GRID_W)
    cstart = jnp.clip(cols - NA_COLS // 2, 0, GRID_W - NA_COLS)
    col_ok = (cols[None, :] >= cstart[:, None]) & (cols[None, :] < cstart[:, None] + NA_COLS)
    dc = jnp.clip(cols[None, :] - cols[:, None], -(NA_COLS - 1), NA_COLS - 1) + (NA_COLS - 1)
    rpb_c = rpb[:, :, dc]
    tabs = []
    for g in (0, 1, G - 1):
        base = min(max(2 * g - 1, 0), R // 4 - 4)
        r = NA_QROWS * g + jnp.arange(NA_QROWS)
        rs = jnp.clip(r - NA_ROWS // 2, 0, R - NA_ROWS)
        kr = 4 * base + jnp.arange(NA_KROWS)
        valid = (kr[None, :] >= rs[:, None]) & (kr[None, :] < rs[:, None] + NA_ROWS)
        dr = jnp.clip(kr[None, :] - r[:, None] + (NA_ROWS - 1), 0, 2 * NA_ROWS - 2)
        tab = rpb_c[:, dr]
        ok = valid[None, :, :, None, None] & col_ok[None, None, None, :, :]
        tab = jnp.where(ok, tab, NA_NEG)
        tabs.append(jnp.transpose(tab, (0, 1, 3, 2, 4)).reshape(H, NA_QROWS * GRID_W, NA_KROWS * GRID_W))
    return jnp.stack(tabs, axis=0)


def na_attention(proj_main, tabs, B, n, m, out_dtype=jnp.float32):
    W = MIX_W
    tq = NA_QROWS * GRID_W
    G = n // tq
    n_kblocks = n // ((NA_KROWS // 4) * GRID_W)
    cq, ck, cv = [(MAIN_OFF[5] + p * W) // LANE for p in range(3)]
    ctx0 = B * n // m

    def tab_index(b, hp, g):
        return (jnp.where(g == 0, 0, jnp.where(g == G - 1, 2, 1)), hp, 0, 0)

    return pl.pallas_call(
        functools.partial(_na_kernel, n_kblocks=n_kblocks),
        grid=(B, W // LANE, G),
        in_specs=[
            pl.BlockSpec((tq, LANE), lambda b, hp, g: (b * G + g, cq + hp)),
            pl.BlockSpec((n, LANE), lambda b, hp, g: (b, ck + hp)),
            pl.BlockSpec((n, LANE), lambda b, hp, g: (b, cv + hp)),
            pl.BlockSpec((m, LANE), lambda b, hp, g: (ctx0 + b, ck + hp)),
            pl.BlockSpec((m, LANE), lambda b, hp, g: (ctx0 + b, cv + hp)),
            pl.BlockSpec((1, 2, tq, NA_KROWS * GRID_W), tab_index),
        ],
        out_specs=pl.BlockSpec((tq, LANE), lambda b, hp, g: (b * G + g, hp)),
        out_shape=jax.ShapeDtypeStruct((B * n, W), out_dtype),
        compiler_params=pltpu.CompilerParams(
            dimension_semantics=("arbitrary",) * 3, vmem_limit_bytes=VMEM_LIMIT),
        name="na_attention",
    )(proj_main, proj_main, proj_main, proj_main, proj_main, tabs)


def _merge_kernel(x_ref, ya_ref, of_ref, ob_ref, yc_ref, yd_ref, gc_ref, gz_ref, mod_ref, p_ref, wo_ref,
                  gn_ref, hs_ref, lng_ref, lnb_ref, o_ref, *, alpha):
    o = of_ref[0] + ob_ref[0]
    sq = o * o
    hi = sq.astype(jnp.bfloat16)
    lo = (sq - hi.astype(jnp.float32)).astype(jnp.bfloat16)
    ms = (jnp.dot(hi, hs_ref[...], preferred_element_type=jnp.float32)
          + jnp.dot(lo, hs_ref[...], preferred_element_type=jnp.float32)) * (1.0 / HEAD_DIM)
    gz = gz_ref[...].astype(jnp.float32)
    yb = o * lax.rsqrt(ms + RMS_EPS) * gn_ref[...] * (gz * jax.nn.sigmoid(gz))
    acc = None
    for mi, y in enumerate((ya_ref[...], yb, yc_ref[...], yd_ref[...])):
        br = jnp.dot(y.astype(jnp.bfloat16), p_ref[mi], preferred_element_type=jnp.float32)
        gm = jax.nn.sigmoid(gc_ref[:, mi * D_MODEL:(mi + 1) * D_MODEL].astype(jnp.float32))
        acc = gm * br if acc is None else acc + gm * br
    mix = jnp.dot(acc.astype(jnp.bfloat16), wo_ref[...], preferred_element_type=jnp.float32)
    z = alpha * x_ref[...] + mod_ref[0, 2:3, :] * mix
    o_ref[...] = _layer_norm_rows(z) * lng_ref[...] + lnb_ref[...]


def merge_residual_ln(x, ya, o_f, o_b, yc, yd, proj_main, mod, proj, w_o, gdn_norm, ln_g, ln_b, alpha, n, m):
    T, D = x.shape
    R = MERGE_TILE
    lat_tiles, ctx_tiles = n // R, m // R
    scan_rows = functools.partial(_stream_to_scan_rows, n_lat_tiles=o_f.shape[0] * lat_tiles,
                                  lat_tiles=lat_tiles, ctx_tiles=ctx_tiles)
    row = lambda i: (i, 0)
    const2 = lambda i: (0, 0)
    branch = pl.BlockSpec((R, MIX_W), row)
    scan = pl.BlockSpec((1, R, MIX_W), scan_rows)
    head_sum = (jnp.arange(MIX_W)[:, None] // HEAD_DIM == jnp.arange(MIX_W)[None, :] // HEAD_DIM).astype(jnp.bfloat16)
    return pl.pallas_call(
        functools.partial(_merge_kernel, alpha=alpha),
        grid=(T // R,),
        in_specs=[pl.BlockSpec((R, D), row), branch, scan, scan, branch, branch,
                  pl.BlockSpec((R, N_BRANCH * D), row),
                  pl.BlockSpec((R, MIX_W), lambda i: (i, MAIN_OFF[3] // MIX_W)),
                  pl.BlockSpec((1, 6, D), lambda i: (i, 0, 0)),
                  pl.BlockSpec((N_BRANCH, MIX_W, D), lambda i: (0, 0, 0)),
                  pl.BlockSpec((D, D), const2),
                  pl.BlockSpec((1, MIX_W), const2),
                  pl.BlockSpec((MIX_W, MIX_W), const2),
                  pl.BlockSpec((1, D), const2),
                  pl.BlockSpec((1, D), const2)],
        out_specs=pl.BlockSpec((R, D), row),
        out_shape=jax.ShapeDtypeStruct((T, D), jnp.float32),
        compiler_params=pltpu.CompilerParams(dimension_semantics=("arbitrary",), vmem_limit_bytes=VMEM_LIMIT),
        name="merge_residual_ln",
    )(x, ya, o_f, o_b, yc, yd, proj_main, proj_main, mod, proj, w_o,
      jnp.tile(gdn_norm, GDN_H).reshape(1, MIX_W), head_sum, ln_g.reshape(1, D), ln_b.reshape(1, D))


def _route_kernel(x_ref, mod_ref, rw_ref, rb_ref, h_ref, idx_ref, wts_ref):
    h = (_layer_norm_rows(x_ref[...]) * mod_ref[0, 4:5, :] + mod_ref[0, 3:4, :]).astype(jnp.bfloat16)
    h_ref[...] = h.astype(h_ref.dtype)
    logits = lax.dot_general(rw_ref[...], h, _NT_DIMS, preferred_element_type=jnp.float32)
    s = jax.nn.sigmoid(logits)
    sel = s + rb_ref[...]
    per = N_EXPERTS // N_GROUPS
    srow = [s[e:e + 1, :] for e in range(N_EXPERTS)]
    vrow = [sel[e:e + 1, :] for e in range(N_EXPERTS)]
    best = None
    for gi in range(N_GROUPS):
        grp = vrow[gi * per:(gi + 1) * per]
        gs = None
        for a in range(per):
            for b in range(a + 1, per):
                ps = grp[a] + grp[b]
                gs = ps if gs is None else jnp.maximum(gs, ps)
        if best is None:
            best, bg = gs, jnp.zeros(gs.shape, jnp.int32)
        else:
            upd = gs > best
            bg = jnp.where(upd, gi, bg)
            best = jnp.where(upd, gs, best)
    cv, cs = [], []
    for j in range(per):
        v_j, s_j = vrow[j], srow[j]
        for gi in range(1, N_GROUPS):
            v_j = jnp.where(bg == gi, vrow[gi * per + j], v_j)
            s_j = jnp.where(bg == gi, srow[gi * per + j], s_j)
        cv.append(v_j)
        cs.append(s_j)

    def first_argmax(vals):
        bv, bi = vals[0], jnp.zeros(vals[0].shape, jnp.int32)
        for j in range(1, per):
            upd = vals[j] > bv
            bi = jnp.where(upd, j, bi)
            bv = jnp.where(upd, vals[j], bv)
        return bi

    i1 = first_argmax(cv)
    i2 = first_argmax([jnp.where(i1 == j, -jnp.inf, cv[j]) for j in range(per)])
    w1 = cs[0]
    w2 = cs[0]
    for j in range(1, per):
        w1 = jnp.where(i1 == j, cs[j], w1)
        w2 = jnp.where(i2 == j, cs[j], w2)
    tot = w1 + w2
    idx_ref[...] = jnp.concatenate([bg * per + i1, bg * per + i2], axis=0)
    wts_ref[...] = jnp.concatenate([w1 / tot, w2 / tot], axis=0)


def moe_route(x, mod, router_w, router_b):
    T, D = x.shape
    return pl.pallas_call(
        _route_kernel,
        grid=(T // MERGE_TILE,),
        in_specs=[pl.BlockSpec((MERGE_TILE, D), lambda i: (i, 0)),
                  pl.BlockSpec((1, 6, D), lambda i: (i, 0, 0)),
                  pl.BlockSpec((N_EXPERTS, D), lambda i: (0, 0)),
                  pl.BlockSpec((N_EXPERTS, 1), lambda i: (0, 0))],
        out_specs=[pl.BlockSpec((MERGE_TILE, D), lambda i: (i, 0)),
                   pl.BlockSpec((TOP_K, MERGE_TILE), lambda i: (0, i)),
                   pl.BlockSpec((TOP_K, MERGE_TILE), lambda i: (0, i))],
        out_shape=[jax.ShapeDtypeStruct((T, D), jnp.float32),
                   jax.ShapeDtypeStruct((TOP_K, T), jnp.int32),
                   jax.ShapeDtypeStruct((TOP_K, T), jnp.float32)],
        compiler_params=pltpu.CompilerParams(dimension_semantics=("arbitrary",), vmem_limit_bytes=VMEM_LIMIT),
        name="moe_route",
    )(x, mod, router_w.T.astype(jnp.bfloat16), router_b.reshape(N_EXPERTS, 1).astype(jnp.float32))


EXPERT_BLOCK = 256


def _expert_kernel(be_ref, nu_ref, x_ref, w1_ref, w3_ref, w2_ref, o_ref):
    i = pl.program_id(0)

    @pl.when(i < nu_ref[0])
    def _():
        x = x_ref[...].astype(jnp.bfloat16)
        bf = lambda ref: ref[0].astype(jnp.bfloat16)
        a = jnp.dot(x, bf(w1_ref), preferred_element_type=jnp.float32)
        b = jnp.dot(x, bf(w3_ref), preferred_element_type=jnp.float32)
        hmid = (a * jax.nn.sigmoid(a) * b).astype(jnp.bfloat16)
        o_ref[...] = jnp.dot(hmid, bf(w2_ref), preferred_element_type=jnp.float32).astype(o_ref.dtype)

    @pl.when(i >= nu_ref[0])
    def _():
        o_ref[...] = jnp.zeros(o_ref.shape, o_ref.dtype)


def expert_ffn(blk_exp, n_used, xb, w1, w3, w2, out_dtype=jnp.float32):
    cap, D = xb.shape
    F = w1.shape[-1]
    n_blk = cap // EXPERT_BLOCK
    return pl.pallas_call(
        _expert_kernel,
        grid_spec=pltpu.PrefetchScalarGridSpec(
            num_scalar_prefetch=2,
            grid=(n_blk,),
            in_specs=[pl.BlockSpec((EXPERT_BLOCK, D), lambda i, be, nu: (i, 0)),
                      pl.BlockSpec((1, D, F), lambda i, be, nu: (be[i], 0, 0)),
                      pl.BlockSpec((1, D, F), lambda i, be, nu: (be[i], 0, 0)),
                      pl.BlockSpec((1, F, D), lambda i, be, nu: (be[i], 0, 0))],
            out_specs=pl.BlockSpec((EXPERT_BLOCK, D), lambda i, be, nu: (i, 0)),
        ),
        out_shape=jax.ShapeDtypeStruct((cap, D), out_dtype),
        compiler_params=pltpu.CompilerParams(dimension_semantics=("arbitrary",), vmem_limit_bytes=VMEM_LIMIT),
        name="expert_ffn",
    )(blk_exp, n_used, xb, w1, w3, w2)


def _gdn_kernel(qf_ref, kf_ref, vf_ref, gf_ref, qb_ref, kb_ref, vb_ref, gb_ref, of_ref, ob_ref, s_ref):
    C = GDN_CHUNK
    nb = qf_ref.shape[0]

    @pl.when(pl.program_id(0) == 0)
    def _():
        s_ref[...] = jnp.zeros(s_ref.shape, jnp.float32)

    bf = lambda t: t.astype(jnp.bfloat16)
    mm = lambda a, b: jnp.dot(bf(a), bf(b), preferred_element_type=jnp.float32)
    nt = lambda a, b: lax.dot_general(bf(a), bf(b), _NT_DIMS, preferred_element_type=jnp.float32)
    tn = lambda a, b: lax.dot_general(bf(a), bf(b), (((0,), (0,)), ((), ())), preferred_element_type=jnp.float32)
    col = lambda t, h: t[:, h:h + 1]
    rel = lax.broadcasted_iota(jnp.int32, (C, C), 0) - lax.broadcasted_iota(jnp.int32, (C, C), 1)

    qs, ks, vs, decay, beta_c, e_gc, e_rest, e_tot, strict = [], [], [], [], [], [], [], [], []
    for bi in range(nb):
        for d, refs in enumerate(((qf_ref, kf_ref, vf_ref, gf_ref), (qb_ref, kb_ref, vb_ref, gb_ref))):
            q_ref, k_ref, v_ref, g_ref = refs
            incl = rel >= 0 if d == 0 else rel <= 0
            g = g_ref[bi]
            gc = jnp.dot(incl.astype(jnp.float32), g, precision=lax.Precision.HIGHEST,
                         preferred_element_type=jnp.float32)
            gc_t = gc.T
            tot = jnp.sum(g, axis=0, keepdims=True)
            eg, er, et = jnp.exp(gc), jnp.exp(tot - gc), jnp.exp(tot)
            q, k, v = q_ref[bi], k_ref[bi], v_ref[bi]
            for h in range(GDN_H):
                sl = slice(h * HEAD_DIM, (h + 1) * HEAD_DIM)
                gl = d * GDN_H + h
                qs.append(q[:, sl])
                ks.append(k[:, sl])
                vs.append(v[:, sl])
                decay.append(jnp.where(incl, jnp.exp(col(gc, gl) - gc_t[gl:gl + 1, :]), 0.0))
                strict.append(rel > 0 if d == 0 else rel < 0)
                beta_c.append(col(g, 2 * GDN_H + gl))
                e_gc.append(col(eg, gl))
                e_rest.append(col(er, gl))
                e_tot.append(col(et, gl))
    chains = range(len(qs))
    kb = [ks[c] * beta_c[c] for c in chains]
    a_mat = [jnp.where(strict[c], nt(kb[c], ks[c]) * decay[c], 0.0) for c in chains]
    intra = [nt(qs[c], ks[c]) * decay[c] for c in chains]
    ri = lax.broadcasted_iota(jnp.int32, (C, C), 0)
    ci = lax.broadcasted_iota(jnp.int32, (C, C), 1)

    def level_mask(s, upper):
        sh = s.bit_length() - 1
        same = (ri >> (sh + 1)) == (ci >> (sh + 1))
        r_hi = ((ri >> sh) & 1) == 1
        c_hi = ((ci >> sh) & 1) == 1
        return same & c_hi & ~r_hi if upper else same & r_hi & ~c_hi

    masks = {s: (level_mask(s, False), level_mask(s, True)) for s in (1, 2, 4, 8, 16, 32)}
    upper = [(c // GDN_H) % 2 == 1 for c in chains]
    eye = (ri == ci).astype(jnp.float32)
    t_inv = [eye - jnp.where(masks[1][upper[c]], a_mat[c], 0.0) for c in chains]
    for s in (2, 4, 8, 16, 32):
        tm = [mm(t_inv[c], jnp.where(masks[s][upper[c]], a_mat[c], 0.0)) for c in chains]
        t_inv = [t_inv[c] - mm(tm[c], t_inv[c]) for c in chains]
    xs = [mm(t_inv[c], jnp.concatenate([vs[c] * beta_c[c], kb[c] * e_gc[c]], axis=1)) for c in chains]
    st = [s_ref[c] for c in chains]
    v_new = [xs[c][:, :HEAD_DIM] - mm(xs[c][:, HEAD_DIM:], st[c]) for c in chains]
    o_st = [mm(qs[c] * e_gc[c], st[c]) for c in chains]
    o_in = [mm(intra[c], v_new[c]) for c in chains]
    s_up = [tn(ks[c] * e_rest[c], v_new[c]) for c in chains]
    for c in chains:
        s_ref[c] = st[c] * e_tot[c] + s_up[c]
    outs = [o_st[c] + o_in[c] for c in chains]
    for bi in range(nb):
        for d, o_ref in enumerate((of_ref, ob_ref)):
            c0 = (bi * 2 + d) * GDN_H
            o_ref[bi] = jnp.concatenate(outs[c0:c0 + GDN_H], axis=1)


def gdn_scan(q, k, v, gates, n_ctx_chunks):
    B, Tt, W = q.shape
    C = GDN_CHUNK
    NC = Tt // C

    def bwd(s):
        return jnp.where(s < n_ctx_chunks, n_ctx_chunks - 1 - s, NC - 1 - (s - n_ctx_chunks))

    tok_f = pl.BlockSpec((B, C, W), lambda s: (0, s, 0))
    tok_b = pl.BlockSpec((B, C, W), lambda s: (0, bwd(s), 0))
    gate_f = pl.BlockSpec((B, C, LANE), lambda s: (0, s, 0))
    gate_b = pl.BlockSpec((B, C, LANE), lambda s: (0, bwd(s), 0))
    return pl.pallas_call(
        _gdn_kernel,
        grid=(NC,),
        in_specs=[tok_f, tok_f, tok_f, gate_f, tok_b, tok_b, tok_b, gate_b],
        out_specs=[tok_f, tok_b],
        out_shape=[jax.ShapeDtypeStruct((B, Tt, W), jnp.float32)] * 2,
        scratch_shapes=[pltpu.VMEM((B * 2 * GDN_H, HEAD_DIM, HEAD_DIM), jnp.float32)],
        compiler_params=pltpu.CompilerParams(dimension_semantics=("arbitrary",), vmem_limit_bytes=VMEM_LIMIT),
        name="gdn_scan",
    )(q, k, v, gates, q, k, v, gates)


def _stream_to_scan_rows(i, n_lat_tiles, lat_tiles, ctx_tiles):
    ic = i - n_lat_tiles
    return (jnp.where(i < n_lat_tiles, i // lat_tiles, ic // ctx_tiles),
            jnp.where(i < n_lat_tiles, ctx_tiles + i % lat_tiles, ic % ctx_tiles), 0)


def _gdn_pre_kernel(xq_ref, xk_ref, xv_ref, pq_ref, pk_ref, pv_ref, nq_ref, nk_ref, nv_ref, gab_ref, cw_ref,
                    al_ref, dtb_ref, hs_ref, q_ref, k_ref, v_ref, gate_ref, *, n_lat_tiles, lat_tiles, ctx_tiles):
    i = pl.program_id(0)
    R = xq_ref.shape[0]
    pos = jnp.where(i < n_lat_tiles, i % lat_tiles, (i - n_lat_tiles) % ctx_tiles)
    last = jnp.where(i < n_lat_tiles, lat_tiles - 1, ctx_tiles - 1)
    rows = lax.broadcasted_iota(jnp.int32, (R, 1), 0)
    f32 = lambda t: t.astype(jnp.float32)

    def conv_silu(x_ref, p_ref, n_ref, part):
        x = f32(x_ref[...])
        before = jnp.where(pos == 0, 0.0, f32(p_ref[7:8, :]))
        after = jnp.where(pos == last, 0.0, f32(n_ref[0:1, :]))
        x_prev = jnp.where(rows == 0, before, pltpu.roll(x, 1, axis=0))
        x_next = jnp.where(rows == R - 1, after, pltpu.roll(x, R - 1, axis=0))
        cw = cw_ref[:, part * MIX_W:(part + 1) * MIX_W]
        u = x_prev * cw[0:1] + x * cw[1:2] + x_next * cw[2:3]
        return u * jax.nn.sigmoid(u)

    def head_sumsq(t):
        sq = t * t
        hi = sq.astype(jnp.bfloat16)
        lo = (sq - f32(hi)).astype(jnp.bfloat16)
        return (jnp.dot(hi, hs_ref[...], preferred_element_type=jnp.float32)
                + jnp.dot(lo, hs_ref[...], preferred_element_type=jnp.float32))

    q = conv_silu(xq_ref, pq_ref, nq_ref, 0)
    k = conv_silu(xk_ref, pk_ref, nk_ref, 1)
    q_ref[0] = q * lax.rsqrt(head_sumsq(q) + RMS_EPS) * HEAD_DIM ** -0.5
    k_ref[0] = k * lax.rsqrt(head_sumsq(k) + RMS_EPS)
    v_ref[0] = conv_silu(xv_ref, pv_ref, nv_ref, 2)
    gab = gab_ref[...]
    t = gab + dtb_ref[...]
    softplus = jnp.maximum(t, 0.0) + jnp.log1p(jnp.exp(-jnp.abs(t)))
    lane = lax.broadcasted_iota(jnp.int32, (1, LANE), 1)
    gate_ref[0] = jnp.where(lane < 2 * GDN_H, -jnp.exp(al_ref[...]) * softplus,
                            jnp.where(lane < 4 * GDN_H, jax.nn.sigmoid(gab), 0.0))


def gdn_pre_scan_inputs(proj_main, gab, conv_w, a_log, dt_bias, B, n, m):
    T = proj_main.shape[0]
    R = MERGE_TILE
    lat_tiles, ctx_tiles = n // R, m // R
    maps = dict(n_lat_tiles=B * lat_tiles, lat_tiles=lat_tiles, ctx_tiles=ctx_tiles)
    c0 = MAIN_OFF[2] // MIX_W
    halo = R // 8
    cur = [pl.BlockSpec((R, MIX_W), functools.partial(lambda i, c: (i, c), c=c0 + p)) for p in range(3)]
    prev = [pl.BlockSpec((8, MIX_W), functools.partial(lambda i, c: (jnp.maximum(i * halo - 1, 0), c), c=c0 + p))
            for p in range(3)]
    nxt = [pl.BlockSpec((8, MIX_W), functools.partial(lambda i, c: (jnp.minimum((i + 1) * halo, T // 8 - 1), c),
                                                      c=c0 + p)) for p in range(3)]
    const2 = lambda i: (0, 0)
    out_rows = lambda i: _stream_to_scan_rows(i, **maps)
    pad16 = lambda t: jnp.pad(t.reshape(1, 2 * GDN_H), ((0, 0), (0, LANE - 2 * GDN_H)))
    head_sum = (jnp.arange(MIX_W)[:, None] // HEAD_DIM == jnp.arange(MIX_W)[None, :] // HEAD_DIM).astype(jnp.bfloat16)
    tok = jax.ShapeDtypeStruct((B, m + n, MIX_W), jnp.float32)
    return pl.pallas_call(
        functools.partial(_gdn_pre_kernel, **maps),
        grid=(T // R,),
        in_specs=cur + prev + nxt + [pl.BlockSpec((R, LANE), lambda i: (i, 0)),
                                     pl.BlockSpec((GDN_CONV, 3 * MIX_W), const2),
                                     pl.BlockSpec((1, LANE), const2), pl.BlockSpec((1, LANE), const2),
                                     pl.BlockSpec((MIX_W, MIX_W), const2)],
        out_specs=[pl.BlockSpec((1, R, MIX_W), out_rows)] * 3 + [pl.BlockSpec((1, R, LANE), out_rows)],
        out_shape=[tok, tok, tok, jax.ShapeDtypeStruct((B, m + n, LANE), jnp.float32)],
        compiler_params=pltpu.CompilerParams(dimension_semantics=("arbitrary",), vmem_limit_bytes=VMEM_LIMIT),
        name="gdn_pre",
    )(*([proj_main] * 9), gab, conv_w.T, pad16(a_log), pad16(dt_bias), head_sum)


def gdn_branch(proj_main, gab, conv_w, a_log, dt_bias, B, n, m):
    q, k, v, gates = gdn_pre_scan_inputs(proj_main, gab, conv_w, a_log, dt_bias, B, n, m)
    return gdn_scan(q, k, v, gates, m // GDN_CHUNK)


FFT_R = 128
FFT_COLS = 4096
FFT_K1_STEP = 2


def _dft_tables(L):
    R = FFT_R
    N = R * R
    half = L // R
    idx = jnp.arange(R, dtype=jnp.int32)
    ang1 = (-2.0 * math.pi / R) * ((idx[:, None] * idx[None, :]) % R).astype(jnp.float32)
    fr, fi = jnp.cos(ang1), jnp.sin(ang1)
    blk = lambda re, im: jnp.concatenate([jnp.concatenate([re, -im], axis=1),
                                          jnp.concatenate([im, re], axis=1)], axis=0)
    m1 = blk(fr[:, :half], fi[:, :half])
    m1_real = jnp.concatenate([fr, fi], axis=0)
    m3 = blk(fr.T[:half], -fi.T[:half]) * (1.0 / N)
    k = idx[:, None, None] + R * idx[None, :, None]
    ang2 = (-2.0 * math.pi / N) * ((idx[None, None, :] * k) % N).astype(jnp.float32)
    gr, gi = jnp.cos(ang2), jnp.sin(ang2)
    g_fwd = jnp.concatenate([jnp.concatenate([gr, -gi], axis=2),
                             jnp.concatenate([gi, gr], axis=2)], axis=1)
    g_inv = jnp.swapaxes(g_fwd, 1, 2)
    bf = lambda t: t.astype(jnp.bfloat16)
    return bf(m1), bf(m1_real), bf(m3), bf(g_fwd), bf(g_inv)


def _colmm_kernel(m_ref, x_ref, o_ref):
    o_ref[...] = jnp.dot(m_ref[...], x_ref[...].astype(jnp.bfloat16),
                         preferred_element_type=jnp.float32).astype(o_ref.dtype)


def colmm(mat, x, out_dtype=jnp.bfloat16):
    M, K = mat.shape
    n_cols = x.shape[1]
    return pl.pallas_call(
        _colmm_kernel,
        grid=(n_cols // FFT_COLS,),
        in_specs=[pl.BlockSpec((M, K), lambda j: (0, 0)), pl.BlockSpec((K, FFT_COLS), lambda j: (0, j))],
        out_specs=pl.BlockSpec((M, FFT_COLS), lambda j: (0, j)),
        out_shape=jax.ShapeDtypeStruct((M, n_cols), out_dtype),
        compiler_params=pltpu.CompilerParams(dimension_semantics=("arbitrary",), vmem_limit_bytes=VMEM_LIMIT),
        name="hyena_colmm",
    )(mat, x)


def _colmm_gate_kernel(m_ref, r_ref, z_ref, x_ref, skip_ref, o_ref):
    y = jnp.dot(m_ref[...], r_ref[...], preferred_element_type=jnp.float32)
    o_ref[...] = x_ref[...] * (y + z_ref[...] * skip_ref[...])


def colmm_gate(mat, r, z, gate, skip_cols):
    M, K = mat.shape
    n_cols = r.shape[1]
    col = lambda j: (0, j)
    return pl.pallas_call(
        _colmm_gate_kernel,
        grid=(n_cols // FFT_COLS,),
        in_specs=[pl.BlockSpec((M, K), lambda j: (0, 0)), pl.BlockSpec((K, FFT_COLS), col),
                  pl.BlockSpec((M, FFT_COLS), col), pl.BlockSpec((M, FFT_COLS), col),
                  pl.BlockSpec((1, FFT_COLS), col)],
        out_specs=pl.BlockSpec((M, FFT_COLS), col),
        out_shape=jax.ShapeDtypeStruct((M, n_cols), jnp.float32),
        compiler_params=pltpu.CompilerParams(dimension_semantics=("arbitrary",), vmem_limit_bytes=VMEM_LIMIT),
        name="hyena_colmm_gate",
    )(mat, r, z, gate, skip_cols)


def _spectrum_kernel(p_ref, g_ref, o_ref):
    R = FFT_R
    for j in range(FFT_K1_STEP):
        p = jnp.concatenate([p_ref[0, j], p_ref[1, j]], axis=0)
        q = jnp.dot(g_ref[j], p, preferred_element_type=jnp.float32)
        o_ref[0, j] = q[:R]
        o_ref[1, j] = q[R:]


def _freq_kernel(p_ref, g_ref, gi_ref, h_ref, o_ref):
    R = FFT_R
    for j in range(FFT_K1_STEP):
        p = jnp.concatenate([p_ref[0, j], p_ref[1, j]], axis=0)
        q = jnp.dot(g_ref[j], p, preferred_element_type=jnp.float32)
        qr, qi = q[:R], q[R:]
        hr, hi = h_ref[0, j], h_ref[1, j]
        y = jnp.concatenate([qr * hr - qi * hi, qr * hi + qi * hr], axis=0).astype(jnp.bfloat16)
        r = jnp.dot(gi_ref[j], y, preferred_element_type=jnp.float32).astype(o_ref.dtype)
        o_ref[0, j] = r[:R]
        o_ref[1, j] = r[R:]


def hyena_spectrum(p, g_fwd):
    C = p.shape[-1]
    R, S = FFT_R, FFT_K1_STEP
    blk = pl.BlockSpec((2, S, R, LANE * 4), lambda i, c: (0, i, 0, c))
    return pl.pallas_call(
        _spectrum_kernel,
        grid=(R // S, C // (LANE * 4)),
        in_specs=[blk, pl.BlockSpec((S, 2 * R, 2 * R), lambda i, c: (i, 0, 0))],
        out_specs=blk,
        out_shape=jax.ShapeDtypeStruct((2, R, R, C), jnp.float32),
        compiler_params=pltpu.CompilerParams(dimension_semantics=("arbitrary",) * 2, vmem_limit_bytes=VMEM_LIMIT),
        name="hyena_spectrum",
    )(p, g_fwd)


def hyena_freq(p, g_fwd, g_inv, spec, order):
    C = p.shape[-1]
    R, S = FFT_R, FFT_K1_STEP
    blk = pl.BlockSpec((2, S, R, C), lambda i: (0, i, 0, 0))
    gspec = pl.BlockSpec((S, 2 * R, 2 * R), lambda i: (i, 0, 0))
    return pl.pallas_call(
        _freq_kernel,
        grid=(R // S,),
        in_specs=[blk, gspec, gspec, pl.BlockSpec((2, S, R, C), lambda i: (0, i, 0, order))],
        out_specs=blk,
        out_shape=jax.ShapeDtypeStruct((2, R, R, C), jnp.bfloat16),
        compiler_params=pltpu.CompilerParams(dimension_semantics=("arbitrary",), vmem_limit_bytes=VMEM_LIMIT),
        name="hyena_freq",
    )(p, g_fwd, g_inv, spec)


def hyena_taps(L, w1, b1, w2, b2, w3, freq, deltas):
    f32 = jnp.float32
    pos = jnp.arange(L, dtype=f32)
    pos_b = L - pos
    bands = jnp.linspace(1e-4, HY_BANDS - 1, HY_BANDS, dtype=f32)[None, :]

    def mlp(j, w3_dir, delta_dir):
        t = (j / (L - 1))[:, None]
        ang = (2.0 * math.pi / L) * j[:, None]
        feats = jnp.concatenate([t, jnp.cos(bands * ang), -jnp.sin(bands * ang)], axis=-1)
        h = jnp.sin(freq * (feats @ w1 + b1))
        h = jnp.sin(freq * (h @ w2 + b2))
        h = h @ w3_dir
        return h * (jnp.exp(-t * jnp.abs(delta_dir).reshape(1, -1)) + HY_MOD_SHIFT)

    w3d = w3.reshape(w3.shape[0], 2, HY_ORDER * HY_W)
    h_f = mlp(pos, w3d[:, 0], deltas[0])
    h_b = mlp(pos_b, w3d[:, 1], deltas[1])
    h_b = jnp.where((pos > 0)[:, None], h_b, 0.0)
    taps = jnp.concatenate([h_f, h_b], axis=0)
    return taps / jnp.sum(jnp.abs(taps), axis=0, keepdims=True)


def hyena_latent(proj, conv_w, skip, filt, tables):
    B, L, _ = proj.shape
    assert B == 2 and L % FFT_R == 0 and 2 * L == FFT_R * FFT_R
    R, C = FFT_R, HY_W
    m1, m1_real, m3, g_fwd, g_inv = tables
    taps = hyena_taps(L, *filt)
    spec = hyena_spectrum(colmm(m1_real, taps.reshape(R, R * HY_ORDER * C)).reshape(2, R, R, HY_ORDER * C), g_fwd)
    u = centred_dwconv(proj, conv_w)
    v, x1, x2 = jnp.split(u, 3, axis=-1)
    rows = B * (L // R)
    z = v.reshape(rows, R * C)
    for o, gate in enumerate((x1, x2)):
        p = colmm(m1, z).reshape(2, R, R, C)
        r = hyena_freq(p, g_fwd, g_inv, spec, o).reshape(2 * R, R * C)
        z = colmm_gate(m3, r, z, gate.reshape(rows, R * C), jnp.tile(skip[o], R).reshape(1, R * C))
    return z.reshape(B, L, C)


def _combine_ln_kernel(dest_ref, x_ref, w_ref, mod_ref, lng_ref, lnb_ref, yb_ref, o_ref, buf_ref, sem, *, alpha):
    R = x_ref.shape[0]

    def row_copy(k, r):
        row = dest_ref[0, 0, k * R + r]
        return pltpu.make_async_copy(yb_ref.at[pl.ds(row, 1)], buf_ref.at[k, pl.ds(r, 1)], sem.at[k])

    def issue(r, carry):
        row_copy(0, r).start()
        row_copy(1, r).start()
        return carry

    def drain(r, carry):
        row_copy(0, r).wait()
        row_copy(1, r).wait()
        return carry

    lax.fori_loop(0, R, issue, 0)
    lax.fori_loop(0, R, drain, 0)
    y = w_ref[:, 0:1] * buf_ref[0] + w_ref[:, 1:2] * buf_ref[1]
    z = alpha * x_ref[...] + mod_ref[0, 5:6, :] * y
    o_ref[...] = _layer_norm_rows(z) * lng_ref[...] + lnb_ref[...]


def moe_combine_ln(x, yb, dest, wts, mod, ln_g, ln_b, alpha):
    T, D = x.shape
    R = MERGE_TILE
    dest_tiles = jnp.swapaxes(dest.reshape(T // R, R, TOP_K), 1, 2).reshape(T // R, 1, TOP_K * R)
    row = lambda i: (i, 0)
    return pl.pallas_call(
        functools.partial(_combine_ln_kernel, alpha=alpha),
        grid=(T // R,),
        in_specs=[pl.BlockSpec((1, 1, TOP_K * R), lambda i: (i, 0, 0), memory_space=pltpu.SMEM),
                  pl.BlockSpec((R, D), row),
                  pl.BlockSpec((R, TOP_K), row),
                  pl.BlockSpec((1, 6, D), lambda i: (i, 0, 0)),
                  pl.BlockSpec((1, D), lambda i: (0, 0)),
                  pl.BlockSpec((1, D), lambda i: (0, 0)),
                  pl.BlockSpec(memory_space=pl.ANY)],
        out_specs=pl.BlockSpec((R, D), row),
        out_shape=jax.ShapeDtypeStruct((T, D), jnp.float32),
        scratch_shapes=[pltpu.VMEM((TOP_K, R, D), jnp.float32), pltpu.SemaphoreType.DMA((TOP_K,))],
        compiler_params=pltpu.CompilerParams(dimension_semantics=("arbitrary",), vmem_limit_bytes=VMEM_LIMIT),
        name="moe_combine_ln",
    )(dest_tiles, x, wts.T, mod, ln_g.reshape(1, D), ln_b.reshape(1, D), yb)


def moe_dispatch(idx, n_tok):
    n_slot = n_tok * TOP_K
    e_flat = idx.T.reshape(-1)
    onehot = (e_flat[:, None] == jnp.arange(N_EXPERTS, dtype=jnp.int32)[None, :]).astype(jnp.int32)
    csum = jnp.cumsum(onehot, axis=0)
    rank = jnp.sum(csum * onehot, axis=1) - 1
    counts = csum[-1]
    padded = (counts + EXPERT_BLOCK - 1) // EXPERT_BLOCK * EXPERT_BLOCK
    pend = jnp.cumsum(padded)
    pstart = pend - padded
    dest = jnp.sum(onehot * pstart[None, :], axis=1) + rank
    n_blk = -(-n_slot // EXPERT_BLOCK) + N_EXPERTS
    cap = n_blk * EXPERT_BLOCK
    t_flat = jnp.arange(n_slot, dtype=jnp.int32) // TOP_K
    buf_tok = jnp.zeros((cap,), jnp.int32).at[dest].set(t_flat)
    blk_exp = jnp.minimum(jnp.searchsorted(pend, jnp.arange(n_blk, dtype=jnp.int32) * EXPERT_BLOCK, side='right'),
                          N_EXPERTS - 1).astype(jnp.int32)
    n_used = (pend[-1] // EXPERT_BLOCK).astype(jnp.int32).reshape(1)
    return buf_tok, dest, blk_exp, n_used


def layer_norm(x, g=None, b=None):
    mu = jnp.mean(x, -1, keepdims=True)
    var = jnp.mean(jnp.square(x - mu), -1, keepdims=True)
    y = (x - mu) * lax.rsqrt(var + LN_EPS)
    if g is not None:
        y = y * g + b
    return y


def rms_norm(x, w):
    return x * lax.rsqrt(jnp.mean(jnp.square(x), -1, keepdims=True) + RMS_EPS) * w


def l2_normalize(x):
    return x * lax.rsqrt(jnp.sum(jnp.square(x), -1, keepdims=True) + RMS_EPS)


def modulate(x, shift, scale):
    return layer_norm(x) * (1.0 + scale) + shift


def centred_dwconv(u, w):
    K = w.shape[-1]
    T = u.shape[1]
    up = jnp.pad(u, ((0, 0), (K // 2, K // 2), (0, 0)))
    out = up[:, 0:T, :] * w[:, 0]
    for j in range(1, K):
        out = out + up[:, j:j + T, :] * w[:, j]
    return out


def hyena_filters(L, w1, b1, w2, b2, w3, freq, deltas):
    f32 = jnp.float32
    t = jnp.linspace(0.0, 1.0, L, dtype=f32)[:, None]
    ang = 2.0 * math.pi * jnp.arange(L, dtype=f32)[:, None] / L
    bands = jnp.linspace(1e-4, HY_BANDS - 1, HY_BANDS, dtype=f32)[None, :]
    feats = jnp.concatenate([t, jnp.cos(bands * ang), -jnp.sin(bands * ang)], axis=-1)
    h = jnp.sin(freq * (feats @ w1 + b1))
    h = jnp.sin(freq * (h @ w2 + b2))
    h = (h @ w3).reshape(L, 2, HY_ORDER, HY_W)
    window = jnp.exp(-t[:, :, None, None] * jnp.abs(deltas)) + HY_MOD_SHIFT
    h = h * window
    h_fwd, h_bwd = h[:, 0], h[:, 1]
    taps = jnp.concatenate([h_fwd, jnp.zeros_like(h_fwd[:1]), jnp.flip(h_bwd[1:], 0)], axis=0)
    taps = taps / jnp.sum(jnp.abs(taps), axis=0, keepdims=True)
    return jnp.fft.rfft(taps, axis=0)


def hyena_mix(proj, conv_w, skip, filt_f):
    L = proj.shape[1]
    u = centred_dwconv(proj, conv_w)
    v, x1, x2 = jnp.split(u, 3, axis=-1)
    z = v
    for o, gate in enumerate((x1, x2)):
        zf = jnp.fft.rfft(z, n=2 * L, axis=1)
        y = jnp.fft.irfft(zf * filt_f[:, o], n=2 * L, axis=1)[:, :L]
        z = gate * (y + z * skip[o])
    return z


def gated_delta_chunks(q, k, v, g, beta, s0):
    B, T, H, _ = q.shape
    C = GDN_CHUNK
    N = T // C

    def chunks(a):
        return jnp.moveaxis(a.reshape(B, N, C, H, *a.shape[3:]), 3, 1)

    q, k, v, g, beta = chunks(q), chunks(k), chunks(v), chunks(g), chunks(beta)
    gc = jnp.cumsum(g, axis=-1)
    tri = jnp.tril(jnp.ones((C, C), bool))
    strict = jnp.tril(jnp.ones((C, C), bool), -1)
    decay = jnp.exp(jnp.where(tri, gc[..., :, None] - gc[..., None, :], -jnp.inf))
    kb = k * beta[..., None]
    A = jnp.where(strict, jnp.einsum('bhnid,bhnjd->bhnij', kb, k) * decay, 0.0)
    eye = jnp.eye(C, dtype=jnp.float32)
    Tm = lax.linalg.triangular_solve(eye + A, jnp.broadcast_to(eye, A.shape),
                                     left_side=True, lower=True, unit_diagonal=True)
    u_val = jnp.einsum('bhnij,bhnjd->bhnid', Tm, v * beta[..., None])
    w_val = jnp.einsum('bhnij,bhnjd->bhnid', Tm, kb * jnp.exp(gc)[..., None])
    intra = jnp.einsum('bhnid,bhnjd->bhnij', q, k) * decay
    q_dec = q * jnp.exp(gc)[..., None]
    g_last = gc[..., -1]
    k_dec = k * jnp.exp(g_last[..., None] - gc)[..., None]

    def step(S, xs):
        qd, w, u, att, kd, gl = xs
        v_new = u - jnp.einsum('bhcd,bhde->bhce', w, S)
        o = jnp.einsum('bhcd,bhde->bhce', qd, S) + jnp.einsum('bhij,bhje->bhie', att, v_new)
        S = S * jnp.exp(gl)[..., None, None] + jnp.einsum('bhcd,bhce->bhde', kd, v_new)
        return S, o

    xs = tuple(jnp.moveaxis(a, 2, 0) for a in (q_dec, w_val, u_val, intra, k_dec, g_last))
    s_fin, o = lax.scan(step, s0, xs)
    o = jnp.transpose(o, (1, 0, 3, 2, 4)).reshape(B, T, H, -1)
    return o, s_fin


def gdn_mix(qkv, z, a, b, conv_w, a_log, dt_bias, norm_w, init_states, with_output):
    B, T, _ = qkv.shape
    u = jax.nn.silu(centred_dwconv(qkv, conv_w))
    q, k, v = [t.reshape(B, T, GDN_H, HEAD_DIM) for t in jnp.split(u, 3, axis=-1)]
    q = l2_normalize(q) * HEAD_DIM ** -0.5
    k = l2_normalize(k)
    g = -jnp.exp(a_log) * jax.nn.softplus(a.reshape(B, T, 2, GDN_H) + dt_bias)
    beta = jax.nn.sigmoid(b.reshape(B, T, 2, GDN_H))
    outs, finals = [], []
    for d in range(2):
        rev = (lambda t: jnp.flip(t, 1)) if d == 1 else (lambda t: t)
        o, s_fin = gated_delta_chunks(rev(q), rev(k), rev(v), rev(g[:, :, d]), rev(beta[:, :, d]), init_states[d])
        finals.append(s_fin)
        if with_output:
            outs.append(rev(o))
    if not with_output:
        return None, (finals[0], finals[1])
    o = rms_norm(outs[0] + outs[1], norm_w) * jax.nn.silu(z.reshape(B, T, GDN_H, HEAD_DIM))
    return o.reshape(B, T, GDN_H * HEAD_DIM), (finals[0], finals[1])


def axial_rope_angles(n):
    t = jnp.arange(n)
    row = (t // GRID_W).astype(jnp.float32)
    col = (t % GRID_W).astype(jnp.float32)
    nf = HEAD_DIM // 4
    inv = ROPE_THETA ** (-jnp.arange(nf, dtype=jnp.float32) / nf)
    return row[:, None] * inv, col[:, None] * inv


def rope_half(x, ang):
    x1, x2 = jnp.split(x, 2, axis=-1)
    cos, sin = jnp.cos(ang), jnp.sin(ang)
    return jnp.concatenate([x1 * cos - x2 * sin, x1 * sin + x2 * cos], axis=-1)


def axial_rope(x, ang_r, ang_c):
    half = HEAD_DIM // 2
    ar = ang_r[None, :, None, None, :]
    ac = ang_c[None, :, None, None, :]
    return jnp.concatenate([rope_half(x[..., :half], ar), rope_half(x[..., half:], ac)], axis=-1)


def diff_attend(q, k, v, lam):
    s = jnp.einsum('bqhcd,bkhcd->bhcqk', q, k) * HEAD_DIM ** -0.5
    p = jax.nn.softmax(s, axis=-1)
    a = p[:, :, 0] - lam * p[:, :, 1]
    return jnp.einsum('bhqk,bkhd->bqhd', a, v)


def diff_latent(q, k, v, kc, vc, lam, ang_r, ang_c):
    B, n = q.shape[:2]
    q = axial_rope(q, ang_r, ang_c)
    k_all = jnp.concatenate([axial_rope(k, ang_r, ang_c), kc], axis=1)
    v_all = jnp.concatenate([v, vc], axis=1)
    nb = n // Q_BLOCK
    qb = jnp.swapaxes(q.reshape(B, nb, Q_BLOCK, *q.shape[2:]), 0, 1)
    o = lax.map(lambda qi: diff_attend(qi, k_all, v_all, lam), qb)
    return jnp.swapaxes(o, 0, 1).reshape(B, n, DIFF_H, DIFF_VD)


def diff_finish(o, norm_w, lam_init):
    B, T = o.shape[:2]
    return (rms_norm(o, norm_w) * (1.0 - lam_init)).reshape(B, T, DIFF_H * DIFF_VD)


def split_diff(p):
    B, T, _ = p.shape
    q, k, v = jnp.split(p, 3, axis=-1)
    return (q.reshape(B, T, DIFF_H, 2, HEAD_DIM), k.reshape(B, T, DIFF_H, 2, HEAD_DIM),
            v.reshape(B, T, DIFF_H, DIFF_VD))


def dense_attend(q, k, v):
    s = jnp.einsum('bqhd,bkhd->bhqk', q, k) * HEAD_DIM ** -0.5
    p = jax.nn.softmax(s, axis=-1)
    return jnp.einsum('bhqk,bkhd->bqhd', p, v)


def na_latent(q, k, v, kc, vc, rpb):
    B, n, H, dh = q.shape
    R = n // GRID_W
    kh, kw = min(NA_ROWS, R), NA_COLS
    grid = lambda t: t.reshape(B, R, GRID_W, H, dh)
    kg, vg = grid(k), grid(v)
    cols = jnp.arange(GRID_W)
    cstart = jnp.clip(cols - kw // 2, 0, GRID_W - kw)
    col_ok = (cols[None, :] >= cstart[:, None]) & (cols[None, :] < cstart[:, None] + kw)
    mask = jnp.broadcast_to(col_ok[:, None, :], (GRID_W, kh, GRID_W)).reshape(GRID_W, kh * GRID_W)
    dc = jnp.clip(cols[None, :] - cols[:, None], -(kw - 1), kw - 1) + (kw - 1)
    rpb_c = rpb[:, :, dc]
    scale = dh ** -0.5

    def row(args):
        r, qr = args
        rs = jnp.clip(r - kh // 2, 0, R - kh)
        kr = lax.dynamic_slice_in_dim(kg, rs, kh, axis=1).reshape(B, kh * GRID_W, H, dh)
        vr = lax.dynamic_slice_in_dim(vg, rs, kh, axis=1).reshape(B, kh * GRID_W, H, dh)
        dr = rs + jnp.arange(kh) - r + (NA_ROWS - 1)
        bias = jnp.transpose(rpb_c[:, dr], (0, 2, 1, 3)).reshape(H, GRID_W, kh * GRID_W)
        s_win = jnp.einsum('bqhd,bkhd->bhqk', qr, kr) * scale + bias[None]
        s_win = jnp.where(mask, s_win, -jnp.inf)
        s_ctx = jnp.einsum('bqhd,bkhd->bhqk', qr, kc) * scale
        p = jax.nn.softmax(jnp.concatenate([s_win, s_ctx], axis=-1), axis=-1)
        nw = kh * GRID_W
        return (jnp.einsum('bhqk,bkhd->bqhd', p[..., :nw], vr)
                + jnp.einsum('bhqk,bkhd->bqhd', p[..., nw:], vc))

    o = lax.map(row, (jnp.arange(R), jnp.swapaxes(grid(q), 0, 1)))
    return jnp.swapaxes(o, 0, 1).reshape(B, n, H * dh)


def merge_branches(ys, gate_cols, proj, w_o):
    br = jnp.einsum('btmc,mcd->btmd', jnp.stack(ys, axis=2), proj)
    g = jax.nn.sigmoid(gate_cols.reshape(gate_cols.shape[0], gate_cols.shape[1], N_BRANCH, -1))
    return jnp.sum(g * br, axis=2) @ w_o


def moe_ffn(u, router_w, router_b, w1, w3, w2):
    n_tok, d = u.shape
    s = jax.nn.sigmoid(u @ router_w)
    sel = s + router_b
    per = N_EXPERTS // N_GROUPS
    group_score = lax.top_k(sel.reshape(n_tok, N_GROUPS, per), TOP_K)[0].sum(-1)
    best_group = jnp.argmax(group_score, axis=-1)
    in_group = (jnp.arange(N_EXPERTS) // per)[None, :] == best_group[:, None]
    _, idx = lax.top_k(jnp.where(in_group, sel, -jnp.inf), TOP_K)
    wts = jnp.take_along_axis(s, idx, axis=-1)
    wts = wts / jnp.sum(wts, -1, keepdims=True)
    n_slot = n_tok * TOP_K
    e_flat = idx.reshape(-1).astype(jnp.int32)
    t_flat = jnp.repeat(jnp.arange(n_tok, dtype=jnp.int32), TOP_K)
    w_flat = wts.reshape(-1)
    order = jnp.argsort(e_flat)
    e_s, t_s, w_s = e_flat[order], t_flat[order], w_flat[order]
    counts = jax.ops.segment_sum(jnp.ones_like(e_flat), e_flat, num_segments=N_EXPERTS)
    padded = (counts + MOE_BLOCK - 1) // MOE_BLOCK * MOE_BLOCK
    start = jnp.cumsum(counts) - counts
    pend = jnp.cumsum(padded)
    pstart = pend - padded
    dest = pstart[e_s] + jnp.arange(n_slot, dtype=jnp.int32) - start[e_s]
    n_blk = -(-n_slot // MOE_BLOCK) + N_EXPERTS
    cap = n_blk * MOE_BLOCK
    buf_tok = jnp.full((cap,), n_tok, jnp.int32).at[dest].set(t_s)
    buf_w = jnp.zeros((cap,), jnp.float32).at[dest].set(w_s)
    blk_exp = jnp.minimum(jnp.searchsorted(pend, jnp.arange(n_blk, dtype=jnp.int32) * MOE_BLOCK, side='right'),
                          N_EXPERTS - 1)
    u_pad = jnp.concatenate([u, jnp.zeros((1, d), u.dtype)], axis=0)
    xb = u_pad[buf_tok].reshape(n_blk, MOE_BLOCK, d)

    def expert_block(args):
        xi, e = args
        return (jax.nn.silu(xi @ w1[e]) * (xi @ w3[e])) @ w2[e]

    yb = lax.map(expert_block, (xb, blk_exp)).reshape(cap, d)
    y = jax.ops.segment_sum(yb * buf_w[:, None], buf_tok, num_segments=n_tok + 1)
    return y[:n_tok]


def _diff_pre_kernel(xq_ref, xk_ref, xv_ref, cos_ref, sin_ref, q_ref, k_ref, v_ref, *, n_lat_tiles, q_scale):
    is_lat = pl.program_id(0) < n_lat_tiles
    cos = jnp.tile(jnp.where(is_lat, cos_ref[...], 1.0), (1, DIFF_H))
    sin = jnp.tile(jnp.where(is_lat, sin_ref[...], 0.0), (1, DIFF_H))
    lane = lax.broadcasted_iota(jnp.int32, (1, MIX_W), 1)
    first_half = (lane % (HEAD_DIM // 2)) < HEAD_DIM // 4

    def rope(x):
        x = x.astype(jnp.float32)
        partner = jnp.where(first_half, pltpu.roll(x, MIX_W - HEAD_DIM // 4, axis=1),
                            pltpu.roll(x, HEAD_DIM // 4, axis=1))
        return x * cos + partner * sin

    q_ref[...] = (rope(xq_ref[...]) * q_scale).astype(q_ref.dtype)
    k_ref[0] = rope(xk_ref[...]).astype(k_ref.dtype)
    v_ref[0] = xv_ref[...]


def diff_pre(proj_main, cos_t, sin_t, B, n, m):
    T = proj_main.shape[0]
    R = MERGE_TILE
    lat_tiles, ctx_tiles = n // R, m // R
    n_lat_tiles = B * lat_tiles
    c0 = MAIN_OFF[4] // MIX_W
    cur = [pl.BlockSpec((R, MIX_W), functools.partial(lambda i, c: (i, c), c=c0 + p)) for p in range(3)]
    table = pl.BlockSpec((R, DIFF_VD), lambda i: (jnp.where(i < n_lat_tiles, i % lat_tiles, 0), 0))

    def key_rows(i):
        ic = i - n_lat_tiles
        return (jnp.where(i < n_lat_tiles, i // lat_tiles, ic // ctx_tiles),
                jnp.where(i < n_lat_tiles, i % lat_tiles, lat_tiles + ic % ctx_tiles), 0)

    kv = jax.ShapeDtypeStruct((B, n + m, MIX_W), jnp.bfloat16)
    return pl.pallas_call(
        functools.partial(_diff_pre_kernel, n_lat_tiles=n_lat_tiles,
                          q_scale=HEAD_DIM ** -0.5 * math.log2(math.e)),
        grid=(T // R,),
        in_specs=cur + [table, table],
        out_specs=[pl.BlockSpec((R, MIX_W), lambda i: (i, 0)),
                   pl.BlockSpec((1, R, MIX_W), key_rows), pl.BlockSpec((1, R, MIX_W), key_rows)],
        out_shape=[jax.ShapeDtypeStruct((T, MIX_W), jnp.bfloat16), kv, kv],
        compiler_params=pltpu.CompilerParams(dimension_semantics=("arbitrary",), vmem_limit_bytes=VMEM_LIMIT),
        name="diff_pre",
    )(proj_main, proj_main, proj_main, cos_t, sin_t)


def rope_tables(ang_r, ang_c):
    cos = jnp.concatenate([jnp.cos(ang_r)] * 2 + [jnp.cos(ang_c)] * 2, axis=-1)
    sin = jnp.concatenate([-jnp.sin(ang_r), jnp.sin(ang_r), -jnp.sin(ang_c), jnp.sin(ang_c)], axis=-1)
    return jnp.tile(cos, (1, 2)), jnp.tile(sin, (1, 2))


def diff_branch(proj_main, cos_t, sin_t, lam, norm_w, lam_init, B, n, m):
    q, k_all, v_all = diff_pre(proj_main, cos_t, sin_t, B, n, m)
    nk = n + m
    k_all = jnp.transpose(k_all.reshape(B, nk, DIFF_H, 2, HEAD_DIM), (0, 2, 3, 1, 4))
    vt_all = jnp.transpose(v_all.reshape(B, nk // DIFF_TK, DIFF_TK, DIFF_H, DIFF_VD), (0, 3, 1, 4, 2))
    return diff_attention(q[:B * n].reshape(B, n, MIX_W), k_all, vt_all, lam, norm_w, 1.0 - lam_init)


def _split_w_in(w):
    segs = jnp.split(w, SPLIT_IDX, axis=-1)
    ab = jnp.concatenate([segs[3], segs[4]], axis=-1)
    ab = jnp.pad(ab, ((0, 0), (0, LANE - ab.shape[-1])))
    main = jnp.concatenate([segs[7], segs[0], segs[1], segs[2], segs[5], segs[6]], axis=-1)
    return main.astype(jnp.bfloat16), ab.astype(jnp.bfloat16)


MAIN_SIZES = (N_BRANCH * D_MODEL, 3 * HY_W, 3 * MIX_W, MIX_W, 3 * MIX_W, 3 * MIX_W)
MAIN_OFF = tuple(sum(MAIN_SIZES[:i]) for i in range(len(MAIN_SIZES) + 1))
IN_TN = 1792


def kernel(x, c, ctx, c_ctx, w_mod, b_mod, w_in, hy_conv, hy_w1, hy_b1, hy_w2, hy_b2, hy_w3, hy_freq,
           hy_deltas, hy_skip, gdn_conv, gdn_a_log, gdn_dt_bias, gdn_norm, diff_lam, diff_norm, na_rpb,
           branch_proj, w_out, ln_g, ln_b, router_w, router_b, exp_w1, exp_w3, exp_w2):
    B, n, D = x.shape
    m = ctx.shape[1]
    depth = w_mod.shape[0]
    dn_alpha = (2 * depth) ** 0.25
    ang_r, ang_c = axial_rope_angles(n)
    zero_state = jnp.zeros((B, GDN_H, HEAD_DIM, HEAD_DIM), jnp.float32)
    bf16 = jnp.bfloat16
    n_lat = B * n
    dft_tables = _dft_tables(n)
    cos_t, sin_t = rope_tables(ang_r, ang_c)
    tok = jnp.concatenate([x.reshape(n_lat, D), ctx.reshape(B * m, D)], axis=0)
    for l in range(depth):
        ctx_out = l < depth - 1
        mx = jax.nn.silu(c) @ w_mod[l] + b_mod[l]
        mc = jax.nn.silu(c_ctx) @ w_mod[l] + b_mod[l]
        one_plus = jnp.array([0.0, 1.0, 0.0, 0.0, 1.0, 0.0], jnp.float32)[:, None]
        mod_x = mx.reshape(B, 6, D) + one_plus
        mod_c = mc.reshape(1, 6, D) + one_plus
        mod = jnp.concatenate([jnp.repeat(mod_x, n // MERGE_TILE, axis=0),
                               jnp.repeat(mod_c, B * m // MERGE_TILE, axis=0)], axis=0)

        proj, gab = ln_mod_matmul(tok, mod, *_split_w_in(w_in[l]), IN_TN)

        def seg(i, latent):
            rows = slice(0, n_lat) if latent else slice(n_lat, None)
            return proj[rows, MAIN_OFF[i]:MAIN_OFF[i + 1]].reshape(B, -1, MAIN_SIZES[i]).astype(jnp.float32)

        hy_x, gqkv_x, gz_x, dqkv_x, nqkv_x = [seg(i, True) for i in range(1, 6)]
        hy_c, gqkv_c, gz_c, dqkv_c, nqkv_c = [seg(i, False) for i in range(1, 6)]
        gab_x, gab_c = gab[:n_lat].reshape(B, n, LANE), gab[n_lat:].reshape(B, m, LANE)
        ga_x, gb_x = gab_x[..., :2 * GDN_H], gab_x[..., 2 * GDN_H:4 * GDN_H]
        ga_c, gb_c = gab_c[..., :2 * GDN_H], gab_c[..., 2 * GDN_H:4 * GDN_H]
        filt = (hy_w1[l], hy_b1[l], hy_w2[l], hy_b2[l], hy_w3[l], hy_freq[l], hy_deltas[l])

        ya_x = hyena_latent(hy_x, hy_conv[l], hy_skip[l], filt, dft_tables)

        o_f, o_b = gdn_branch(proj, gab, gdn_conv[l], gdn_a_log[l], gdn_dt_bias[l], B, n, m)

        lq1, lk1, lq2, lk2 = diff_lam[l]
        lam_init = 0.8 - 0.6 * math.exp(-0.3 * l)
        lam = jnp.exp(jnp.sum(lq1 * lk1)) - jnp.exp(jnp.sum(lq2 * lk2)) + lam_init
        yc_x = diff_branch(proj, cos_t, sin_t, lam, diff_norm[l], lam_init, B, n, m)
        yd_x = na_attention(proj, na_bias_tables(na_rpb[l], n // GRID_W), B, n, m)

        ys = [t.reshape(n_lat, MIX_W) for t in (ya_x, yc_x, yd_x)]
        if ctx_out:
            dq_c, dk_c, dv_c = split_diff(dqkv_c)
            nq_c, nk_c, nv_c = [t.reshape(B, m, NA_H, HEAD_DIM) for t in jnp.split(nqkv_c, 3, axis=-1)]
            ya_c = hyena_mix(hy_c, hy_conv[l], hy_skip[l], hyena_filters(m, *filt))
            yc_c = diff_finish(diff_attend(dq_c, dk_c, dv_c, lam), diff_norm[l], lam_init)
            yd_c = dense_attend(nq_c, nk_c, nv_c)
            ys_c = [t.reshape(B * m, MIX_W) for t in (ya_c, yc_c, yd_c)]
            ys = [jnp.concatenate([a, b], axis=0) for a, b in zip(ys, ys_c)]
        else:
            tok = tok[:n_lat]
        tok = merge_residual_ln(tok, ys[0], o_f, o_b, ys[1], ys[2], proj, mod, branch_proj[l].astype(bf16),
                                w_out[l].astype(bf16), gdn_norm[l], ln_g[l, 0], ln_b[l, 0], dn_alpha, n, m)

        n_tok = tok.shape[0]
        h2, idx, wts = moe_route(tok, mod, router_w, router_b)
        buf_tok, dest, blk_exp, n_used = moe_dispatch(idx, n_tok)
        yb = expert_ffn(blk_exp, n_used, jnp.take(h2, buf_tok, axis=0), exp_w1[l], exp_w3[l], exp_w2[l])
        tok = moe_combine_ln(tok, yb, dest.reshape(n_tok, TOP_K), wts, mod, ln_g[l, 1], ln_b[l, 1], dn_alpha)
    return tok[:n_lat].reshape(B, n, D)
```

```python
import functools
import math

import jax
import jax.numpy as jnp
from jax import lax
from jax.experimental import pallas as pl
from jax.experimental.pallas import tpu as pltpu

D_MODEL = 1024
GRID_W = 64
HEAD_DIM = 64
MIX_W = D_MODEL // 2
N_BRANCH = 4
HY_W = MIX_W
HY_ORDER = 2
HY_BANDS = 16
HY_MOD_SHIFT = 0.05
GDN_H = MIX_W // HEAD_DIM
GDN_CHUNK = 64
GDN_CONV = 3
DIFF_VD = 2 * HEAD_DIM
DIFF_H = MIX_W // DIFF_VD
NA_H = MIX_W // HEAD_DIM
NA_ROWS = 8
NA_COLS = 16
N_EXPERTS = 16
N_GROUPS = 4
TOP_K = 2
MOE_BLOCK = 128
Q_BLOCK = 128
ROPE_THETA = 10000.0
LN_EPS = 1e-5
RMS_EPS = 1e-6
SPLIT_SIZES = (3 * HY_W, 3 * MIX_W, MIX_W, 2 * GDN_H, 2 * GDN_H, 3 * MIX_W, 3 * MIX_W, N_BRANCH * D_MODEL)
SPLIT_IDX = tuple(sum(SPLIT_SIZES[:i + 1]) for i in range(len(SPLIT_SIZES) - 1))

LANE = 128
ROW_TILE = 512
MERGE_TILE = 256
VMEM_LIMIT = 48 * 1024 * 1024


def _ln_mod_matmul_kernel(x_ref, mod_ref, w_ref, wg_ref, o_ref, og_ref, h_ref):
    @pl.when(pl.program_id(1) == 0)
    def _():
        x = x_ref[...]
        mu = jnp.mean(x, axis=-1, keepdims=True)
        xc = x - mu
        var = jnp.mean(xc * xc, axis=-1, keepdims=True)
        y = xc * lax.rsqrt(var + LN_EPS)
        h = (y * mod_ref[0, 1:2, :] + mod_ref[0, 0:1, :]).astype(h_ref.dtype)
        h_ref[...] = h
        og_ref[...] = jnp.dot(h, wg_ref[...], preferred_element_type=jnp.float32)

    o_ref[...] = jnp.dot(h_ref[...], w_ref[...], preferred_element_type=jnp.float32).astype(o_ref.dtype)


def ln_mod_matmul(x, mod, w, w_gates, tn):
    T, D = x.shape
    N = w.shape[1]
    return pl.pallas_call(
        _ln_mod_matmul_kernel,
        grid=(T // ROW_TILE, N // tn),
        in_specs=[
            pl.BlockSpec((ROW_TILE, D), lambda i, j: (i, 0)),
            pl.BlockSpec((1, 6, D), lambda i, j: (i * (ROW_TILE // MERGE_TILE), 0, 0)),
            pl.BlockSpec((D, tn), lambda i, j: (0, j)),
            pl.BlockSpec((D, LANE), lambda i, j: (0, 0)),
        ],
        out_specs=[pl.BlockSpec((ROW_TILE, tn), lambda i, j: (i, j)),
                   pl.BlockSpec((ROW_TILE, LANE), lambda i, j: (i, 0))],
        out_shape=[jax.ShapeDtypeStruct((T, N), jnp.bfloat16), jax.ShapeDtypeStruct((T, LANE), jnp.float32)],
        scratch_shapes=[pltpu.VMEM((ROW_TILE, D), jnp.bfloat16)],
        compiler_params=pltpu.CompilerParams(
            dimension_semantics=("arbitrary", "arbitrary"), vmem_limit_bytes=VMEM_LIMIT),
        name="ln_mod_matmul",
    )(x, mod, w, w_gates)


def _layer_norm_rows(x):
    mu = jnp.mean(x, axis=-1, keepdims=True)
    xc = x - mu
    var = jnp.mean(xc * xc, axis=-1, keepdims=True)
    return xc * lax.rsqrt(var + LN_EPS)


_NT_DIMS = (((1,), (1,)), ((), ()))


DIFF_TQ = 512
DIFF_TK = 768
DIFF_ROWS = 32


def _diff_attn_kernel(lam_ref, q_ref, k_ref, vt_ref, nw_ref, o_ref, s_ref, *, out_scale):
    tq = q_ref.shape[1]
    n_tiles = k_ref.shape[3] // DIFF_TK
    q = q_ref[0]
    qc = (q[:, :HEAD_DIM], q[:, HEAD_DIM:])

    sub = 8
    fold = lambda t, op: op(t.reshape(t.shape[0] // sub, sub, t.shape[1]), axis=0)

    def scores(j, slot):
        start = pl.multiple_of(j * DIFF_TK, DIFF_TK)
        mx = []
        for c in range(2):
            k = k_ref[0, 0, c, pl.ds(start, DIFF_TK), :]
            st = lax.dot_general(k, qc[c], _NT_DIMS, preferred_element_type=jnp.float32)
            s_ref[slot, c] = st
            mx.append(fold(st, jnp.max))
        return tuple(mx)

    def softmax_pv(j, slot, mx, carry):
        new, ps = [], []
        for c in range(2):
            m_prev, l_prev, acc = carry[c]
            m_new = jnp.maximum(m_prev, jnp.max(mx[c], axis=0, keepdims=True))
            alpha = jnp.exp2(m_prev - m_new)
            psum, chunks = None, []
            for r in range(DIFF_TK // DIFF_ROWS):
                p = jnp.exp2(s_ref[slot, c, r * DIFF_ROWS:(r + 1) * DIFF_ROWS, :] - m_new)
                part = fold(p, jnp.sum)
                psum = part if psum is None else psum + part
                chunks.append(p.astype(jnp.bfloat16))
            new.append((m_new, alpha * l_prev + jnp.sum(psum, axis=0, keepdims=True), alpha * acc))
            ps.append(jnp.concatenate(chunks, axis=0))
        pv = jnp.dot(vt_ref[0, 0, j], jnp.concatenate(ps, axis=1), preferred_element_type=jnp.float32)
        return tuple((new[c][0], new[c][1], new[c][2] + pv[:, c * tq:(c + 1) * tq]) for c in range(2))

    def pair(jj, state):
        carry, mx0 = state
        j0 = 2 * jj
        mx1 = scores(j0 + 1, 1)
        carry = softmax_pv(j0, 0, mx0, carry)
        mx0 = scores(j0 + 2, 0)
        return softmax_pv(j0 + 1, 1, mx1, carry), mx0

    carry = tuple((jnp.full((1, tq), -jnp.inf, jnp.float32), jnp.zeros((1, tq), jnp.float32),
                   jnp.zeros((DIFF_VD, tq), jnp.float32)) for _ in range(2))
    mx0 = scores(0, 0)
    n_pairs = (n_tiles - 1) // 2
    carry, mx0 = lax.fori_loop(0, n_pairs, pair, (carry, mx0))
    if n_tiles % 2 == 0:
        mx1 = scores(n_tiles - 1, 1)
        carry = softmax_pv(n_tiles - 2, 0, mx0, carry)
        carry = softmax_pv(n_tiles - 1, 1, mx1, carry)
    else:
        carry = softmax_pv(n_tiles - 1, 0, mx0, carry)
    (_, l0, a0), (_, l1, a1) = carry
    o = a0 / l0 - lam_ref[0] * (a1 / l1)
    o = o * lax.rsqrt(jnp.mean(o * o, axis=0, keepdims=True) + RMS_EPS) * (nw_ref[...] * out_scale)
    o_ref[0] = o.T.astype(o_ref.dtype)


def diff_attention(q, k, vt, lam, norm_w, out_scale, out_dtype=jnp.float32):
    B, nq, W = q.shape
    nk = k.shape[3]
    assert nq % DIFF_TQ == 0 and nk % DIFF_TK == 0
    H = W // DIFF_VD
    return pl.pallas_call(
        functools.partial(_diff_attn_kernel, out_scale=out_scale),
        grid=(B, H, nq // DIFF_TQ),
        in_specs=[
            pl.BlockSpec(memory_space=pltpu.SMEM),
            pl.BlockSpec((1, DIFF_TQ, DIFF_VD), lambda b, h, i: (b, i, h)),
            pl.BlockSpec((1, 1, 2, nk, HEAD_DIM), lambda b, h, i: (b, h, 0, 0, 0)),
            pl.BlockSpec((1, 1, nk // DIFF_TK, DIFF_VD, DIFF_TK), lambda b, h, i: (b, h, 0, 0, 0)),
            pl.BlockSpec((DIFF_VD, 1), lambda b, h, i: (0, 0)),
        ],
        out_specs=pl.BlockSpec((1, DIFF_TQ, DIFF_VD), lambda b, h, i: (b, i, h)),
        out_shape=jax.ShapeDtypeStruct((B, nq, W), out_dtype),
        scratch_shapes=[pltpu.VMEM((2, 2, DIFF_TK, DIFF_TQ), jnp.float32)],
        compiler_params=pltpu.CompilerParams(
            dimension_semantics=("arbitrary",) * 3, vmem_limit_bytes=VMEM_LIMIT),
        name="diff_attention",
    )(lam.reshape(1).astype(jnp.float32), q, k, vt, norm_w.reshape(DIFF_VD, 1))


NA_QROWS = 8
NA_KROWS = 16
NA_NEG = -1e30


def _na_kernel(q_ref, k_ref, v_ref, kc_ref, vc_ref, tab_ref, o_ref, *, n_kblocks):
    g = pl.program_id(2)
    kb = (NA_KROWS // 4) * GRID_W
    base = jnp.clip(2 * g - 1, 0, n_kblocks - 4)
    start = pl.multiple_of(base * kb, kb)
    nwin = NA_KROWS * GRID_W
    q = q_ref[...] * HEAD_DIM ** -0.5
    kw = k_ref[pl.ds(start, nwin), :]
    vw = v_ref[pl.ds(start, nwin), :]
    kc = kc_ref[...]
    vc = vc_ref[...]
    outs = []
    for hh in range(2):
        sl = slice(hh * HEAD_DIM, (hh + 1) * HEAD_DIM)
        qh = q[:, sl]
        s = lax.dot_general(qh, kw[:, sl], _NT_DIMS, preferred_element_type=jnp.float32) + tab_ref[0, hh].astype(jnp.float32)
        sc = lax.dot_general(qh, kc[:, sl], _NT_DIMS, preferred_element_type=jnp.float32)
        m = jnp.maximum(jnp.max(s, axis=-1, keepdims=True), jnp.max(sc, axis=-1, keepdims=True))
        p = jnp.exp(s - m)
        pc = jnp.exp(sc - m)
        l = jnp.sum(p, axis=-1, keepdims=True) + jnp.sum(pc, axis=-1, keepdims=True)
        o = (jnp.dot(p.astype(vw.dtype), vw[:, sl], preferred_element_type=jnp.float32)
             + jnp.dot(pc.astype(vc.dtype), vc[:, sl], preferred_element_type=jnp.float32))
        outs.append(o / l)
    o_ref[...] = jnp.concatenate(outs, axis=-1).astype(o_ref.dtype)


def na_bias_tables(rpb, R):
    H = rpb.shape[0]
    G = R // NA_QROWS
    cols = jnp.arange(GRID_W)
    cstart = jnp.clip(cols - NA_COLS // 2, 0, GRID_W - NA_COLS)
    col_ok = (cols[None, :] >= cstart[:, None]) & (cols[None, :] < cstart[:, None] + NA_COLS)
    dc = jnp.clip(cols[None, :] - cols[:, None], -(NA_COLS - 1), NA_COLS - 1) + (NA_COLS - 1)
    rpb_c = rpb[:, :, dc]
    tabs = []
    for g in (0, 1, G - 1):
        base = min(max(2 * g - 1, 0), R // 4 - 4)
        r = NA_QROWS * g + jnp.arange(NA_QROWS)
        rs = jnp.clip(r - NA_ROWS // 2, 0, R - NA_ROWS)
        kr = 4 * base + jnp.arange(NA_KROWS)
        valid = (kr[None, :] >= rs[:, None]) & (kr[None, :] < rs[:, None] + NA_ROWS)
        dr = jnp.clip(kr[None, :] - r[:, None] + (NA_ROWS - 1), 0, 2 * NA_ROWS - 2)
        tab = rpb_c[:, dr]
        ok = valid[None, :, :, None, None] & col_ok[None, None, None, :, :]
        tab = jnp.where(ok, tab, NA_NEG)
        tabs.append(jnp.transpose(tab, (0, 1, 3, 2, 4)).reshape(H, NA_QROWS * GRID_W, NA_KROWS * GRID_W))
    return jnp.stack(tabs, axis=0).astype(jnp.bfloat16)


def na_attention(proj_main, tabs, B, n, m, out_dtype=jnp.float32):
    W = MIX_W
    tq = NA_QROWS * GRID_W
    G = n // tq
    n_kblocks = n // ((NA_KROWS // 4) * GRID_W)
    cq, ck, cv = [(MAIN_OFF[5] + p * W) // LANE for p in range(3)]
    ctx0 = B * n // m

    def tab_index(b, hp, g):
        return (jnp.where(g == 0, 0, jnp.where(g == G - 1, 2, 1)), hp, 0, 0)

    return pl.pallas_call(
        functools.partial(_na_kernel, n_kblocks=n_kblocks),
        grid=(B, W // LANE, G),
        in_specs=[
            pl.BlockSpec((tq, LANE), lambda b, hp, g: (b * G + g, cq + hp)),
            pl.BlockSpec((n, LANE), lambda b, hp, g: (b, ck + hp)),
            pl.BlockSpec((n, LANE), lambda b, hp, g: (b, cv + hp)),
            pl.BlockSpec((m, LANE), lambda b, hp, g: (ctx0 + b, ck + hp)),
            pl.BlockSpec((m, LANE), lambda b, hp, g: (ctx0 + b, cv + hp)),
            pl.BlockSpec((1, 2, tq, NA_KROWS * GRID_W), tab_index),
        ],
        out_specs=pl.BlockSpec((tq, LANE), lambda b, hp, g: (b * G + g, hp)),
        out_shape=jax.ShapeDtypeStruct((B * n, W), out_dtype),
        compiler_params=pltpu.CompilerParams(
            dimension_semantics=("arbitrary",) * 3, vmem_limit_bytes=VMEM_LIMIT),
        name="na_attention",
    )(proj_main, proj_main, proj_main, proj_main, proj_main, tabs)


def _merge_kernel(x_ref, ya_ref, of_ref, ob_ref, yc_ref, yd_ref, gc_ref, gz_ref, mod_ref, p_ref, wo_ref,
                  gn_ref, hs_ref, lng_ref, lnb_ref, o_ref, *, alpha):
    o = of_ref[0] + ob_ref[0]
    sq = o * o
    hi = sq.astype(jnp.bfloat16)
    lo = (sq - hi.astype(jnp.float32)).astype(jnp.bfloat16)
    ms = (jnp.dot(hi, hs_ref[...], preferred_element_type=jnp.float32)
          + jnp.dot(lo, hs_ref[...], preferred_element_type=jnp.float32)) * (1.0 / HEAD_DIM)
    gz = gz_ref[...].astype(jnp.float32)
    yb = o * lax.rsqrt(ms + RMS_EPS) * gn_ref[...] * (gz * jax.nn.sigmoid(gz))
    acc = None
    for mi, y in enumerate((ya_ref[...], yb, yc_ref[...], yd_ref[...])):
        br = jnp.dot(y.astype(jnp.bfloat16), p_ref[mi], preferred_element_type=jnp.float32)
        gm = jax.nn.sigmoid(gc_ref[:, mi * D_MODEL:(mi + 1) * D_MODEL].astype(jnp.float32))
        acc = gm * br if acc is None else acc + gm * br
    mix = jnp.dot(acc.astype(jnp.bfloat16), wo_ref[...], preferred_element_type=jnp.float32)
    z = alpha * x_ref[...] + mod_ref[0, 2:3, :] * mix
    o_ref[...] = _layer_norm_rows(z) * lng_ref[...] + lnb_ref[...]


def merge_residual_ln(x, ya, o_f, o_b, yc, yd, proj_main, mod, proj, w_o, gdn_norm, ln_g, ln_b, alpha, n, m):
    T, D = x.shape
    R = MERGE_TILE
    lat_tiles, ctx_tiles = n // R, m // R
    scan_rows = functools.partial(_stream_to_scan_rows, n_lat_tiles=o_f.shape[0] * lat_tiles,
                                  lat_tiles=lat_tiles, ctx_tiles=ctx_tiles)
    row = lambda i: (i, 0)
    const2 = lambda i: (0, 0)
    branch = pl.BlockSpec((R, MIX_W), row)
    scan = pl.BlockSpec((1, R, MIX_W), scan_rows)
    head_sum = (jnp.arange(MIX_W)[:, None] // HEAD_DIM == jnp.arange(MIX_W)[None, :] // HEAD_DIM).astype(jnp.bfloat16)
    return pl.pallas_call(
        functools.partial(_merge_kernel, alpha=alpha),
        grid=(T // R,),
        in_specs=[pl.BlockSpec((R, D), row), branch, scan, scan, branch, branch,
                  pl.BlockSpec((R, N_BRANCH * D), row),
                  pl.BlockSpec((R, MIX_W), lambda i: (i, MAIN_OFF[3] // MIX_W)),
                  pl.BlockSpec((1, 6, D), lambda i: (i, 0, 0)),
                  pl.BlockSpec((N_BRANCH, MIX_W, D), lambda i: (0, 0, 0)),
                  pl.BlockSpec((D, D), const2),
                  pl.BlockSpec((1, MIX_W), const2),
                  pl.BlockSpec((MIX_W, MIX_W), const2),
                  pl.BlockSpec((1, D), const2),
                  pl.BlockSpec((1, D), const2)],
        out_specs=pl.BlockSpec((R, D), row),
        out_shape=jax.ShapeDtypeStruct((T, D), jnp.float32),
        compiler_params=pltpu.CompilerParams(dimension_semantics=("arbitrary",), vmem_limit_bytes=VMEM_LIMIT),
        name="merge_residual_ln",
    )(x, ya, o_f, o_b, yc, yd, proj_main, proj_main, mod, proj, w_o,
      jnp.tile(gdn_norm, GDN_H).reshape(1, MIX_W), head_sum, ln_g.reshape(1, D), ln_b.reshape(1, D))


def _route_kernel(x_ref, mod_ref, rw_ref, rb_ref, h_ref, idx_ref, wts_ref):
    h = (_layer_norm_rows(x_ref[...]) * mod_ref[0, 4:5, :] + mod_ref[0, 3:4, :]).astype(jnp.bfloat16)
    h_ref[...] = h.astype(h_ref.dtype)
    logits = lax.dot_general(rw_ref[...], h, _NT_DIMS, preferred_element_type=jnp.float32)
    s = jax.nn.sigmoid(logits)
    sel = s + rb_ref[...]
    per = N_EXPERTS // N_GROUPS
    srow = [s[e:e + 1, :] for e in range(N_EXPERTS)]
    vrow = [sel[e:e + 1, :] for e in range(N_EXPERTS)]
    best = None
    for gi in range(N_GROUPS):
        grp = vrow[gi * per:(gi + 1) * per]
        gs = None
        for a in range(per):
            for b in range(a + 1, per):
                ps = grp[a] + grp[b]
                gs = ps if gs is None else jnp.maximum(gs, ps)
        if best is None:
            best, bg = gs, jnp.zeros(gs.shape, jnp.int32)
        else:
            upd = gs > best
            bg = jnp.where(upd, gi, bg)
            best = jnp.where(upd, gs, best)
    cv, cs = [], []
    for j in range(per):
        v_j, s_j = vrow[j], srow[j]
        for gi in range(1, N_GROUPS):
            v_j = jnp.where(bg == gi, vrow[gi * per + j], v_j)
            s_j = jnp.where(bg == gi, srow[gi * per + j], s_j)
        cv.append(v_j)
        cs.append(s_j)

    def first_argmax(vals):
        bv, bi = vals[0], jnp.zeros(vals[0].shape, jnp.int32)
        for j in range(1, per):
            upd = vals[j] > bv
            bi = jnp.where(upd, j, bi)
            bv = jnp.where(upd, vals[j], bv)
        return bi

    i1 = first_argmax(cv)
    i2 = first_argmax([jnp.where(i1 == j, -jnp.inf, cv[j]) for j in range(per)])
    w1 = cs[0]
    w2 = cs[0]
    for j in range(1, per):
        w1 = jnp.where(i1 == j, cs[j], w1)
        w2 = jnp.where(i2 == j, cs[j], w2)
    tot = w1 + w2
    idx_ref[...] = jnp.concatenate([bg * per + i1, bg * per + i2], axis=0)
    wts_ref[...] = jnp.concatenate([w1 / tot, w2 / tot], axis=0)


def moe_route(x, mod, router_w, router_b):
    T, D = x.shape
    return pl.pallas_call(
        _route_kernel,
        grid=(T // MERGE_TILE,),
        in_specs=[pl.BlockSpec((MERGE_TILE, D), lambda i: (i, 0)),
                  pl.BlockSpec((1, 6, D), lambda i: (i, 0, 0)),
                  pl.BlockSpec((N_EXPERTS, D), lambda i: (0, 0)),
                  pl.BlockSpec((N_EXPERTS, 1), lambda i: (0, 0))],
        out_specs=[pl.BlockSpec((MERGE_TILE, D), lambda i: (i, 0)),
                   pl.BlockSpec((TOP_K, MERGE_TILE), lambda i: (0, i)),
                   pl.BlockSpec((TOP_K, MERGE_TILE), lambda i: (0, i))],
        out_shape=[jax.ShapeDtypeStruct((T, D), jnp.float32),
                   jax.ShapeDtypeStruct((TOP_K, T), jnp.int32),
                   jax.ShapeDtypeStruct((TOP_K, T), jnp.float32)],
        compiler_params=pltpu.CompilerParams(dimension_semantics=("arbitrary",), vmem_limit_bytes=VMEM_LIMIT),
        name="moe_route",
    )(x, mod, router_w.T.astype(jnp.bfloat16), router_b.reshape(N_EXPERTS, 1).astype(jnp.float32))


EXPERT_BLOCK = 256


def _expert_kernel(be_ref, nu_ref, x_ref, w1_ref, w3_ref, w2_ref, o_ref):
    i = pl.program_id(0)

    @pl.when(i < nu_ref[0])
    def _():
        x = x_ref[...].astype(jnp.bfloat16)
        bf = lambda ref: ref[0].astype(jnp.bfloat16)
        a = jnp.dot(x, bf(w1_ref), preferred_element_type=jnp.float32)
        b = jnp.dot(x, bf(w3_ref), preferred_element_type=jnp.float32)
        hmid = (a * jax.nn.sigmoid(a) * b).astype(jnp.bfloat16)
        o_ref[...] = jnp.dot(hmid, bf(w2_ref), preferred_element_type=jnp.float32).astype(o_ref.dtype)

    @pl.when(i >= nu_ref[0])
    def _():
        o_ref[...] = jnp.zeros(o_ref.shape, o_ref.dtype)


def expert_ffn(blk_exp, n_used, xb, w1, w3, w2, out_dtype=jnp.float32):
    cap, D = xb.shape
    F = w1.shape[-1]
    n_blk = cap // EXPERT_BLOCK
    return pl.pallas_call(
        _expert_kernel,
        grid_spec=pltpu.PrefetchScalarGridSpec(
            num_scalar_prefetch=2,
            grid=(n_blk,),
            in_specs=[pl.BlockSpec((EXPERT_BLOCK, D), lambda i, be, nu: (i, 0)),
                      pl.BlockSpec((1, D, F), lambda i, be, nu: (be[i], 0, 0)),
                      pl.BlockSpec((1, D, F), lambda i, be, nu: (be[i], 0, 0)),
                      pl.BlockSpec((1, F, D), lambda i, be, nu: (be[i], 0, 0))],
            out_specs=pl.BlockSpec((EXPERT_BLOCK, D), lambda i, be, nu: (i, 0)),
        ),
        out_shape=jax.ShapeDtypeStruct((cap, D), out_dtype),
        compiler_params=pltpu.CompilerParams(dimension_semantics=("arbitrary",), vmem_limit_bytes=VMEM_LIMIT),
        name="expert_ffn",
    )(blk_exp, n_used, xb, w1, w3, w2)


def _gdn_kernel(qf_ref, kf_ref, vf_ref, gf_ref, qb_ref, kb_ref, vb_ref, gb_ref, of_ref, ob_ref, s_ref):
    C = GDN_CHUNK
    nb = qf_ref.shape[0]

    @pl.when(pl.program_id(0) == 0)
    def _():
        s_ref[...] = jnp.zeros(s_ref.shape, jnp.float32)

    bf = lambda t: t.astype(jnp.bfloat16)
    mm = lambda a, b: jnp.dot(bf(a), bf(b), preferred_element_type=jnp.float32)
    nt = lambda a, b: lax.dot_general(bf(a), bf(b), _NT_DIMS, preferred_element_type=jnp.float32)
    tn = lambda a, b: lax.dot_general(bf(a), bf(b), (((0,), (0,)), ((), ())), preferred_element_type=jnp.float32)
    col = lambda t, h: t[:, h:h + 1]
    rel = lax.broadcasted_iota(jnp.int32, (C, C), 0) - lax.broadcasted_iota(jnp.int32, (C, C), 1)

    qs, ks, vs, decay, beta_c, e_gc, e_rest, e_tot, strict = [], [], [], [], [], [], [], [], []
    for bi in range(nb):
        for d, refs in enumerate(((qf_ref, kf_ref, vf_ref, gf_ref), (qb_ref, kb_ref, vb_ref, gb_ref))):
            q_ref, k_ref, v_ref, g_ref = refs
            incl = rel >= 0 if d == 0 else rel <= 0
            g = g_ref[bi]
            gc = jnp.dot(incl.astype(jnp.float32), g, precision=lax.Precision.HIGHEST,
                         preferred_element_type=jnp.float32)
            gc_t = gc.T
            tot = jnp.sum(g, axis=0, keepdims=True)
            eg, er, et = jnp.exp(gc), jnp.exp(tot - gc), jnp.exp(tot)
            q, k, v = q_ref[bi], k_ref[bi], v_ref[bi]
            for h in range(GDN_H):
                sl = slice(h * HEAD_DIM, (h + 1) * HEAD_DIM)
                gl = d * GDN_H + h
                qs.append(q[:, sl])
                ks.append(k[:, sl])
                vs.append(v[:, sl])
                decay.append(jnp.where(incl, jnp.exp(col(gc, gl) - gc_t[gl:gl + 1, :]), 0.0))
                strict.append(rel > 0 if d == 0 else rel < 0)
                beta_c.append(col(g, 2 * GDN_H + gl))
                e_gc.append(col(eg, gl))
                e_rest.append(col(er, gl))
                e_tot.append(col(et, gl))
    chains = range(len(qs))
    kb = [ks[c] * beta_c[c] for c in chains]
    a_mat = [jnp.where(strict[c], nt(kb[c], ks[c]) * decay[c], 0.0) for c in chains]
    intra = [nt(qs[c], ks[c]) * decay[c] for c in chains]
    ri = lax.broadcasted_iota(jnp.int32, (C, C), 0)
    ci = lax.broadcasted_iota(jnp.int32, (C, C), 1)

    def level_mask(s, upper):
        sh = s.bit_length() - 1
        same = (ri >> (sh + 1)) == (ci >> (sh + 1))
        r_hi = ((ri >> sh) & 1) == 1
        c_hi = ((ci >> sh) & 1) == 1
        return same & c_hi & ~r_hi if upper else same & r_hi & ~c_hi

    masks = {s: (level_mask(s, False), level_mask(s, True)) for s in (1, 2, 4, 8, 16, 32)}
    upper = [(c // GDN_H) % 2 == 1 for c in chains]
    eye = (ri == ci).astype(jnp.float32)
    t_inv = [eye - jnp.where(masks[1][upper[c]], a_mat[c], 0.0) for c in chains]
    for s in (2, 4, 8, 16, 32):
        tm = [mm(t_inv[c], jnp.where(masks[s][upper[c]], a_mat[c], 0.0)) for c in chains]
        t_inv = [t_inv[c] - mm(tm[c], t_inv[c]) for c in chains]
    xs = [mm(t_inv[c], jnp.concatenate([vs[c] * beta_c[c], kb[c] * e_gc[c]], axis=1)) for c in chains]
    st = [s_ref[c] for c in chains]
    v_new = [xs[c][:, :HEAD_DIM] - mm(xs[c][:, HEAD_DIM:], st[c]) for c in chains]
    o_st = [mm(qs[c] * e_gc[c], st[c]) for c in chains]
    o_in = [mm(intra[c], v_new[c]) for c in chains]
    s_up = [tn(ks[c] * e_rest[c], v_new[c]) for c in chains]
    for c in chains:
        s_ref[c] = st[c] * e_tot[c] + s_up[c]
    outs = [o_st[c] + o_in[c] for c in chains]
    for bi in range(nb):
        for d, o_ref in enumerate((of_ref, ob_ref)):
            c0 = (bi * 2 + d) * GDN_H
            o_ref[bi] = jnp.concatenate(outs[c0:c0 + GDN_H], axis=1)


def gdn_scan(q, k, v, gates, n_ctx_chunks):
    B, Tt, W = q.shape
    C = GDN_CHUNK
    NC = Tt // C

    def bwd(s):
        return jnp.where(s < n_ctx_chunks, n_ctx_chunks - 1 - s, NC - 1 - (s - n_ctx_chunks))

    tok_f = pl.BlockSpec((B, C, W), lambda s: (0, s, 0))
    tok_b = pl.BlockSpec((B, C, W), lambda s: (0, bwd(s), 0))
    gate_f = pl.BlockSpec((B, C, LANE), lambda s: (0, s, 0))
    gate_b = pl.BlockSpec((B, C, LANE), lambda s: (0, bwd(s), 0))
    return pl.pallas_call(
        _gdn_kernel,
        grid=(NC,),
        in_specs=[tok_f, tok_f, tok_f, gate_f, tok_b, tok_b, tok_b, gate_b],
        out_specs=[tok_f, tok_b],
        out_shape=[jax.ShapeDtypeStruct((B, Tt, W), jnp.float32)] * 2,
        scratch_shapes=[pltpu.VMEM((B * 2 * GDN_H, HEAD_DIM, HEAD_DIM), jnp.float32)],
        compiler_params=pltpu.CompilerParams(dimension_semantics=("arbitrary",), vmem_limit_bytes=VMEM_LIMIT),
        name="gdn_scan",
    )(q, k, v, gates, q, k, v, gates)


def _stream_to_scan_rows(i, n_lat_tiles, lat_tiles, ctx_tiles):
    ic = i - n_lat_tiles
    return (jnp.where(i < n_lat_tiles, i // lat_tiles, ic // ctx_tiles),
            jnp.where(i < n_lat_tiles, ctx_tiles + i % lat_tiles, ic % ctx_tiles), 0)


def _gdn_pre_kernel(xq_ref, xk_ref, xv_ref, pq_ref, pk_ref, pv_ref, nq_ref, nk_ref, nv_ref, gab_ref, cw_ref,
                    al_ref, dtb_ref, hs_ref, q_ref, k_ref, v_ref, gate_ref, *, n_lat_tiles, lat_tiles, ctx_tiles):
    i = pl.program_id(0)
    R = xq_ref.shape[0]
    pos = jnp.where(i < n_lat_tiles, i % lat_tiles, (i - n_lat_tiles) % ctx_tiles)
    last = jnp.where(i < n_lat_tiles, lat_tiles - 1, ctx_tiles - 1)
    rows = lax.broadcasted_iota(jnp.int32, (R, 1), 0)
    f32 = lambda t: t.astype(jnp.float32)

    def conv_silu(x_ref, p_ref, n_ref, part):
        x = f32(x_ref[...])
        before = jnp.where(pos == 0, 0.0, f32(p_ref[7:8, :]))
        after = jnp.where(pos == last, 0.0, f32(n_ref[0:1, :]))
        x_prev = jnp.where(rows == 0, before, pltpu.roll(x, 1, axis=0))
        x_next = jnp.where(rows == R - 1, after, pltpu.roll(x, R - 1, axis=0))
        cw = cw_ref[:, part * MIX_W:(part + 1) * MIX_W]
        u = x_prev * cw[0:1] + x * cw[1:2] + x_next * cw[2:3]
        return u * jax.nn.sigmoid(u)

    def head_sumsq(t):
        sq = t * t
        hi = sq.astype(jnp.bfloat16)
        lo = (sq - f32(hi)).astype(jnp.bfloat16)
        return (jnp.dot(hi, hs_ref[...], preferred_element_type=jnp.float32)
                + jnp.dot(lo, hs_ref[...], preferred_element_type=jnp.float32))

    q = conv_silu(xq_ref, pq_ref, nq_ref, 0)
    k = conv_silu(xk_ref, pk_ref, nk_ref, 1)
    q_ref[0] = q * lax.rsqrt(head_sumsq(q) + RMS_EPS) * HEAD_DIM ** -0.5
    k_ref[0] = k * lax.rsqrt(head_sumsq(k) + RMS_EPS)
    v_ref[0] = conv_silu(xv_ref, pv_ref, nv_ref, 2)
    gab = gab_ref[...]
    t = gab + dtb_ref[...]
    softplus = jnp.maximum(t, 0.0) + jnp.log1p(jnp.exp(-jnp.abs(t)))
    lane = lax.broadcasted_iota(jnp.int32, (1, LANE), 1)
    gate_ref[0] = jnp.where(lane < 2 * GDN_H, -jnp.exp(al_ref[...]) * softplus,
                            jnp.where(lane < 4 * GDN_H, jax.nn.sigmoid(gab), 0.0))


def gdn_pre_scan_inputs(proj_main, gab, conv_w, a_log, dt_bias, B, n, m):
    T = proj_main.shape[0]
    R = MERGE_TILE
    lat_tiles, ctx_tiles = n // R, m // R
    maps = dict(n_lat_tiles=B * lat_tiles, lat_tiles=lat_tiles, ctx_tiles=ctx_tiles)
    c0 = MAIN_OFF[2] // MIX_W
    halo = R // 8
    cur = [pl.BlockSpec((R, MIX_W), functools.partial(lambda i, c: (i, c), c=c0 + p)) for p in range(3)]
    prev = [pl.BlockSpec((8, MIX_W), functools.partial(lambda i, c: (jnp.maximum(i * halo - 1, 0), c), c=c0 + p))
            for p in range(3)]
    nxt = [pl.BlockSpec((8, MIX_W), functools.partial(lambda i, c: (jnp.minimum((i + 1) * halo, T // 8 - 1), c),
                                                      c=c0 + p)) for p in range(3)]
    const2 = lambda i: (0, 0)
    out_rows = lambda i: _stream_to_scan_rows(i, **maps)
    pad16 = lambda t: jnp.pad(t.reshape(1, 2 * GDN_H), ((0, 0), (0, LANE - 2 * GDN_H)))
    head_sum = (jnp.arange(MIX_W)[:, None] // HEAD_DIM == jnp.arange(MIX_W)[None, :] // HEAD_DIM).astype(jnp.bfloat16)
    tok = jax.ShapeDtypeStruct((B, m + n, MIX_W), jnp.float32)
    return pl.pallas_call(
        functools.partial(_gdn_pre_kernel, **maps),
        grid=(T // R,),
        in_specs=cur + prev + nxt + [pl.BlockSpec((R, LANE), lambda i: (i, 0)),
                                     pl.BlockSpec((GDN_CONV, 3 * MIX_W), const2),
                                     pl.BlockSpec((1, LANE), const2), pl.BlockSpec((1, LANE), const2),
                                     pl.BlockSpec((MIX_W, MIX_W), const2)],
        out_specs=[pl.BlockSpec((1, R, MIX_W), out_rows)] * 3 + [pl.BlockSpec((1, R, LANE), out_rows)],
        out_shape=[tok, tok, tok, jax.ShapeDtypeStruct((B, m + n, LANE), jnp.float32)],
        compiler_params=pltpu.CompilerParams(dimension_semantics=("arbitrary",), vmem_limit_bytes=VMEM_LIMIT),
        name="gdn_pre",
    )(*([proj_main] * 9), gab, conv_w.T, pad16(a_log), pad16(dt_bias), head_sum)


def gdn_branch(proj_main, gab, conv_w, a_log, dt_bias, B, n, m):
    q, k, v, gates = gdn_pre_scan_inputs(proj_main, gab, conv_w, a_log, dt_bias, B, n, m)
    return gdn_scan(q, k, v, gates, m // GDN_CHUNK)


FFT_R = 128
FFT_COLS = 4096
FFT_K1_STEP = 2


def _dft_tables(L):
    R = FFT_R
    N = R * R
    half = L // R
    idx = jnp.arange(R, dtype=jnp.int32)
    ang1 = (-2.0 * math.pi / R) * ((idx[:, None] * idx[None, :]) % R).astype(jnp.float32)
    fr, fi = jnp.cos(ang1), jnp.sin(ang1)
    blk = lambda re, im: jnp.concatenate([jnp.concatenate([re, -im], axis=1),
                                          jnp.concatenate([im, re], axis=1)], axis=0)
    m1 = blk(fr[:, :half], fi[:, :half])
    m1_real = jnp.concatenate([fr, fi], axis=0)
    m3 = blk(fr.T[:half], -fi.T[:half]) * (1.0 / N)
    k = idx[:, None, None] + R * idx[None, :, None]
    ang2 = (-2.0 * math.pi / N) * ((idx[None, None, :] * k) % N).astype(jnp.float32)
    gr, gi = jnp.cos(ang2), jnp.sin(ang2)
    g_fwd = jnp.concatenate([jnp.concatenate([gr, -gi], axis=2),
                             jnp.concatenate([gi, gr], axis=2)], axis=1)
    g_inv = jnp.swapaxes(g_fwd, 1, 2)
    bf = lambda t: t.astype(jnp.bfloat16)
    return bf(m1), bf(m1_real), bf(m3), bf(g_fwd), bf(g_inv)


def _colmm_kernel(m_ref, x_ref, o_ref):
    o_ref[...] = jnp.dot(m_ref[...], x_ref[...].astype(jnp.bfloat16),
                         preferred_element_type=jnp.float32).astype(o_ref.dtype)


def colmm(mat, x, out_dtype=jnp.bfloat16):
    M, K = mat.shape
    n_cols = x.shape[1]
    return pl.pallas_call(
        _colmm_kernel,
        grid=(n_cols // FFT_COLS,),
        in_specs=[pl.BlockSpec((M, K), lambda j: (0, 0)), pl.BlockSpec((K, FFT_COLS), lambda j: (0, j))],
        out_specs=pl.BlockSpec((M, FFT_COLS), lambda j: (0, j)),
        out_shape=jax.ShapeDtypeStruct((M, n_cols), out_dtype),
        compiler_params=pltpu.CompilerParams(dimension_semantics=("arbitrary",), vmem_limit_bytes=VMEM_LIMIT),
        name="hyena_colmm",
    )(mat, x)


def _colmm_gate_kernel(m_ref, r_ref, z_ref, x_ref, skip_ref, o_ref):
    y = jnp.dot(m_ref[...], r_ref[...], preferred_element_type=jnp.float32)
    o_ref[...] = x_ref[...] * (y + z_ref[...] * skip_ref[...])


def colmm_gate(mat, r, z, gate, skip_cols):
    M, K = mat.shape
    n_cols = r.shape[1]
    col = lambda j: (0, j)
    return pl.pallas_call(
        _colmm_gate_kernel,
        grid=(n_cols // FFT_COLS,),
        in_specs=[pl.BlockSpec((M, K), lambda j: (0, 0)), pl.BlockSpec((K, FFT_COLS), col),
                  pl.BlockSpec((M, FFT_COLS), col), pl.BlockSpec((M, FFT_COLS), col),
                  pl.BlockSpec((1, FFT_COLS), col)],
        out_specs=pl.BlockSpec((M, FFT_COLS), col),
        out_shape=jax.ShapeDtypeStruct((M, n_cols), jnp.float32),
        compiler_params=pltpu.CompilerParams(dimension_semantics=("arbitrary",), vmem_limit_bytes=VMEM_LIMIT),
        name="hyena_colmm_gate",
    )(mat, r, z, gate, skip_cols)


def _spectrum_kernel(p_ref, g_ref, o_ref):
    R = FFT_R
    for j in range(FFT_K1_STEP):
        p = jnp.concatenate([p_ref[0, j], p_ref[1, j]], axis=0)
        q = jnp.dot(g_ref[j], p, preferred_element_type=jnp.float32)
        o_ref[0, j] = q[:R]
        o_ref[1, j] = q[R:]


def _freq_kernel(p_ref, g_ref, gi_ref, h_ref, o_ref):
    R = FFT_R
    for j in range(FFT_K1_STEP):
        p = jnp.concatenate([p_ref[0, j], p_ref[1, j]], axis=0)
        q = jnp.dot(g_ref[j], p, preferred_element_type=jnp.float32)
        qr, qi = q[:R], q[R:]
        hr, hi = h_ref[0, j], h_ref[1, j]
        y = jnp.concatenate([qr * hr - qi * hi, qr * hi + qi * hr], axis=0).astype(jnp.bfloat16)
        r = jnp.dot(gi_ref[j], y, preferred_element_type=jnp.float32).astype(o_ref.dtype)
        o_ref[0, j] = r[:R]
        o_ref[1, j] = r[R:]


def hyena_spectrum(p, g_fwd):
    C = p.shape[-1]
    R, S = FFT_R, FFT_K1_STEP
    blk = pl.BlockSpec((2, S, R, LANE * 4), lambda i, c: (0, i, 0, c))
    return pl.pallas_call(
        _spectrum_kernel,
        grid=(R // S, C // (LANE * 4)),
        in_specs=[blk, pl.BlockSpec((S, 2 * R, 2 * R), lambda i, c: (i, 0, 0))],
        out_specs=blk,
        out_shape=jax.ShapeDtypeStruct((2, R, R, C), jnp.float32),
        compiler_params=pltpu.CompilerParams(dimension_semantics=("arbitrary",) * 2, vmem_limit_bytes=VMEM_LIMIT),
        name="hyena_spectrum",
    )(p, g_fwd)


def hyena_freq(p, g_fwd, g_inv, spec, order):
    C = p.shape[-1]
    R, S = FFT_R, FFT_K1_STEP
    blk = pl.BlockSpec((2, S, R, C), lambda i: (0, i, 0, 0))
    gspec = pl.BlockSpec((S, 2 * R, 2 * R), lambda i: (i, 0, 0))
    return pl.pallas_call(
        _freq_kernel,
        grid=(R // S,),
        in_specs=[blk, gspec, gspec, pl.BlockSpec((2, S, R, C), lambda i: (0, i, 0, order))],
        out_specs=blk,
        out_shape=jax.ShapeDtypeStruct((2, R, R, C), jnp.bfloat16),
        compiler_params=pltpu.CompilerParams(dimension_semantics=("arbitrary",), vmem_limit_bytes=VMEM_LIMIT),
        name="hyena_freq",
    )(p, g_fwd, g_inv, spec)


def hyena_taps(L, w1, b1, w2, b2, w3, freq, deltas):
    f32 = jnp.float32
    pos = jnp.arange(L, dtype=f32)
    pos_b = L - pos
    bands = jnp.linspace(1e-4, HY_BANDS - 1, HY_BANDS, dtype=f32)[None, :]

    def mlp(j, w3_dir, delta_dir):
        t = (j / (L - 1))[:, None]
        ang = (2.0 * math.pi / L) * j[:, None]
        feats = jnp.concatenate([t, jnp.cos(bands * ang), -jnp.sin(bands * ang)], axis=-1)
        h = jnp.sin(freq * (feats @ w1 + b1))
        h = jnp.sin(freq * (h @ w2 + b2))
        h = h @ w3_dir
        return h * (jnp.exp(-t * jnp.abs(delta_dir).reshape(1, -1)) + HY_MOD_SHIFT)

    w3d = w3.reshape(w3.shape[0], 2, HY_ORDER * HY_W)
    h_f = mlp(pos, w3d[:, 0], deltas[0])
    h_b = mlp(pos_b, w3d[:, 1], deltas[1])
    h_b = jnp.where((pos > 0)[:, None], h_b, 0.0)
    taps = jnp.concatenate([h_f, h_b], axis=0)
    return taps / jnp.sum(jnp.abs(taps), axis=0, keepdims=True)


def hyena_latent(proj, conv_w, skip, filt, tables):
    B, L, _ = proj.shape
    assert B == 2 and L % FFT_R == 0 and 2 * L == FFT_R * FFT_R
    R, C = FFT_R, HY_W
    m1, m1_real, m3, g_fwd, g_inv = tables
    taps = hyena_taps(L, *filt)
    spec = hyena_spectrum(colmm(m1_real, taps.reshape(R, R * HY_ORDER * C)).reshape(2, R, R, HY_ORDER * C), g_fwd)
    u = centred_dwconv(proj, conv_w)
    v, x1, x2 = jnp.split(u, 3, axis=-1)
    rows = B * (L // R)
    z = v.reshape(rows, R * C)
    for o, gate in enumerate((x1, x2)):
        p = colmm(m1, z).reshape(2, R, R, C)
        r = hyena_freq(p, g_fwd, g_inv, spec, o).reshape(2 * R, R * C)
        z = colmm_gate(m3, r, z, gate.reshape(rows, R * C), jnp.tile(skip[o], R).reshape(1, R * C))
    return z.reshape(B, L, C)


def _combine_ln_kernel(dest_ref, x_ref, w_ref, mod_ref, lng_ref, lnb_ref, yb_ref, o_ref, buf_ref, sem, *, alpha):
    R = x_ref.shape[0]

    def row_copy(k, r):
        row = dest_ref[0, 0, k * R + r]
        return pltpu.make_async_copy(yb_ref.at[pl.ds(row, 1)], buf_ref.at[k, pl.ds(r, 1)], sem.at[k])

    def issue(r, carry):
        row_copy(0, r).start()
        row_copy(1, r).start()
        return carry

    def drain(r, carry):
        row_copy(0, r).wait()
        row_copy(1, r).wait()
        return carry

    lax.fori_loop(0, R, issue, 0, unroll=8)
    lax.fori_loop(0, R, drain, 0, unroll=8)
    y = w_ref[:, 0:1] * buf_ref[0] + w_ref[:, 1:2] * buf_ref[1]
    z = alpha * x_ref[...] + mod_ref[0, 5:6, :] * y
    o_ref[...] = _layer_norm_rows(z) * lng_ref[...] + lnb_ref[...]


def moe_combine_ln(x, yb, dest, wts, mod, ln_g, ln_b, alpha):
    T, D = x.shape
    R = MERGE_TILE
    dest_tiles = jnp.swapaxes(dest.reshape(T // R, R, TOP_K), 1, 2).reshape(T // R, 1, TOP_K * R)
    row = lambda i: (i, 0)
    return pl.pallas_call(
        functools.partial(_combine_ln_kernel, alpha=alpha),
        grid=(T // R,),
        in_specs=[pl.BlockSpec((1, 1, TOP_K * R), lambda i: (i, 0, 0), memory_space=pltpu.SMEM),
                  pl.BlockSpec((R, D), row),
                  pl.BlockSpec((R, TOP_K), row),
                  pl.BlockSpec((1, 6, D), lambda i: (i, 0, 0)),
                  pl.BlockSpec((1, D), lambda i: (0, 0)),
                  pl.BlockSpec((1, D), lambda i: (0, 0)),
                  pl.BlockSpec(memory_space=pl.ANY)],
        out_specs=pl.BlockSpec((R, D), row),
        out_shape=jax.ShapeDtypeStruct((T, D), jnp.float32),
        scratch_shapes=[pltpu.VMEM((TOP_K, R, D), jnp.float32), pltpu.SemaphoreType.DMA((TOP_K,))],
        compiler_params=pltpu.CompilerParams(dimension_semantics=("arbitrary",), vmem_limit_bytes=VMEM_LIMIT),
        name="moe_combine_ln",
    )(dest_tiles, x, wts.T, mod, ln_g.reshape(1, D), ln_b.reshape(1, D), yb)


def moe_dispatch(idx, n_tok):
    n_slot = n_tok * TOP_K
    e_flat = idx.T.reshape(-1)
    onehot = (e_flat[:, None] == jnp.arange(N_EXPERTS, dtype=jnp.int32)[None, :]).astype(jnp.int32)
    csum = jnp.cumsum(onehot, axis=0)
    rank = jnp.sum(csum * onehot, axis=1) - 1
    counts = csum[-1]
    padded = (counts + EXPERT_BLOCK - 1) // EXPERT_BLOCK * EXPERT_BLOCK
    pend = jnp.cumsum(padded)
    pstart = pend - padded
    dest = jnp.sum(onehot * pstart[None, :], axis=1) + rank
    n_blk = -(-n_slot // EXPERT_BLOCK) + N_EXPERTS
    cap = n_blk * EXPERT_BLOCK
    t_flat = jnp.arange(n_slot, dtype=jnp.int32) // TOP_K
    buf_tok = jnp.zeros((cap,), jnp.int32).at[dest].set(t_flat)
    blk_exp = jnp.minimum(jnp.searchsorted(pend, jnp.arange(n_blk, dtype=jnp.int32) * EXPERT_BLOCK, side='right'),
                          N_EXPERTS - 1).astype(jnp.int32)
    n_used = (pend[-1] // EXPERT_BLOCK).astype(jnp.int32).reshape(1)
    return buf_tok, dest, blk_exp, n_used


def layer_norm(x, g=None, b=None):
    mu = jnp.mean(x, -1, keepdims=True)
    var = jnp.mean(jnp.square(x - mu), -1, keepdims=True)
    y = (x - mu) * lax.rsqrt(var + LN_EPS)
    if g is not None:
        y = y * g + b
    return y


def rms_norm(x, w):
    return x * lax.rsqrt(jnp.mean(jnp.square(x), -1, keepdims=True) + RMS_EPS) * w


def l2_normalize(x):
    return x * lax.rsqrt(jnp.sum(jnp.square(x), -1, keepdims=True) + RMS_EPS)


def modulate(x, shift, scale):
    return layer_norm(x) * (1.0 + scale) + shift


def centred_dwconv(u, w):
    K = w.shape[-1]
    T = u.shape[1]
    up = jnp.pad(u, ((0, 0), (K // 2, K // 2), (0, 0)))
    out = up[:, 0:T, :] * w[:, 0]
    for j in range(1, K):
        out = out + up[:, j:j + T, :] * w[:, j]
    return out


def hyena_filters(L, w1, b1, w2, b2, w3, freq, deltas):
    f32 = jnp.float32
    t = jnp.linspace(0.0, 1.0, L, dtype=f32)[:, None]
    ang = 2.0 * math.pi * jnp.arange(L, dtype=f32)[:, None] / L
    bands = jnp.linspace(1e-4, HY_BANDS - 1, HY_BANDS, dtype=f32)[None, :]
    feats = jnp.concatenate([t, jnp.cos(bands * ang), -jnp.sin(bands * ang)], axis=-1)
    h = jnp.sin(freq * (feats @ w1 + b1))
    h = jnp.sin(freq * (h @ w2 + b2))
    h = (h @ w3).reshape(L, 2, HY_ORDER, HY_W)
    window = jnp.exp(-t[:, :, None, None] * jnp.abs(deltas)) + HY_MOD_SHIFT
    h = h * window
    h_fwd, h_bwd = h[:, 0], h[:, 1]
    taps = jnp.concatenate([h_fwd, jnp.zeros_like(h_fwd[:1]), jnp.flip(h_bwd[1:], 0)], axis=0)
    taps = taps / jnp.sum(jnp.abs(taps), axis=0, keepdims=True)
    return jnp.fft.rfft(taps, axis=0)


def hyena_mix(proj, conv_w, skip, filt_f):
    L = proj.shape[1]
    u = centred_dwconv(proj, conv_w)
    v, x1, x2 = jnp.split(u, 3, axis=-1)
    z = v
    for o, gate in enumerate((x1, x2)):
        zf = jnp.fft.rfft(z, n=2 * L, axis=1)
        y = jnp.fft.irfft(zf * filt_f[:, o], n=2 * L, axis=1)[:, :L]
        z = gate * (y + z * skip[o])
    return z


def gated_delta_chunks(q, k, v, g, beta, s0):
    B, T, H, _ = q.shape
    C = GDN_CHUNK
    N = T // C

    def chunks(a):
        return jnp.moveaxis(a.reshape(B, N, C, H, *a.shape[3:]), 3, 1)

    q, k, v, g, beta = chunks(q), chunks(k), chunks(v), chunks(g), chunks(beta)
    gc = jnp.cumsum(g, axis=-1)
    tri = jnp.tril(jnp.ones((C, C), bool))
    strict = jnp.tril(jnp.ones((C, C), bool), -1)
    decay = jnp.exp(jnp.where(tri, gc[..., :, None] - gc[..., None, :], -jnp.inf))
    kb = k * beta[..., None]
    A = jnp.where(strict, jnp.einsum('bhnid,bhnjd->bhnij', kb, k) * decay, 0.0)
    eye = jnp.eye(C, dtype=jnp.float32)
    Tm = lax.linalg.triangular_solve(eye + A, jnp.broadcast_to(eye, A.shape),
                                     left_side=True, lower=True, unit_diagonal=True)
    u_val = jnp.einsum('bhnij,bhnjd->bhnid', Tm, v * beta[..., None])
    w_val = jnp.einsum('bhnij,bhnjd->bhnid', Tm, kb * jnp.exp(gc)[..., None])
    intra = jnp.einsum('bhnid,bhnjd->bhnij', q, k) * decay
    q_dec = q * jnp.exp(gc)[..., None]
    g_last = gc[..., -1]
    k_dec = k * jnp.exp(g_last[..., None] - gc)[..., None]

    def step(S, xs):
        qd, w, u, att, kd, gl = xs
        v_new = u - jnp.einsum('bhcd,bhde->bhce', w, S)
        o = jnp.einsum('bhcd,bhde->bhce', qd, S) + jnp.einsum('bhij,bhje->bhie', att, v_new)
        S = S * jnp.exp(gl)[..., None, None] + jnp.einsum('bhcd,bhce->bhde', kd, v_new)
        return S, o

    xs = tuple(jnp.moveaxis(a, 2, 0) for a in (q_dec, w_val, u_val, intra, k_dec, g_last))
    s_fin, o = lax.scan(step, s0, xs)
    o = jnp.transpose(o, (1, 0, 3, 2, 4)).reshape(B, T, H, -1)
    return o, s_fin


def gdn_mix(qkv, z, a, b, conv_w, a_log, dt_bias, norm_w, init_states, with_output):
    B, T, _ = qkv.shape
    u = jax.nn.silu(centred_dwconv(qkv, conv_w))
    q, k, v = [t.reshape(B, T, GDN_H, HEAD_DIM) for t in jnp.split(u, 3, axis=-1)]
    q = l2_normalize(q) * HEAD_DIM ** -0.5
    k = l2_normalize(k)
    g = -jnp.exp(a_log) * jax.nn.softplus(a.reshape(B, T, 2, GDN_H) + dt_bias)
    beta = jax.nn.sigmoid(b.reshape(B, T, 2, GDN_H))
    outs, finals = [], []
    for d in range(2):
        rev = (lambda t: jnp.flip(t, 1)) if d == 1 else (lambda t: t)
        o, s_fin = gated_delta_chunks(rev(q), rev(k), rev(v), rev(g[:, :, d]), rev(beta[:, :, d]), init_states[d])
        finals.append(s_fin)
        if with_output:
            outs.append(rev(o))
    if not with_output:
        return None, (finals[0], finals[1])
    o = rms_norm(outs[0] + outs[1], norm_w) * jax.nn.silu(z.reshape(B, T, GDN_H, HEAD_DIM))
    return o.reshape(B, T, GDN_H * HEAD_DIM), (finals[0], finals[1])


def axial_rope_angles(n):
    t = jnp.arange(n)
    row = (t // GRID_W).astype(jnp.float32)
    col = (t % GRID_W).astype(jnp.float32)
    nf = HEAD_DIM // 4
    inv = ROPE_THETA ** (-jnp.arange(nf, dtype=jnp.float32) / nf)
    return row[:, None] * inv, col[:, None] * inv


def rope_half(x, ang):
    x1, x2 = jnp.split(x, 2, axis=-1)
    cos, sin = jnp.cos(ang), jnp.sin(ang)
    return jnp.concatenate([x1 * cos - x2 * sin, x1 * sin + x2 * cos], axis=-1)


def axial_rope(x, ang_r, ang_c):
    half = HEAD_DIM // 2
    ar = ang_r[None, :, None, None, :]
    ac = ang_c[None, :, None, None, :]
    return jnp.concatenate([rope_half(x[..., :half], ar), rope_half(x[..., half:], ac)], axis=-1)


def diff_attend(q, k, v, lam):
    s = jnp.einsum('bqhcd,bkhcd->bhcqk', q, k) * HEAD_DIM ** -0.5
    p = jax.nn.softmax(s, axis=-1)
    a = p[:, :, 0] - lam * p[:, :, 1]
    return jnp.einsum('bhqk,bkhd->bqhd', a, v)


def diff_latent(q, k, v, kc, vc, lam, ang_r, ang_c):
    B, n = q.shape[:2]
    q = axial_rope(q, ang_r, ang_c)
    k_all = jnp.concatenate([axial_rope(k, ang_r, ang_c), kc], axis=1)
    v_all = jnp.concatenate([v, vc], axis=1)
    nb = n // Q_BLOCK
    qb = jnp.swapaxes(q.reshape(B, nb, Q_BLOCK, *q.shape[2:]), 0, 1)
    o = lax.map(lambda qi: diff_attend(qi, k_all, v_all, lam), qb)
    return jnp.swapaxes(o, 0, 1).reshape(B, n, DIFF_H, DIFF_VD)


def diff_finish(o, norm_w, lam_init):
    B, T = o.shape[:2]
    return (rms_norm(o, norm_w) * (1.0 - lam_init)).reshape(B, T, DIFF_H * DIFF_VD)


def split_diff(p):
    B, T, _ = p.shape
    q, k, v = jnp.split(p, 3, axis=-1)
    return (q.reshape(B, T, DIFF_H, 2, HEAD_DIM), k.reshape(B, T, DIFF_H, 2, HEAD_DIM),
            v.reshape(B, T, DIFF_H, DIFF_VD))


def dense_attend(q, k, v):
    s = jnp.einsum('bqhd,bkhd->bhqk', q, k) * HEAD_DIM ** -0.5
    p = jax.nn.softmax(s, axis=-1)
    return jnp.einsum('bhqk,bkhd->bqhd', p, v)


def na_latent(q, k, v, kc, vc, rpb):
    B, n, H, dh = q.shape
    R = n // GRID_W
    kh, kw = min(NA_ROWS, R), NA_COLS
    grid = lambda t: t.reshape(B, R, GRID_W, H, dh)
    kg, vg = grid(k), grid(v)
    cols = jnp.arange(GRID_W)
    cstart = jnp.clip(cols - kw // 2, 0, GRID_W - kw)
    col_ok = (cols[None, :] >= cstart[:, None]) & (cols[None, :] < cstart[:, None] + kw)
    mask = jnp.broadcast_to(col_ok[:, None, :], (GRID_W, kh, GRID_W)).reshape(GRID_W, kh * GRID_W)
    dc = jnp.clip(cols[None, :] - cols[:, None], -(kw - 1), kw - 1) + (kw - 1)
    rpb_c = rpb[:, :, dc]
    scale = dh ** -0.5

    def row(args):
        r, qr = args
        rs = jnp.clip(r - kh // 2, 0, R - kh)
        kr = lax.dynamic_slice_in_dim(kg, rs, kh, axis=1).reshape(B, kh * GRID_W, H, dh)
        vr = lax.dynamic_slice_in_dim(vg, rs, kh, axis=1).reshape(B, kh * GRID_W, H, dh)
        dr = rs + jnp.arange(kh) - r + (NA_ROWS - 1)
        bias = jnp.transpose(rpb_c[:, dr], (0, 2, 1, 3)).reshape(H, GRID_W, kh * GRID_W)
        s_win = jnp.einsum('bqhd,bkhd->bhqk', qr, kr) * scale + bias[None]
        s_win = jnp.where(mask, s_win, -jnp.inf)
        s_ctx = jnp.einsum('bqhd,bkhd->bhqk', qr, kc) * scale
        p = jax.nn.softmax(jnp.concatenate([s_win, s_ctx], axis=-1), axis=-1)
        nw = kh * GRID_W
        return (jnp.einsum('bhqk,bkhd->bqhd', p[..., :nw], vr)
                + jnp.einsum('bhqk,bkhd->bqhd', p[..., nw:], vc))

    o = lax.map(row, (jnp.arange(R), jnp.swapaxes(grid(q), 0, 1)))
    return jnp.swapaxes(o, 0, 1).reshape(B, n, H * dh)


def merge_branches(ys, gate_cols, proj, w_o):
    br = jnp.einsum('btmc,mcd->btmd', jnp.stack(ys, axis=2), proj)
    g = jax.nn.sigmoid(gate_cols.reshape(gate_cols.shape[0], gate_cols.shape[1], N_BRANCH, -1))
    return jnp.sum(g * br, axis=2) @ w_o


def moe_ffn(u, router_w, router_b, w1, w3, w2):
    n_tok, d = u.shape
    s = jax.nn.sigmoid(u @ router_w)
    sel = s + router_b
    per = N_EXPERTS // N_GROUPS
    group_score = lax.top_k(sel.reshape(n_tok, N_GROUPS, per), TOP_K)[0].sum(-1)
    best_group = jnp.argmax(group_score, axis=-1)
    in_group = (jnp.arange(N_EXPERTS) // per)[None, :] == best_group[:, None]
    _, idx = lax.top_k(jnp.where(in_group, sel, -jnp.inf), TOP_K)
    wts = jnp.take_along_axis(s, idx, axis=-1)
    wts = wts / jnp.sum(wts, -1, keepdims=True)
    n_slot = n_tok * TOP_K
    e_flat = idx.reshape(-1).astype(jnp.int32)
    t_flat = jnp.repeat(jnp.arange(n_tok, dtype=jnp.int32), TOP_K)
    w_flat = wts.reshape(-1)
    order = jnp.argsort(e_flat)
    e_s, t_s, w_s = e_flat[order], t_flat[order], w_flat[order]
    counts = jax.ops.segment_sum(jnp.ones_like(e_flat), e_flat, num_segments=N_EXPERTS)
    padded = (counts + MOE_BLOCK - 1) // MOE_BLOCK * MOE_BLOCK
    start = jnp.cumsum(counts) - counts
    pend = jnp.cumsum(padded)
    pstart = pend - padded
    dest = pstart[e_s] + jnp.arange(n_slot, dtype=jnp.int32) - start[e_s]
    n_blk = -(-n_slot // MOE_BLOCK) + N_EXPERTS
    cap = n_blk * MOE_BLOCK
    buf_tok = jnp.full((cap,), n_tok, jnp.int32).at[dest].set(t_s)
    buf_w = jnp.zeros((cap,), jnp.float32).at[dest].set(w_s)
    blk_exp = jnp.minimum(jnp.searchsorted(pend, jnp.arange(n_blk, dtype=jnp.int32) * MOE_BLOCK, side='right'),
                          N_EXPERTS - 1)
    u_pad = jnp.concatenate([u, jnp.zeros((1, d), u.dtype)], axis=0)
    xb = u_pad[buf_tok].reshape(n_blk, MOE_BLOCK, d)

    def expert_block(args):
        xi, e = args
        return (jax.nn.silu(xi @ w1[e]) * (xi @ w3[e])) @ w2[e]

    yb = lax.map(expert_block, (xb, blk_exp)).reshape(cap, d)
    y = jax.ops.segment_sum(yb * buf_w[:, None], buf_tok, num_segments=n_tok + 1)
    return y[:n_tok]


def _diff_pre_kernel(xq_ref, xk_ref, xv_ref, cos_ref, sin_ref, q_ref, k_ref, v_ref, *, n_lat_tiles, q_scale):
    is_lat = pl.program_id(0) < n_lat_tiles
    cos = jnp.tile(jnp.where(is_lat, cos_ref[...], 1.0), (1, DIFF_H))
    sin = jnp.tile(jnp.where(is_lat, sin_ref[...], 0.0), (1, DIFF_H))
    lane = lax.broadcasted_iota(jnp.int32, (1, MIX_W), 1)
    first_half = (lane % (HEAD_DIM // 2)) < HEAD_DIM // 4

    def rope(x):
        x = x.astype(jnp.float32)
        partner = jnp.where(first_half, pltpu.roll(x, MIX_W - HEAD_DIM // 4, axis=1),
                            pltpu.roll(x, HEAD_DIM // 4, axis=1))
        return x * cos + partner * sin

    q_ref[...] = (rope(xq_ref[...]) * q_scale).astype(q_ref.dtype)
    k_ref[0] = rope(xk_ref[...]).astype(k_ref.dtype)
    v_ref[0] = xv_ref[...]


def diff_pre(proj_main, cos_t, sin_t, B, n, m):
    T = proj_main.shape[0]
    R = MERGE_TILE
    lat_tiles, ctx_tiles = n // R, m // R
    n_lat_tiles = B * lat_tiles
    c0 = MAIN_OFF[4] // MIX_W
    cur = [pl.BlockSpec((R, MIX_W), functools.partial(lambda i, c: (i, c), c=c0 + p)) for p in range(3)]
    table = pl.BlockSpec((R, DIFF_VD), lambda i: (jnp.where(i < n_lat_tiles, i % lat_tiles, 0), 0))

    def key_rows(i):
        ic = i - n_lat_tiles
        return (jnp.where(i < n_lat_tiles, i // lat_tiles, ic // ctx_tiles),
                jnp.where(i < n_lat_tiles, i % lat_tiles, lat_tiles + ic % ctx_tiles), 0)

    kv = jax.ShapeDtypeStruct((B, n + m, MIX_W), jnp.bfloat16)
    return pl.pallas_call(
        functools.partial(_diff_pre_kernel, n_lat_tiles=n_lat_tiles,
                          q_scale=HEAD_DIM ** -0.5 * math.log2(math.e)),
        grid=(T // R,),
        in_specs=cur + [table, table],
        out_specs=[pl.BlockSpec((R, MIX_W), lambda i: (i, 0)),
                   pl.BlockSpec((1, R, MIX_W), key_rows), pl.BlockSpec((1, R, MIX_W), key_rows)],
        out_shape=[jax.ShapeDtypeStruct((T, MIX_W), jnp.bfloat16), kv, kv],
        compiler_params=pltpu.CompilerParams(dimension_semantics=("arbitrary",), vmem_limit_bytes=VMEM_LIMIT),
        name="diff_pre",
    )(proj_main, proj_main, proj_main, cos_t, sin_t)


def rope_tables(ang_r, ang_c):
    cos = jnp.concatenate([jnp.cos(ang_r)] * 2 + [jnp.cos(ang_c)] * 2, axis=-1)
    sin = jnp.concatenate([-jnp.sin(ang_r), jnp.sin(ang_r), -jnp.sin(ang_c), jnp.sin(ang_c)], axis=-1)
    return jnp.tile(cos, (1, 2)), jnp.tile(sin, (1, 2))


def diff_branch(proj_main, cos_t, sin_t, lam, norm_w, lam_init, B, n, m):
    q, k_all, v_all = diff_pre(proj_main, cos_t, sin_t, B, n, m)
    nk = n + m
    k_all = jnp.transpose(k_all.reshape(B, nk, DIFF_H, 2, HEAD_DIM), (0, 2, 3, 1, 4))
    vt_all = jnp.transpose(v_all.reshape(B, nk // DIFF_TK, DIFF_TK, DIFF_H, DIFF_VD), (0, 3, 1, 4, 2))
    return diff_attention(q[:B * n].reshape(B, n, MIX_W), k_all, vt_all, lam, norm_w, 1.0 - lam_init)


def _split_w_in(w):
    segs = jnp.split(w, SPLIT_IDX, axis=-1)
    ab = jnp.concatenate([segs[3], segs[4]], axis=-1)
    ab = jnp.pad(ab, ((0, 0), (0, LANE - ab.shape[-1])))
    main = jnp.concatenate([segs[7], segs[0], segs[1], segs[2], segs[5], segs[6]], axis=-1)
    return main.astype(jnp.bfloat16), ab.astype(jnp.bfloat16)


MAIN_SIZES = (N_BRANCH * D_MODEL, 3 * HY_W, 3 * MIX_W, MIX_W, 3 * MIX_W, 3 * MIX_W)
MAIN_OFF = tuple(sum(MAIN_SIZES[:i]) for i in range(len(MAIN_SIZES) + 1))
IN_TN = 1792


def kernel(x, c, ctx, c_ctx, w_mod, b_mod, w_in, hy_conv, hy_w1, hy_b1, hy_w2, hy_b2, hy_w3, hy_freq,
           hy_deltas, hy_skip, gdn_conv, gdn_a_log, gdn_dt_bias, gdn_norm, diff_lam, diff_norm, na_rpb,
           branch_proj, w_out, ln_g, ln_b, router_w, router_b, exp_w1, exp_w3, exp_w2):
    B, n, D = x.shape
    m = ctx.shape[1]
    depth = w_mod.shape[0]
    dn_alpha = (2 * depth) ** 0.25
    ang_r, ang_c = axial_rope_angles(n)
    zero_state = jnp.zeros((B, GDN_H, HEAD_DIM, HEAD_DIM), jnp.float32)
    bf16 = jnp.bfloat16
    n_lat = B * n
    dft_tables = _dft_tables(n)
    cos_t, sin_t = rope_tables(ang_r, ang_c)
    tok = jnp.concatenate([x.reshape(n_lat, D), ctx.reshape(B * m, D)], axis=0)
    for l in range(depth):
        ctx_out = l < depth - 1
        mx = jax.nn.silu(c) @ w_mod[l] + b_mod[l]
        mc = jax.nn.silu(c_ctx) @ w_mod[l] + b_mod[l]
        one_plus = jnp.array([0.0, 1.0, 0.0, 0.0, 1.0, 0.0], jnp.float32)[:, None]
        mod_x = mx.reshape(B, 6, D) + one_plus
        mod_c = mc.reshape(1, 6, D) + one_plus
        mod = jnp.concatenate([jnp.repeat(mod_x, n // MERGE_TILE, axis=0),
                               jnp.repeat(mod_c, B * m // MERGE_TILE, axis=0)], axis=0)

        proj, gab = ln_mod_matmul(tok, mod, *_split_w_in(w_in[l]), IN_TN)

        def seg(i, latent):
            rows = slice(0, n_lat) if latent else slice(n_lat, None)
            return proj[rows, MAIN_OFF[i]:MAIN_OFF[i + 1]].reshape(B, -1, MAIN_SIZES[i]).astype(jnp.float32)

        hy_x, gqkv_x, gz_x, dqkv_x, nqkv_x = [seg(i, True) for i in range(1, 6)]
        hy_c, gqkv_c, gz_c, dqkv_c, nqkv_c = [seg(i, False) for i in range(1, 6)]
        gab_x, gab_c = gab[:n_lat].reshape(B, n, LANE), gab[n_lat:].reshape(B, m, LANE)
        ga_x, gb_x = gab_x[..., :2 * GDN_H], gab_x[..., 2 * GDN_H:4 * GDN_H]
        ga_c, gb_c = gab_c[..., :2 * GDN_H], gab_c[..., 2 * GDN_H:4 * GDN_H]
        filt = (hy_w1[l], hy_b1[l], hy_w2[l], hy_b2[l], hy_w3[l], hy_freq[l], hy_deltas[l])

        ya_x = hyena_latent(hy_x, hy_conv[l], hy_skip[l], filt, dft_tables)

        o_f, o_b = gdn_branch(proj, gab, gdn_conv[l], gdn_a_log[l], gdn_dt_bias[l], B, n, m)

        lq1, lk1, lq2, lk2 = diff_lam[l]
        lam_init = 0.8 - 0.6 * math.exp(-0.3 * l)
        lam = jnp.exp(jnp.sum(lq1 * lk1)) - jnp.exp(jnp.sum(lq2 * lk2)) + lam_init
        yc_x = diff_branch(proj, cos_t, sin_t, lam, diff_norm[l], lam_init, B, n, m)
        yd_x = na_attention(proj, na_bias_tables(na_rpb[l], n // GRID_W), B, n, m)

        ys = [t.reshape(n_lat, MIX_W) for t in (ya_x, yc_x, yd_x)]
        if ctx_out:
            dq_c, dk_c, dv_c = split_diff(dqkv_c)
            nq_c, nk_c, nv_c = [t.reshape(B, m, NA_H, HEAD_DIM) for t in jnp.split(nqkv_c, 3, axis=-1)]
            ya_c = hyena_mix(hy_c, hy_conv[l], hy_skip[l], hyena_filters(m, *filt))
            yc_c = diff_finish(diff_attend(dq_c, dk_c, dv_c, lam), diff_norm[l], lam_init)
            yd_c = dense_attend(nq_c, nk_c, nv_c)
            ys_c = [t.reshape(B * m, MIX_W) for t in (ya_c, yc_c, yd_c)]
            ys = [jnp.concatenate([a, b], axis=0) for a, b in zip(ys, ys_c)]
        else:
            tok = tok[:n_lat]
        tok = merge_residual_ln(tok, ys[0], o_f, o_b, ys[1], ys[2], proj, mod, branch_proj[l].astype(bf16),
                                w_out[l].astype(bf16), gdn_norm[l], ln_g[l, 0], ln_b[l, 0], dn_alpha, n, m)

        n_tok = tok.shape[0]
        h2, idx, wts = moe_route(tok, mod, router_w, router_b)
        buf_tok, dest, blk_exp, n_used = moe_dispatch(idx, n_tok)
        yb = expert_ffn(blk_exp, n_used, jnp.take(h2, buf_tok, axis=0), exp_w1[l], exp_w3[l], exp_w2[l])
        tok = moe_combine_ln(tok, yb, dest.reshape(n_tok, TOP_K), wts, mod, ln_g[l, 1], ln_b[l, 1], dn_alpha)
    return tok[:n_lat].reshape(B, n, D)
```

```python
import functools
import math

import jax
import jax.numpy as jnp
from jax import lax
from jax.experimental import pallas as pl
from jax.experimental.pallas import tpu as pltpu

D_MODEL = 1024
GRID_W = 64
HEAD_DIM = 64
MIX_W = D_MODEL // 2
N_BRANCH = 4
HY_W = MIX_W
HY_ORDER = 2
HY_BANDS = 16
HY_MOD_SHIFT = 0.05
GDN_H = MIX_W // HEAD_DIM
GDN_CHUNK = 64
GDN_CONV = 3
DIFF_VD = 2 * HEAD_DIM
DIFF_H = MIX_W // DIFF_VD
NA_H = MIX_W // HEAD_DIM
NA_ROWS = 8
NA_COLS = 16
N_EXPERTS = 16
N_GROUPS = 4
TOP_K = 2
ROPE_THETA = 10000.0
LN_EPS = 1e-5
RMS_EPS = 1e-6
SPLIT_SIZES = (3 * HY_W, 3 * MIX_W, MIX_W, 2 * GDN_H, 2 * GDN_H, 3 * MIX_W, 3 * MIX_W, N_BRANCH * D_MODEL)
SPLIT_IDX = tuple(sum(SPLIT_SIZES[:i + 1]) for i in range(len(SPLIT_SIZES) - 1))

LANE = 128
ROW_TILE = 1536
MERGE_TILE = 256
VMEM_LIMIT = 48 * 1024 * 1024


def _ln_mod_matmul_kernel(x_ref, mod_ref, w_ref, wg_ref, o_ref, og_ref, h_ref):
    @pl.when(pl.program_id(1) == 0)
    def _():
        for s in range(mod_ref.shape[0]):
            rows = slice(s * MERGE_TILE, (s + 1) * MERGE_TILE)
            y = _layer_norm_rows(x_ref[rows, :])
            h = (y * mod_ref[s, 1:2, :] + mod_ref[s, 0:1, :]).astype(h_ref.dtype)
            h_ref[rows, :] = h
            og_ref[rows, :] = jnp.dot(h, wg_ref[...], preferred_element_type=jnp.float32)

    o_ref[...] = jnp.dot(h_ref[...], w_ref[...], preferred_element_type=jnp.float32).astype(o_ref.dtype)


def ln_mod_matmul(x, mod, w, w_gates, tn):
    T, D = x.shape
    N = w.shape[1]
    return pl.pallas_call(
        _ln_mod_matmul_kernel,
        grid=(T // ROW_TILE, N // tn),
        in_specs=[
            pl.BlockSpec((ROW_TILE, D), lambda i, j: (i, 0)),
            pl.BlockSpec((ROW_TILE // MERGE_TILE, 6, D), lambda i, j: (i, 0, 0)),
            pl.BlockSpec((D, tn), lambda i, j: (0, j)),
            pl.BlockSpec((D, LANE), lambda i, j: (0, 0)),
        ],
        out_specs=[pl.BlockSpec((ROW_TILE, tn), lambda i, j: (i, j)),
                   pl.BlockSpec((ROW_TILE, LANE), lambda i, j: (i, 0))],
        out_shape=[jax.ShapeDtypeStruct((T, N), jnp.bfloat16), jax.ShapeDtypeStruct((T, LANE), jnp.float32)],
        scratch_shapes=[pltpu.VMEM((ROW_TILE, D), jnp.bfloat16)],
        compiler_params=pltpu.CompilerParams(
            dimension_semantics=("arbitrary", "arbitrary"), vmem_limit_bytes=VMEM_LIMIT),
        name="ln_mod_matmul",
    )(x, mod, w, w_gates)


def _layer_norm_rows(x):
    mu = jnp.mean(x, axis=-1, keepdims=True)
    xc = x - mu
    var = jnp.mean(xc * xc, axis=-1, keepdims=True)
    return xc * lax.rsqrt(var + LN_EPS)


_NT_DIMS = (((1,), (1,)), ((), ()))


DIFF_TQ = 1024
DIFF_TK = 768
DIFF_ROWS = 32


def _diff_attn_kernel(lam_ref, q_ref, k_ref, vt_ref, nw_ref, o_ref, s_ref, *, out_scale):
    tq = q_ref.shape[1]
    n_tiles = k_ref.shape[3] // DIFF_TK
    q = q_ref[0]
    qc = (q[:, :HEAD_DIM], q[:, HEAD_DIM:])

    sub = 8
    fold = lambda t, op: op(t.reshape(t.shape[0] // sub, sub, t.shape[1]), axis=0)

    def scores(j, slot):
        start = pl.multiple_of(j * DIFF_TK, DIFF_TK)
        mx = []
        for c in range(2):
            k = k_ref[0, 0, c, pl.ds(start, DIFF_TK), :]
            st = lax.dot_general(k, qc[c], _NT_DIMS, preferred_element_type=jnp.float32)
            s_ref[slot, c] = st
            mx.append(fold(st, jnp.max))
        return tuple(mx)

    def softmax_pv(j, slot, mx, carry):
        new, ps = [], []
        for c in range(2):
            m_prev, l_prev, acc = carry[c]
            m_new = jnp.maximum(m_prev, jnp.max(mx[c], axis=0, keepdims=True))
            alpha = jnp.exp2(m_prev - m_new)
            psum, chunks = None, []
            for r in range(DIFF_TK // DIFF_ROWS):
                p = jnp.exp2(s_ref[slot, c, r * DIFF_ROWS:(r + 1) * DIFF_ROWS, :] - m_new)
                part = fold(p, jnp.sum)
                psum = part if psum is None else psum + part
                chunks.append(p.astype(jnp.bfloat16))
            new.append((m_new, alpha * l_prev + jnp.sum(psum, axis=0, keepdims=True), alpha * acc))
            ps.append(jnp.concatenate(chunks, axis=0))
        pv = jnp.dot(vt_ref[0, 0, j], jnp.concatenate(ps, axis=1), preferred_element_type=jnp.float32)
        return tuple((new[c][0], new[c][1], new[c][2] + pv[:, c * tq:(c + 1) * tq]) for c in range(2))

    def pair(jj, state):
        carry, mx0 = state
        j0 = 2 * jj
        mx1 = scores(j0 + 1, 1)
        carry = softmax_pv(j0, 0, mx0, carry)
        mx0 = scores(j0 + 2, 0)
        return softmax_pv(j0 + 1, 1, mx1, carry), mx0

    carry = tuple((jnp.full((1, tq), -jnp.inf, jnp.float32), jnp.zeros((1, tq), jnp.float32),
                   jnp.zeros((DIFF_VD, tq), jnp.float32)) for _ in range(2))
    mx0 = scores(0, 0)
    n_pairs = (n_tiles - 1) // 2
    carry, mx0 = lax.fori_loop(0, n_pairs, pair, (carry, mx0))
    if n_tiles % 2 == 0:
        mx1 = scores(n_tiles - 1, 1)
        carry = softmax_pv(n_tiles - 2, 0, mx0, carry)
        carry = softmax_pv(n_tiles - 1, 1, mx1, carry)
    else:
        carry = softmax_pv(n_tiles - 1, 0, mx0, carry)
    (_, l0, a0), (_, l1, a1) = carry
    o = a0 / l0 - lam_ref[0] * (a1 / l1)
    o = o * lax.rsqrt(jnp.mean(o * o, axis=0, keepdims=True) + RMS_EPS) * (nw_ref[...] * out_scale)
    o_ref[0] = o.T.astype(o_ref.dtype)


def diff_attention(q, k, vt, lam, norm_w, out_scale, out_dtype=jnp.float32):
    B, nq, W = q.shape
    nk = k.shape[3]
    assert nq % DIFF_TQ == 0 and nk % DIFF_TK == 0
    H = W // DIFF_VD
    return pl.pallas_call(
        functools.partial(_diff_attn_kernel, out_scale=out_scale),
        grid=(B, H, nq // DIFF_TQ),
        in_specs=[
            pl.BlockSpec(memory_space=pltpu.SMEM),
            pl.BlockSpec((1, DIFF_TQ, DIFF_VD), lambda b, h, i: (b, i, h)),
            pl.BlockSpec((1, 1, 2, nk, HEAD_DIM), lambda b, h, i: (b, h, 0, 0, 0)),
            pl.BlockSpec((1, 1, nk // DIFF_TK, DIFF_VD, DIFF_TK), lambda b, h, i: (b, h, 0, 0, 0)),
            pl.BlockSpec((DIFF_VD, 1), lambda b, h, i: (0, 0)),
        ],
        out_specs=pl.BlockSpec((1, DIFF_TQ, DIFF_VD), lambda b, h, i: (b, i, h)),
        out_shape=jax.ShapeDtypeStruct((B, nq, W), out_dtype),
        scratch_shapes=[pltpu.VMEM((2, 2, DIFF_TK, DIFF_TQ), jnp.float32)],
        compiler_params=pltpu.CompilerParams(
            dimension_semantics=("arbitrary",) * 3, vmem_limit_bytes=VMEM_LIMIT),
        name="diff_attention",
    )(lam.reshape(1).astype(jnp.float32), q, k, vt, norm_w.reshape(DIFF_VD, 1))


NA_QROWS = 8
NA_KROWS = 16
NA_NEG = -1e30


def _na_kernel(q_ref, k_ref, v_ref, kc_ref, vc_ref, tab_ref, o_ref, *, n_kblocks):
    g = pl.program_id(2)
    kb = (NA_KROWS // 4) * GRID_W
    base = jnp.clip(2 * g - 1, 0, n_kblocks - 4)
    start = pl.multiple_of(base * kb, kb)
    nwin = NA_KROWS * GRID_W
    q = q_ref[...] * HEAD_DIM ** -0.5
    kw = k_ref[pl.ds(start, nwin), :]
    vw = v_ref[pl.ds(start, nwin), :]
    kc = kc_ref[...]
    vc = vc_ref[...]
    outs = []
    for hh in range(2):
        sl = slice(hh * HEAD_DIM, (hh + 1) * HEAD_DIM)
        qh = q[:, sl]
        s = lax.dot_general(qh, kw[:, sl], _NT_DIMS, preferred_element_type=jnp.float32) + tab_ref[0, hh].astype(jnp.float32)
        sc = lax.dot_general(qh, kc[:, sl], _NT_DIMS, preferred_element_type=jnp.float32)
        m = jnp.maximum(jnp.max(s, axis=-1, keepdims=True), jnp.max(sc, axis=-1, keepdims=True))
        p = jnp.exp(s - m)
        pc = jnp.exp(sc - m)
        l = jnp.sum(p, axis=-1, keepdims=True) + jnp.sum(pc, axis=-1, keepdims=True)
        o = (jnp.dot(p.astype(vw.dtype), vw[:, sl], preferred_element_type=jnp.float32)
             + jnp.dot(pc.astype(vc.dtype), vc[:, sl], preferred_element_type=jnp.float32))
        outs.append(o / l)
    o_ref[...] = jnp.concatenate(outs, axis=-1).astype(o_ref.dtype)


def na_bias_tables(rpb, R):
    H = rpb.shape[0]
    G = R // NA_QROWS
    cols = jnp.arange(GRID_W)
    cstart = jnp.clip(cols - NA_COLS // 2, 0, GRID_W - NA_COLS)
    col_ok = (cols[None, :] >= cstart[:, None]) & (cols[None, :] < cstart[:, None] + NA_COLS)
    dc = jnp.clip(cols[None, :] - cols[:, None], -(NA_COLS - 1), NA_COLS - 1) + (NA_COLS - 1)
    rpb_c = rpb[:, :, dc]
    tabs = []
    for g in (0, 1, G - 1):
        base = min(max(2 * g - 1, 0), R // 4 - 4)
        r = NA_QROWS * g + jnp.arange(NA_QROWS)
        rs = jnp.clip(r - NA_ROWS // 2, 0, R - NA_ROWS)
        kr = 4 * base + jnp.arange(NA_KROWS)
        valid = (kr[None, :] >= rs[:, None]) & (kr[None, :] < rs[:, None] + NA_ROWS)
        dr = jnp.clip(kr[None, :] - r[:, None] + (NA_ROWS - 1), 0, 2 * NA_ROWS - 2)
        tab = rpb_c[:, dr]
        ok = valid[None, :, :, None, None] & col_ok[None, None, None, :, :]
        tab = jnp.where(ok, tab, NA_NEG)
        tabs.append(jnp.transpose(tab, (0, 1, 3, 2, 4)).reshape(H, NA_QROWS * GRID_W, NA_KROWS * GRID_W))
    return jnp.stack(tabs, axis=0).astype(jnp.bfloat16)


def na_attention(proj_main, tabs, B, n, m, out_dtype=jnp.float32):
    W = MIX_W
    tq = NA_QROWS * GRID_W
    G = n // tq
    n_kblocks = n // ((NA_KROWS // 4) * GRID_W)
    cq, ck, cv = [(MAIN_OFF[5] + p * W) // LANE for p in range(3)]
    ctx0 = B * n // m

    def tab_index(b, hp, g):
        return (jnp.where(g == 0, 0, jnp.where(g == G - 1, 2, 1)), hp, 0, 0)

    return pl.pallas_call(
        functools.partial(_na_kernel, n_kblocks=n_kblocks),
        grid=(B, W // LANE, G),
        in_specs=[
            pl.BlockSpec((tq, LANE), lambda b, hp, g: (b * G + g, cq + hp)),
            pl.BlockSpec((n, LANE), lambda b, hp, g: (b, ck + hp)),
            pl.BlockSpec((n, LANE), lambda b, hp, g: (b, cv + hp)),
            pl.BlockSpec((m, LANE), lambda b, hp, g: (ctx0 + b, ck + hp)),
            pl.BlockSpec((m, LANE), lambda b, hp, g: (ctx0 + b, cv + hp)),
            pl.BlockSpec((1, 2, tq, NA_KROWS * GRID_W), tab_index),
        ],
        out_specs=pl.BlockSpec((tq, LANE), lambda b, hp, g: (b * G + g, hp)),
        out_shape=jax.ShapeDtypeStruct((B * n, W), out_dtype),
        compiler_params=pltpu.CompilerParams(
            dimension_semantics=("arbitrary",) * 3, vmem_limit_bytes=VMEM_LIMIT),
        name="na_attention",
    )(proj_main, proj_main, proj_main, proj_main, proj_main, tabs)


def _merge_kernel(x_ref, ya_ref, of_ref, ob_ref, yc_ref, yd_ref, gc_ref, gz_ref, mod_ref, p_ref, wo_ref,
                  gn_ref, hs_ref, lng_ref, lnb_ref, o_ref, *, alpha):
    o = of_ref[0] + ob_ref[0]
    sq = o * o
    hi = sq.astype(jnp.bfloat16)
    lo = (sq - hi.astype(jnp.float32)).astype(jnp.bfloat16)
    ms = (jnp.dot(hi, hs_ref[...], preferred_element_type=jnp.float32)
          + jnp.dot(lo, hs_ref[...], preferred_element_type=jnp.float32)) * (1.0 / HEAD_DIM)
    gz = gz_ref[...].astype(jnp.float32)
    yb = o * lax.rsqrt(ms + RMS_EPS) * gn_ref[...] * (gz * jax.nn.sigmoid(gz))
    acc = None
    for mi, y in enumerate((ya_ref[...], yb, yc_ref[...], yd_ref[...])):
        br = jnp.dot(y.astype(jnp.bfloat16), p_ref[mi], preferred_element_type=jnp.float32)
        gm = jax.nn.sigmoid(gc_ref[:, mi * D_MODEL:(mi + 1) * D_MODEL].astype(jnp.float32))
        acc = gm * br if acc is None else acc + gm * br
    mix = jnp.dot(acc.astype(jnp.bfloat16), wo_ref[...], preferred_element_type=jnp.float32)
    z = alpha * x_ref[...] + mod_ref[0, 2:3, :] * mix
    o_ref[...] = _layer_norm_rows(z) * lng_ref[...] + lnb_ref[...]


def merge_residual_ln(x, ya, o_f, o_b, yc, yd, proj_main, mod, proj, w_o, gdn_norm, ln_g, ln_b, alpha, n, m):
    T, D = x.shape
    R = MERGE_TILE
    lat_tiles, ctx_tiles = n // R, m // R
    scan_rows = functools.partial(_stream_to_scan_rows, n_lat_tiles=o_f.shape[0] * lat_tiles,
                                  lat_tiles=lat_tiles, ctx_tiles=ctx_tiles)
    row = lambda i: (i, 0)
    const2 = lambda i: (0, 0)
    branch = pl.BlockSpec((R, MIX_W), row)
    scan = pl.BlockSpec((1, R, MIX_W), scan_rows)
    head_sum = (jnp.arange(MIX_W)[:, None] // HEAD_DIM == jnp.arange(MIX_W)[None, :] // HEAD_DIM).astype(jnp.bfloat16)
    return pl.pallas_call(
        functools.partial(_merge_kernel, alpha=alpha),
        grid=(T // R,),
        in_specs=[pl.BlockSpec((R, D), row), branch, scan, scan, branch, branch,
                  pl.BlockSpec((R, N_BRANCH * D), row),
                  pl.BlockSpec((R, MIX_W), lambda i: (i, MAIN_OFF[3] // MIX_W)),
                  pl.BlockSpec((1, 6, D), lambda i: (i, 0, 0)),
                  pl.BlockSpec((N_BRANCH, MIX_W, D), lambda i: (0, 0, 0)),
                  pl.BlockSpec((D, D), const2),
                  pl.BlockSpec((1, MIX_W), const2),
                  pl.BlockSpec((MIX_W, MIX_W), const2),
                  pl.BlockSpec((1, D), const2),
                  pl.BlockSpec((1, D), const2)],
        out_specs=pl.BlockSpec((R, D), row),
        out_shape=jax.ShapeDtypeStruct((T, D), jnp.float32),
        compiler_params=pltpu.CompilerParams(dimension_semantics=("arbitrary",), vmem_limit_bytes=VMEM_LIMIT),
        name="merge_residual_ln",
    )(x, ya, o_f, o_b, yc, yd, proj_main, proj_main, mod, proj, w_o,
      jnp.tile(gdn_norm, GDN_H).reshape(1, MIX_W), head_sum, ln_g.reshape(1, D), ln_b.reshape(1, D))


def _route_kernel(x_ref, mod_ref, rw_ref, rb_ref, h_ref, idx_ref, wts_ref):
    h = (_layer_norm_rows(x_ref[...]) * mod_ref[0, 4:5, :] + mod_ref[0, 3:4, :]).astype(jnp.bfloat16)
    h_ref[...] = h.astype(h_ref.dtype)
    logits = lax.dot_general(rw_ref[...], h, _NT_DIMS, preferred_element_type=jnp.float32)
    s = jax.nn.sigmoid(logits)
    sel = s + rb_ref[...]
    per = N_EXPERTS // N_GROUPS
    srow = [s[e:e + 1, :] for e in range(N_EXPERTS)]
    vrow = [sel[e:e + 1, :] for e in range(N_EXPERTS)]
    best = None
    for gi in range(N_GROUPS):
        grp = vrow[gi * per:(gi + 1) * per]
        gs = None
        for a in range(per):
            for b in range(a + 1, per):
                ps = grp[a] + grp[b]
                gs = ps if gs is None else jnp.maximum(gs, ps)
        if best is None:
            best, bg = gs, jnp.zeros(gs.shape, jnp.int32)
        else:
            upd = gs > best
            bg = jnp.where(upd, gi, bg)
            best = jnp.where(upd, gs, best)
    cv, cs = [], []
    for j in range(per):
        v_j, s_j = vrow[j], srow[j]
        for gi in range(1, N_GROUPS):
            v_j = jnp.where(bg == gi, vrow[gi * per + j], v_j)
            s_j = jnp.where(bg == gi, srow[gi * per + j], s_j)
        cv.append(v_j)
        cs.append(s_j)

    def first_argmax(vals):
        bv, bi = vals[0], jnp.zeros(vals[0].shape, jnp.int32)
        for j in range(1, per):
            upd = vals[j] > bv
            bi = jnp.where(upd, j, bi)
            bv = jnp.where(upd, vals[j], bv)
        return bi

    i1 = first_argmax(cv)
    i2 = first_argmax([jnp.where(i1 == j, -jnp.inf, cv[j]) for j in range(per)])
    w1 = cs[0]
    w2 = cs[0]
    for j in range(1, per):
        w1 = jnp.where(i1 == j, cs[j], w1)
        w2 = jnp.where(i2 == j, cs[j], w2)
    tot = w1 + w2
    idx_ref[...] = jnp.concatenate([bg * per + i1, bg * per + i2], axis=0)
    wts_ref[...] = jnp.concatenate([w1 / tot, w2 / tot], axis=0)


def moe_route(x, mod, router_w, router_b):
    T, D = x.shape
    return pl.pallas_call(
        _route_kernel,
        grid=(T // MERGE_TILE,),
        in_specs=[pl.BlockSpec((MERGE_TILE, D), lambda i: (i, 0)),
                  pl.BlockSpec((1, 6, D), lambda i: (i, 0, 0)),
                  pl.BlockSpec((N_EXPERTS, D), lambda i: (0, 0)),
                  pl.BlockSpec((N_EXPERTS, 1), lambda i: (0, 0))],
        out_specs=[pl.BlockSpec((MERGE_TILE, D), lambda i: (i, 0)),
                   pl.BlockSpec((TOP_K, MERGE_TILE), lambda i: (0, i)),
                   pl.BlockSpec((TOP_K, MERGE_TILE), lambda i: (0, i))],
        out_shape=[jax.ShapeDtypeStruct((T, D), jnp.float32),
                   jax.ShapeDtypeStruct((TOP_K, T), jnp.int32),
                   jax.ShapeDtypeStruct((TOP_K, T), jnp.float32)],
        compiler_params=pltpu.CompilerParams(dimension_semantics=("arbitrary",), vmem_limit_bytes=VMEM_LIMIT),
        name="moe_route",
    )(x, mod, router_w.T.astype(jnp.bfloat16), router_b.reshape(N_EXPERTS, 1).astype(jnp.float32))


EXPERT_BLOCK = 256


def _expert_kernel(be_ref, nu_ref, x_ref, w1_ref, w3_ref, w2_ref, o_ref):
    i = pl.program_id(0)

    @pl.when(i < nu_ref[0])
    def _():
        x = x_ref[...].astype(jnp.bfloat16)
        bf = lambda ref: ref[0].astype(jnp.bfloat16)
        a = jnp.dot(x, bf(w1_ref), preferred_element_type=jnp.float32)
        b = jnp.dot(x, bf(w3_ref), preferred_element_type=jnp.float32)
        hmid = (a * jax.nn.sigmoid(a) * b).astype(jnp.bfloat16)
        o_ref[...] = jnp.dot(hmid, bf(w2_ref), preferred_element_type=jnp.float32).astype(o_ref.dtype)

    @pl.when(i >= nu_ref[0])
    def _():
        o_ref[...] = jnp.zeros(o_ref.shape, o_ref.dtype)


def expert_ffn(blk_exp, n_used, xb, w1, w3, w2, out_dtype=jnp.float32):
    cap, D = xb.shape
    F = w1.shape[-1]
    n_blk = cap // EXPERT_BLOCK
    return pl.pallas_call(
        _expert_kernel,
        grid_spec=pltpu.PrefetchScalarGridSpec(
            num_scalar_prefetch=2,
            grid=(n_blk,),
            in_specs=[pl.BlockSpec((EXPERT_BLOCK, D), lambda i, be, nu: (i, 0)),
                      pl.BlockSpec((1, D, F), lambda i, be, nu: (be[i], 0, 0)),
                      pl.BlockSpec((1, D, F), lambda i, be, nu: (be[i], 0, 0)),
                      pl.BlockSpec((1, F, D), lambda i, be, nu: (be[i], 0, 0))],
            out_specs=pl.BlockSpec((EXPERT_BLOCK, D), lambda i, be, nu: (i, 0)),
        ),
        out_shape=jax.ShapeDtypeStruct((cap, D), out_dtype),
        compiler_params=pltpu.CompilerParams(dimension_semantics=("arbitrary",), vmem_limit_bytes=VMEM_LIMIT),
        name="expert_ffn",
    )(blk_exp, n_used, xb, w1, w3, w2)


def _gdn_kernel(qf_ref, kf_ref, vf_ref, gf_ref, qb_ref, kb_ref, vb_ref, gb_ref, of_ref, ob_ref, s_ref):
    C = GDN_CHUNK
    nb = qf_ref.shape[0]

    @pl.when(pl.program_id(0) == 0)
    def _():
        s_ref[...] = jnp.zeros(s_ref.shape, jnp.float32)

    bf = lambda t: t.astype(jnp.bfloat16)
    mm = lambda a, b: jnp.dot(bf(a), bf(b), preferred_element_type=jnp.float32)
    nt = lambda a, b: lax.dot_general(bf(a), bf(b), _NT_DIMS, preferred_element_type=jnp.float32)
    tn = lambda a, b: lax.dot_general(bf(a), bf(b), (((0,), (0,)), ((), ())), preferred_element_type=jnp.float32)
    col = lambda t, h: t[:, h:h + 1]
    rel = lax.broadcasted_iota(jnp.int32, (C, C), 0) - lax.broadcasted_iota(jnp.int32, (C, C), 1)

    qs, ks, vs, decay, beta_c, e_gc, e_rest, e_tot, strict = [], [], [], [], [], [], [], [], []
    for bi in range(nb):
        for d, refs in enumerate(((qf_ref, kf_ref, vf_ref, gf_ref), (qb_ref, kb_ref, vb_ref, gb_ref))):
            q_ref, k_ref, v_ref, g_ref = refs
            incl = rel >= 0 if d == 0 else rel <= 0
            g = g_ref[bi]
            gc = jnp.dot(incl.astype(jnp.float32), g, precision=lax.Precision.HIGHEST,
                         preferred_element_type=jnp.float32)
            gc_t = gc.T
            tot = jnp.sum(g, axis=0, keepdims=True)
            eg, er, et = jnp.exp(gc), jnp.exp(tot - gc), jnp.exp(tot)
            q, k, v = q_ref[bi], k_ref[bi], v_ref[bi]
            for h in range(GDN_H):
                sl = slice(h * HEAD_DIM, (h + 1) * HEAD_DIM)
                gl = d * GDN_H + h
                qs.append(q[:, sl])
                ks.append(k[:, sl])
                vs.append(v[:, sl])
                decay.append(jnp.where(incl, jnp.exp(col(gc, gl) - gc_t[gl:gl + 1, :]), 0.0))
                strict.append(rel > 0 if d == 0 else rel < 0)
                beta_c.append(col(g, 2 * GDN_H + gl))
                e_gc.append(col(eg, gl))
                e_rest.append(col(er, gl))
                e_tot.append(col(et, gl))
    chains = range(len(qs))
    kb = [ks[c] * beta_c[c] for c in chains]
    a_mat = [jnp.where(strict[c], nt(kb[c], ks[c]) * decay[c], 0.0) for c in chains]
    intra = [nt(qs[c], ks[c]) * decay[c] for c in chains]
    ri = lax.broadcasted_iota(jnp.int32, (C, C), 0)
    ci = lax.broadcasted_iota(jnp.int32, (C, C), 1)

    def level_mask(s, upper):
        sh = s.bit_length() - 1
        same = (ri >> (sh + 1)) == (ci >> (sh + 1))
        r_hi = ((ri >> sh) & 1) == 1
        c_hi = ((ci >> sh) & 1) == 1
        return same & c_hi & ~r_hi if upper else same & r_hi & ~c_hi

    masks = {s: (level_mask(s, False), level_mask(s, True)) for s in (1, 2, 4, 8, 16, 32)}
    upper = [(c // GDN_H) % 2 == 1 for c in chains]
    eye = (ri == ci).astype(jnp.float32)
    t_inv = [eye - jnp.where(masks[1][upper[c]], a_mat[c], 0.0) for c in chains]
    for s in (2, 4, 8, 16, 32):
        tm = [mm(t_inv[c], jnp.where(masks[s][upper[c]], a_mat[c], 0.0)) for c in chains]
        t_inv = [t_inv[c] - mm(tm[c], t_inv[c]) for c in chains]
    xs = [mm(t_inv[c], jnp.concatenate([vs[c] * beta_c[c], kb[c] * e_gc[c]], axis=1)) for c in chains]
    st = [s_ref[c] for c in chains]
    v_new = [xs[c][:, :HEAD_DIM] - mm(xs[c][:, HEAD_DIM:], st[c]) for c in chains]
    o_st = [mm(qs[c] * e_gc[c], st[c]) for c in chains]
    o_in = [mm(intra[c], v_new[c]) for c in chains]
    s_up = [tn(ks[c] * e_rest[c], v_new[c]) for c in chains]
    for c in chains:
        s_ref[c] = st[c] * e_tot[c] + s_up[c]
    outs = [o_st[c] + o_in[c] for c in chains]
    for bi in range(nb):
        for d, o_ref in enumerate((of_ref, ob_ref)):
            c0 = (bi * 2 + d) * GDN_H
            o_ref[bi] = jnp.concatenate(outs[c0:c0 + GDN_H], axis=1)


def gdn_scan(q, k, v, gates, n_ctx_chunks):
    B, Tt, W = q.shape
    C = GDN_CHUNK
    NC = Tt // C

    def bwd(s):
        return jnp.where(s < n_ctx_chunks, n_ctx_chunks - 1 - s, NC - 1 - (s - n_ctx_chunks))

    tok_f = pl.BlockSpec((B, C, W), lambda s: (0, s, 0))
    tok_b = pl.BlockSpec((B, C, W), lambda s: (0, bwd(s), 0))
    gate_f = pl.BlockSpec((B, C, LANE), lambda s: (0, s, 0))
    gate_b = pl.BlockSpec((B, C, LANE), lambda s: (0, bwd(s), 0))
    return pl.pallas_call(
        _gdn_kernel,
        grid=(NC,),
        in_specs=[tok_f, tok_f, tok_f, gate_f, tok_b, tok_b, tok_b, gate_b],
        out_specs=[tok_f, tok_b],
        out_shape=[jax.ShapeDtypeStruct((B, Tt, W), jnp.float32)] * 2,
        scratch_shapes=[pltpu.VMEM((B * 2 * GDN_H, HEAD_DIM, HEAD_DIM), jnp.float32)],
        compiler_params=pltpu.CompilerParams(dimension_semantics=("arbitrary",), vmem_limit_bytes=VMEM_LIMIT),
        name="gdn_scan",
    )(q, k, v, gates, q, k, v, gates)


def _stream_to_scan_rows(i, n_lat_tiles, lat_tiles, ctx_tiles):
    ic = i - n_lat_tiles
    return (jnp.where(i < n_lat_tiles, i // lat_tiles, ic // ctx_tiles),
            jnp.where(i < n_lat_tiles, ctx_tiles + i % lat_tiles, ic % ctx_tiles), 0)


def _gdn_pre_kernel(xq_ref, xk_ref, xv_ref, pq_ref, pk_ref, pv_ref, nq_ref, nk_ref, nv_ref, gab_ref, cw_ref,
                    al_ref, dtb_ref, hs_ref, q_ref, k_ref, v_ref, gate_ref, *, n_lat_tiles, lat_tiles, ctx_tiles):
    i = pl.program_id(0)
    R = xq_ref.shape[0]
    pos = jnp.where(i < n_lat_tiles, i % lat_tiles, (i - n_lat_tiles) % ctx_tiles)
    last = jnp.where(i < n_lat_tiles, lat_tiles - 1, ctx_tiles - 1)
    rows = lax.broadcasted_iota(jnp.int32, (R, 1), 0)
    f32 = lambda t: t.astype(jnp.float32)

    def conv_silu(x_ref, p_ref, n_ref, part):
        x = f32(x_ref[...])
        before = jnp.where(pos == 0, 0.0, f32(p_ref[7:8, :]))
        after = jnp.where(pos == last, 0.0, f32(n_ref[0:1, :]))
        x_prev = jnp.where(rows == 0, before, pltpu.roll(x, 1, axis=0))
        x_next = jnp.where(rows == R - 1, after, pltpu.roll(x, R - 1, axis=0))
        cw = cw_ref[:, part * MIX_W:(part + 1) * MIX_W]
        u = x_prev * cw[0:1] + x * cw[1:2] + x_next * cw[2:3]
        return u * jax.nn.sigmoid(u)

    def head_sumsq(t):
        sq = t * t
        hi = sq.astype(jnp.bfloat16)
        lo = (sq - f32(hi)).astype(jnp.bfloat16)
        return (jnp.dot(hi, hs_ref[...], preferred_element_type=jnp.float32)
                + jnp.dot(lo, hs_ref[...], preferred_element_type=jnp.float32))

    q = conv_silu(xq_ref, pq_ref, nq_ref, 0)
    k = conv_silu(xk_ref, pk_ref, nk_ref, 1)
    q_ref[0] = q * lax.rsqrt(head_sumsq(q) + RMS_EPS) * HEAD_DIM ** -0.5
    k_ref[0] = k * lax.rsqrt(head_sumsq(k) + RMS_EPS)
    v_ref[0] = conv_silu(xv_ref, pv_ref, nv_ref, 2)
    gab = gab_ref[...]
    t = gab + dtb_ref[...]
    softplus = jnp.maximum(t, 0.0) + jnp.log1p(jnp.exp(-jnp.abs(t)))
    lane = lax.broadcasted_iota(jnp.int32, (1, LANE), 1)
    gate_ref[0] = jnp.where(lane < 2 * GDN_H, -jnp.exp(al_ref[...]) * softplus,
                            jnp.where(lane < 4 * GDN_H, jax.nn.sigmoid(gab), 0.0))


def gdn_pre_scan_inputs(proj_main, gab, conv_w, a_log, dt_bias, B, n, m):
    T = proj_main.shape[0]
    R = MERGE_TILE
    lat_tiles, ctx_tiles = n // R, m // R
    maps = dict(n_lat_tiles=B * lat_tiles, lat_tiles=lat_tiles, ctx_tiles=ctx_tiles)
    c0 = MAIN_OFF[2] // MIX_W
    halo = R // 8
    cur = [pl.BlockSpec((R, MIX_W), functools.partial(lambda i, c: (i, c), c=c0 + p)) for p in range(3)]
    prev = [pl.BlockSpec((8, MIX_W), functools.partial(lambda i, c: (jnp.maximum(i * halo - 1, 0), c), c=c0 + p))
            for p in range(3)]
    nxt = [pl.BlockSpec((8, MIX_W), functools.partial(lambda i, c: (jnp.minimum((i + 1) * halo, T // 8 - 1), c),
                                                      c=c0 + p)) for p in range(3)]
    const2 = lambda i: (0, 0)
    out_rows = lambda i: _stream_to_scan_rows(i, **maps)
    pad16 = lambda t: jnp.pad(t.reshape(1, 2 * GDN_H), ((0, 0), (0, LANE - 2 * GDN_H)))
    head_sum = (jnp.arange(MIX_W)[:, None] // HEAD_DIM == jnp.arange(MIX_W)[None, :] // HEAD_DIM).astype(jnp.bfloat16)
    tok = jax.ShapeDtypeStruct((B, m + n, MIX_W), jnp.float32)
    return pl.pallas_call(
        functools.partial(_gdn_pre_kernel, **maps),
        grid=(T // R,),
        in_specs=cur + prev + nxt + [pl.BlockSpec((R, LANE), lambda i: (i, 0)),
                                     pl.BlockSpec((GDN_CONV, 3 * MIX_W), const2),
                                     pl.BlockSpec((1, LANE), const2), pl.BlockSpec((1, LANE), const2),
                                     pl.BlockSpec((MIX_W, MIX_W), const2)],
        out_specs=[pl.BlockSpec((1, R, MIX_W), out_rows)] * 3 + [pl.BlockSpec((1, R, LANE), out_rows)],
        out_shape=[tok, tok, tok, jax.ShapeDtypeStruct((B, m + n, LANE), jnp.float32)],
        compiler_params=pltpu.CompilerParams(dimension_semantics=("arbitrary",), vmem_limit_bytes=VMEM_LIMIT),
        name="gdn_pre",
    )(*([proj_main] * 9), gab, conv_w.T, pad16(a_log), pad16(dt_bias), head_sum)


def gdn_branch(proj_main, gab, conv_w, a_log, dt_bias, B, n, m):
    q, k, v, gates = gdn_pre_scan_inputs(proj_main, gab, conv_w, a_log, dt_bias, B, n, m)
    return gdn_scan(q, k, v, gates, m // GDN_CHUNK)


FFT_R = 128
FFT_COLS = 4096
FFT_K1_STEP = 2


def _dft_tables(L):
    R = FFT_R
    N = R * R
    half = L // R
    idx = jnp.arange(R, dtype=jnp.int32)
    ang1 = (-2.0 * math.pi / R) * ((idx[:, None] * idx[None, :]) % R).astype(jnp.float32)
    fr, fi = jnp.cos(ang1), jnp.sin(ang1)
    blk = lambda re, im: jnp.concatenate([jnp.concatenate([re, -im], axis=1),
                                          jnp.concatenate([im, re], axis=1)], axis=0)
    m1 = blk(fr[:, :half], fi[:, :half])
    m1_real = jnp.concatenate([fr, fi], axis=0)
    m3 = blk(fr.T[:half], -fi.T[:half]) * (1.0 / N)
    k = idx[:, None, None] + R * idx[None, :, None]
    ang2 = (-2.0 * math.pi / N) * ((idx[None, None, :] * k) % N).astype(jnp.float32)
    gr, gi = jnp.cos(ang2), jnp.sin(ang2)
    g_fwd = jnp.concatenate([jnp.concatenate([gr, -gi], axis=2),
                             jnp.concatenate([gi, gr], axis=2)], axis=1)
    g_inv = jnp.swapaxes(g_fwd, 1, 2)
    bf = lambda t: t.astype(jnp.bfloat16)
    return bf(m1), bf(m1_real), bf(m3), bf(g_fwd), bf(g_inv)


def _colmm_kernel(m_ref, x_ref, o_ref):
    o_ref[...] = jnp.dot(m_ref[...], x_ref[...].astype(jnp.bfloat16),
                         preferred_element_type=jnp.float32).astype(o_ref.dtype)


def colmm(mat, x, out_dtype=jnp.bfloat16):
    M, K = mat.shape
    n_cols = x.shape[1]
    return pl.pallas_call(
        _colmm_kernel,
        grid=(n_cols // FFT_COLS,),
        in_specs=[pl.BlockSpec((M, K), lambda j: (0, 0)), pl.BlockSpec((K, FFT_COLS), lambda j: (0, j))],
        out_specs=pl.BlockSpec((M, FFT_COLS), lambda j: (0, j)),
        out_shape=jax.ShapeDtypeStruct((M, n_cols), out_dtype),
        compiler_params=pltpu.CompilerParams(dimension_semantics=("arbitrary",), vmem_limit_bytes=VMEM_LIMIT),
        name="hyena_colmm",
    )(mat, x)


def _colmm_gate_kernel(m_ref, r_ref, z_ref, x_ref, skip_ref, o_ref):
    y = jnp.dot(m_ref[...], r_ref[...], preferred_element_type=jnp.float32)
    o_ref[...] = x_ref[...] * (y + z_ref[...] * skip_ref[...])


def colmm_gate(mat, r, z, gate, skip_cols):
    M, K = mat.shape
    n_cols = r.shape[1]
    col = lambda j: (0, j)
    return pl.pallas_call(
        _colmm_gate_kernel,
        grid=(n_cols // FFT_COLS,),
        in_specs=[pl.BlockSpec((M, K), lambda j: (0, 0)), pl.BlockSpec((K, FFT_COLS), col),
                  pl.BlockSpec((M, FFT_COLS), col), pl.BlockSpec((M, FFT_COLS), col),
                  pl.BlockSpec((1, FFT_COLS), col)],
        out_specs=pl.BlockSpec((M, FFT_COLS), col),
        out_shape=jax.ShapeDtypeStruct((M, n_cols), jnp.float32),
        compiler_params=pltpu.CompilerParams(dimension_semantics=("arbitrary",), vmem_limit_bytes=VMEM_LIMIT),
        name="hyena_colmm_gate",
    )(mat, r, z, gate, skip_cols)


def _spectrum_kernel(p_ref, g_ref, o_ref):
    R = FFT_R
    for j in range(FFT_K1_STEP):
        p = jnp.concatenate([p_ref[0, j], p_ref[1, j]], axis=0)
        q = jnp.dot(g_ref[j], p, preferred_element_type=jnp.float32)
        o_ref[0, j] = q[:R]
        o_ref[1, j] = q[R:]


def _freq_kernel(p_ref, g_ref, gi_ref, h_ref, o_ref):
    R = FFT_R
    for j in range(FFT_K1_STEP):
        p = jnp.concatenate([p_ref[0, j], p_ref[1, j]], axis=0)
        q = jnp.dot(g_ref[j], p, preferred_element_type=jnp.float32)
        qr, qi = q[:R], q[R:]
        hr, hi = h_ref[0, j], h_ref[1, j]
        y = jnp.concatenate([qr * hr - qi * hi, qr * hi + qi * hr], axis=0).astype(jnp.bfloat16)
        r = jnp.dot(gi_ref[j], y, preferred_element_type=jnp.float32).astype(o_ref.dtype)
        o_ref[0, j] = r[:R]
        o_ref[1, j] = r[R:]


def hyena_spectrum(p, g_fwd):
    C = p.shape[-1]
    R, S = FFT_R, FFT_K1_STEP
    blk = pl.BlockSpec((2, S, R, LANE * 4), lambda i, c: (0, i, 0, c))
    return pl.pallas_call(
        _spectrum_kernel,
        grid=(R // S, C // (LANE * 4)),
        in_specs=[blk, pl.BlockSpec((S, 2 * R, 2 * R), lambda i, c: (i, 0, 0))],
        out_specs=blk,
        out_shape=jax.ShapeDtypeStruct((2, R, R, C), jnp.float32),
        compiler_params=pltpu.CompilerParams(dimension_semantics=("arbitrary",) * 2, vmem_limit_bytes=VMEM_LIMIT),
        name="hyena_spectrum",
    )(p, g_fwd)


def hyena_freq(p, g_fwd, g_inv, spec, order):
    C = p.shape[-1]
    R, S = FFT_R, FFT_K1_STEP
    blk = pl.BlockSpec((2, S, R, C), lambda i: (0, i, 0, 0))
    gspec = pl.BlockSpec((S, 2 * R, 2 * R), lambda i: (i, 0, 0))
    return pl.pallas_call(
        _freq_kernel,
        grid=(R // S,),
        in_specs=[blk, gspec, gspec, pl.BlockSpec((2, S, R, C), lambda i: (0, i, 0, order))],
        out_specs=blk,
        out_shape=jax.ShapeDtypeStruct((2, R, R, C), jnp.bfloat16),
        compiler_params=pltpu.CompilerParams(dimension_semantics=("arbitrary",), vmem_limit_bytes=VMEM_LIMIT),
        name="hyena_freq",
    )(p, g_fwd, g_inv, spec)


def hyena_taps(L, w1, b1, w2, b2, w3, freq, deltas):
    f32 = jnp.float32
    pos = jnp.arange(L, dtype=f32)
    pos_b = L - pos
    bands = jnp.linspace(1e-4, HY_BANDS - 1, HY_BANDS, dtype=f32)[None, :]

    def mlp(j, w3_dir, delta_dir):
        t = (j / (L - 1))[:, None]
        ang = (2.0 * math.pi / L) * j[:, None]
        feats = jnp.concatenate([t, jnp.cos(bands * ang), -jnp.sin(bands * ang)], axis=-1)
        h = jnp.sin(freq * (feats @ w1 + b1))
        h = jnp.sin(freq * (h @ w2 + b2))
        h = h @ w3_dir
        return h * (jnp.exp(-t * jnp.abs(delta_dir).reshape(1, -1)) + HY_MOD_SHIFT)

    w3d = w3.reshape(w3.shape[0], 2, HY_ORDER * HY_W)
    h_f = mlp(pos, w3d[:, 0], deltas[0])
    h_b = mlp(pos_b, w3d[:, 1], deltas[1])
    h_b = jnp.where((pos > 0)[:, None], h_b, 0.0)
    taps = jnp.concatenate([h_f, h_b], axis=0)
    return taps / jnp.sum(jnp.abs(taps), axis=0, keepdims=True)


def hyena_latent(proj, conv_w, skip, filt, tables):
    B, L, _ = proj.shape
    assert B == 2 and L % FFT_R == 0 and 2 * L == FFT_R * FFT_R
    R, C = FFT_R, HY_W
    m1, m1_real, m3, g_fwd, g_inv = tables
    taps = hyena_taps(L, *filt)
    spec = hyena_spectrum(colmm(m1_real, taps.reshape(R, R * HY_ORDER * C)).reshape(2, R, R, HY_ORDER * C), g_fwd)
    u = centred_dwconv(proj, conv_w)
    v, x1, x2 = jnp.split(u, 3, axis=-1)
    rows = B * (L // R)
    z = v.reshape(rows, R * C)
    for o, gate in enumerate((x1, x2)):
        p = colmm(m1, z).reshape(2, R, R, C)
        r = hyena_freq(p, g_fwd, g_inv, spec, o).reshape(2 * R, R * C)
        z = colmm_gate(m3, r, z, gate.reshape(rows, R * C), jnp.tile(skip[o], R).reshape(1, R * C))
    return z.reshape(B, L, C)


def _combine_ln_kernel(dest_ref, x_ref, w_ref, mod_ref, lng_ref, lnb_ref, yb_ref, o_ref, buf_ref, sem, *, alpha):
    R = x_ref.shape[0]

    def row_copy(k, r):
        row = dest_ref[0, 0, k * R + r]
        return pltpu.make_async_copy(yb_ref.at[pl.ds(row, 1)], buf_ref.at[k, pl.ds(r, 1)], sem.at[k])

    def issue(r, carry):
        row_copy(0, r).start()
        row_copy(1, r).start()
        return carry

    def drain(r, carry):
        row_copy(0, r).wait()
        row_copy(1, r).wait()
        return carry

    lax.fori_loop(0, R, issue, 0, unroll=8)
    lax.fori_loop(0, R, drain, 0, unroll=8)
    y = w_ref[:, 0:1] * buf_ref[0] + w_ref[:, 1:2] * buf_ref[1]
    z = alpha * x_ref[...] + mod_ref[0, 5:6, :] * y
    o_ref[...] = _layer_norm_rows(z) * lng_ref[...] + lnb_ref[...]


def moe_combine_ln(x, yb, dest, wts, mod, ln_g, ln_b, alpha):
    T, D = x.shape
    R = MERGE_TILE
    dest_tiles = jnp.swapaxes(dest.reshape(T // R, R, TOP_K), 1, 2).reshape(T // R, 1, TOP_K * R)
    row = lambda i: (i, 0)
    return pl.pallas_call(
        functools.partial(_combine_ln_kernel, alpha=alpha),
        grid=(T // R,),
        in_specs=[pl.BlockSpec((1, 1, TOP_K * R), lambda i: (i, 0, 0), memory_space=pltpu.SMEM),
                  pl.BlockSpec((R, D), row),
                  pl.BlockSpec((R, TOP_K), row),
                  pl.BlockSpec((1, 6, D), lambda i: (i, 0, 0)),
                  pl.BlockSpec((1, D), lambda i: (0, 0)),
                  pl.BlockSpec((1, D), lambda i: (0, 0)),
                  pl.BlockSpec(memory_space=pl.ANY)],
        out_specs=pl.BlockSpec((R, D), row),
        out_shape=jax.ShapeDtypeStruct((T, D), jnp.float32),
        scratch_shapes=[pltpu.VMEM((TOP_K, R, D), jnp.float32), pltpu.SemaphoreType.DMA((TOP_K,))],
        compiler_params=pltpu.CompilerParams(dimension_semantics=("arbitrary",), vmem_limit_bytes=VMEM_LIMIT),
        name="moe_combine_ln",
    )(dest_tiles, x, wts.T, mod, ln_g.reshape(1, D), ln_b.reshape(1, D), yb)


def moe_dispatch(idx, n_tok):
    n_slot = n_tok * TOP_K
    e_flat = idx.T.reshape(-1)
    onehot = (e_flat[:, None] == jnp.arange(N_EXPERTS, dtype=jnp.int32)[None, :]).astype(jnp.int32)
    csum = jnp.cumsum(onehot, axis=0)
    rank = jnp.sum(csum * onehot, axis=1) - 1
    counts = csum[-1]
    padded = (counts + EXPERT_BLOCK - 1) // EXPERT_BLOCK * EXPERT_BLOCK
    pend = jnp.cumsum(padded)
    pstart = pend - padded
    dest = jnp.sum(onehot * pstart[None, :], axis=1) + rank
    n_blk = -(-n_slot // EXPERT_BLOCK) + N_EXPERTS
    cap = n_blk * EXPERT_BLOCK
    t_flat = jnp.arange(n_slot, dtype=jnp.int32) // TOP_K
    buf_tok = jnp.zeros((cap,), jnp.int32).at[dest].set(t_flat)
    blk_exp = jnp.minimum(jnp.searchsorted(pend, jnp.arange(n_blk, dtype=jnp.int32) * EXPERT_BLOCK, side='right'),
                          N_EXPERTS - 1).astype(jnp.int32)
    n_used = (pend[-1] // EXPERT_BLOCK).astype(jnp.int32).reshape(1)
    return buf_tok, dest, blk_exp, n_used


def rms_norm(x, w):
    return x * lax.rsqrt(jnp.mean(jnp.square(x), -1, keepdims=True) + RMS_EPS) * w


def centred_dwconv(u, w):
    K = w.shape[-1]
    T = u.shape[1]
    up = jnp.pad(u, ((0, 0), (K // 2, K // 2), (0, 0)))
    out = up[:, 0:T, :] * w[:, 0]
    for j in range(1, K):
        out = out + up[:, j:j + T, :] * w[:, j]
    return out


def hyena_filters(L, w1, b1, w2, b2, w3, freq, deltas):
    f32 = jnp.float32
    t = jnp.linspace(0.0, 1.0, L, dtype=f32)[:, None]
    ang = 2.0 * math.pi * jnp.arange(L, dtype=f32)[:, None] / L
    bands = jnp.linspace(1e-4, HY_BANDS - 1, HY_BANDS, dtype=f32)[None, :]
    feats = jnp.concatenate([t, jnp.cos(bands * ang), -jnp.sin(bands * ang)], axis=-1)
    h = jnp.sin(freq * (feats @ w1 + b1))
    h = jnp.sin(freq * (h @ w2 + b2))
    h = (h @ w3).reshape(L, 2, HY_ORDER, HY_W)
    window = jnp.exp(-t[:, :, None, None] * jnp.abs(deltas)) + HY_MOD_SHIFT
    h = h * window
    h_fwd, h_bwd = h[:, 0], h[:, 1]
    taps = jnp.concatenate([h_fwd, jnp.zeros_like(h_fwd[:1]), jnp.flip(h_bwd[1:], 0)], axis=0)
    taps = taps / jnp.sum(jnp.abs(taps), axis=0, keepdims=True)
    return jnp.fft.rfft(taps, axis=0)


def hyena_mix(proj, conv_w, skip, filt_f):
    L = proj.shape[1]
    u = centred_dwconv(proj, conv_w)
    v, x1, x2 = jnp.split(u, 3, axis=-1)
    z = v
    for o, gate in enumerate((x1, x2)):
        zf = jnp.fft.rfft(z, n=2 * L, axis=1)
        y = jnp.fft.irfft(zf * filt_f[:, o], n=2 * L, axis=1)[:, :L]
        z = gate * (y + z * skip[o])
    return z


def axial_rope_angles(n):
    t = jnp.arange(n)
    row = (t // GRID_W).astype(jnp.float32)
    col = (t % GRID_W).astype(jnp.float32)
    nf = HEAD_DIM // 4
    inv = ROPE_THETA ** (-jnp.arange(nf, dtype=jnp.float32) / nf)
    return row[:, None] * inv, col[:, None] * inv


def diff_attend(q, k, v, lam):
    s = jnp.einsum('bqhcd,bkhcd->bhcqk', q, k) * HEAD_DIM ** -0.5
    p = jax.nn.softmax(s, axis=-1)
    a = p[:, :, 0] - lam * p[:, :, 1]
    return jnp.einsum('bhqk,bkhd->bqhd', a, v)


def diff_finish(o, norm_w, lam_init):
    B, T = o.shape[:2]
    return (rms_norm(o, norm_w) * (1.0 - lam_init)).reshape(B, T, DIFF_H * DIFF_VD)


def split_diff(p):
    B, T, _ = p.shape
    q, k, v = jnp.split(p, 3, axis=-1)
    return (q.reshape(B, T, DIFF_H, 2, HEAD_DIM), k.reshape(B, T, DIFF_H, 2, HEAD_DIM),
            v.reshape(B, T, DIFF_H, DIFF_VD))


def dense_attend(q, k, v):
    s = jnp.einsum('bqhd,bkhd->bhqk', q, k) * HEAD_DIM ** -0.5
    p = jax.nn.softmax(s, axis=-1)
    return jnp.einsum('bhqk,bkhd->bqhd', p, v)


def _diff_pre_kernel(xq_ref, xk_ref, xv_ref, cos_ref, sin_ref, q_ref, k_ref, v_ref, *, n_lat_tiles, q_scale):
    is_lat = pl.program_id(0) < n_lat_tiles
    cos = jnp.tile(jnp.where(is_lat, cos_ref[...], 1.0), (1, DIFF_H))
    sin = jnp.tile(jnp.where(is_lat, sin_ref[...], 0.0), (1, DIFF_H))
    lane = lax.broadcasted_iota(jnp.int32, (1, MIX_W), 1)
    first_half = (lane % (HEAD_DIM // 2)) < HEAD_DIM // 4

    def rope(x):
        x = x.astype(jnp.float32)
        partner = jnp.where(first_half, pltpu.roll(x, MIX_W - HEAD_DIM // 4, axis=1),
                            pltpu.roll(x, HEAD_DIM // 4, axis=1))
        return x * cos + partner * sin

    q_ref[...] = (rope(xq_ref[...]) * q_scale).astype(q_ref.dtype)
    k_ref[0] = rope(xk_ref[...]).astype(k_ref.dtype)
    v_ref[0] = xv_ref[...]


def diff_pre(proj_main, cos_t, sin_t, B, n, m):
    T = proj_main.shape[0]
    R = MERGE_TILE
    lat_tiles, ctx_tiles = n // R, m // R
    n_lat_tiles = B * lat_tiles
    c0 = MAIN_OFF[4] // MIX_W
    cur = [pl.BlockSpec((R, MIX_W), functools.partial(lambda i, c: (i, c), c=c0 + p)) for p in range(3)]
    table = pl.BlockSpec((R, DIFF_VD), lambda i: (jnp.where(i < n_lat_tiles, i % lat_tiles, 0), 0))

    def key_rows(i):
        ic = i - n_lat_tiles
        return (jnp.where(i < n_lat_tiles, i // lat_tiles, ic // ctx_tiles),
                jnp.where(i < n_lat_tiles, i % lat_tiles, lat_tiles + ic % ctx_tiles), 0)

    kv = jax.ShapeDtypeStruct((B, n + m, MIX_W), jnp.bfloat16)
    return pl.pallas_call(
        functools.partial(_diff_pre_kernel, n_lat_tiles=n_lat_tiles,
                          q_scale=HEAD_DIM ** -0.5 * math.log2(math.e)),
        grid=(T // R,),
        in_specs=cur + [table, table],
        out_specs=[pl.BlockSpec((R, MIX_W), lambda i: (i, 0)),
                   pl.BlockSpec((1, R, MIX_W), key_rows), pl.BlockSpec((1, R, MIX_W), key_rows)],
        out_shape=[jax.ShapeDtypeStruct((T, MIX_W), jnp.bfloat16), kv, kv],
        compiler_params=pltpu.CompilerParams(dimension_semantics=("arbitrary",), vmem_limit_bytes=VMEM_LIMIT),
        name="diff_pre",
    )(proj_main, proj_main, proj_main, cos_t, sin_t)


def rope_tables(ang_r, ang_c):
    cos = jnp.concatenate([jnp.cos(ang_r)] * 2 + [jnp.cos(ang_c)] * 2, axis=-1)
    sin = jnp.concatenate([-jnp.sin(ang_r), jnp.sin(ang_r), -jnp.sin(ang_c), jnp.sin(ang_c)], axis=-1)
    return jnp.tile(cos, (1, 2)), jnp.tile(sin, (1, 2))


def diff_branch(proj_main, cos_t, sin_t, lam, norm_w, lam_init, B, n, m):
    q, k_all, v_all = diff_pre(proj_main, cos_t, sin_t, B, n, m)
    nk = n + m
    k_all = jnp.transpose(k_all.reshape(B, nk, DIFF_H, 2, HEAD_DIM), (0, 2, 3, 1, 4))
    vt_all = jnp.transpose(v_all.reshape(B, nk // DIFF_TK, DIFF_TK, DIFF_H, DIFF_VD), (0, 3, 1, 4, 2))
    return diff_attention(q[:B * n].reshape(B, n, MIX_W), k_all, vt_all, lam, norm_w, 1.0 - lam_init)


def _split_w_in(w):
    segs = jnp.split(w, SPLIT_IDX, axis=-1)
    ab = jnp.concatenate([segs[3], segs[4]], axis=-1)
    ab = jnp.pad(ab, ((0, 0), (0, LANE - ab.shape[-1])))
    main = jnp.concatenate([segs[7], segs[0], segs[1], segs[2], segs[5], segs[6]], axis=-1)
    return main.astype(jnp.bfloat16), ab.astype(jnp.bfloat16)


MAIN_SIZES = (N_BRANCH * D_MODEL, 3 * HY_W, 3 * MIX_W, MIX_W, 3 * MIX_W, 3 * MIX_W)
MAIN_OFF = tuple(sum(MAIN_SIZES[:i]) for i in range(len(MAIN_SIZES) + 1))
IN_TN = 768


def kernel(x, c, ctx, c_ctx, w_mod, b_mod, w_in, hy_conv, hy_w1, hy_b1, hy_w2, hy_b2, hy_w3, hy_freq,
           hy_deltas, hy_skip, gdn_conv, gdn_a_log, gdn_dt_bias, gdn_norm, diff_lam, diff_norm, na_rpb,
           branch_proj, w_out, ln_g, ln_b, router_w, router_b, exp_w1, exp_w3, exp_w2):
    B, n, D = x.shape
    m = ctx.shape[1]
    depth = w_mod.shape[0]
    dn_alpha = (2 * depth) ** 0.25
    ang_r, ang_c = axial_rope_angles(n)
    bf16 = jnp.bfloat16
    n_lat = B * n
    dft_tables = _dft_tables(n)
    cos_t, sin_t = rope_tables(ang_r, ang_c)
    tok = jnp.concatenate([x.reshape(n_lat, D), ctx.reshape(B * m, D)], axis=0)
    for l in range(depth):
        ctx_out = l < depth - 1
        mx = jax.nn.silu(c) @ w_mod[l] + b_mod[l]
        mc = jax.nn.silu(c_ctx) @ w_mod[l] + b_mod[l]
        one_plus = jnp.array([0.0, 1.0, 0.0, 0.0, 1.0, 0.0], jnp.float32)[:, None]
        mod_x = mx.reshape(B, 6, D) + one_plus
        mod_c = mc.reshape(1, 6, D) + one_plus
        mod = jnp.concatenate([jnp.repeat(mod_x, n // MERGE_TILE, axis=0),
                               jnp.repeat(mod_c, B * m // MERGE_TILE, axis=0)], axis=0)

        proj, gab = ln_mod_matmul(tok, mod, *_split_w_in(w_in[l]), IN_TN)

        def seg(i, latent):
            rows = slice(0, n_lat) if latent else slice(n_lat, None)
            return proj[rows, MAIN_OFF[i]:MAIN_OFF[i + 1]].reshape(B, -1, MAIN_SIZES[i]).astype(jnp.float32)

        hy_x, gqkv_x, gz_x, dqkv_x, nqkv_x = [seg(i, True) for i in range(1, 6)]
        hy_c, gqkv_c, gz_c, dqkv_c, nqkv_c = [seg(i, False) for i in range(1, 6)]
        gab_x, gab_c = gab[:n_lat].reshape(B, n, LANE), gab[n_lat:].reshape(B, m, LANE)
        ga_x, gb_x = gab_x[..., :2 * GDN_H], gab_x[..., 2 * GDN_H:4 * GDN_H]
        ga_c, gb_c = gab_c[..., :2 * GDN_H], gab_c[..., 2 * GDN_H:4 * GDN_H]
        filt = (hy_w1[l], hy_b1[l], hy_w2[l], hy_b2[l], hy_w3[l], hy_freq[l], hy_deltas[l])

        ya_x = hyena_latent(hy_x, hy_conv[l], hy_skip[l], filt, dft_tables)

        o_f, o_b = gdn_branch(proj, gab, gdn_conv[l], gdn_a_log[l], gdn_dt_bias[l], B, n, m)

        lq1, lk1, lq2, lk2 = diff_lam[l]
        lam_init = 0.8 - 0.6 * math.exp(-0.3 * l)
        lam = jnp.exp(jnp.sum(lq1 * lk1)) - jnp.exp(jnp.sum(lq2 * lk2)) + lam_init
        yc_x = diff_branch(proj, cos_t, sin_t, lam, diff_norm[l], lam_init, B, n, m)
        yd_x = na_attention(proj, na_bias_tables(na_rpb[l], n // GRID_W), B, n, m)

        ys = [t.reshape(n_lat, MIX_W) for t in (ya_x, yc_x, yd_x)]
        if ctx_out:
            dq_c, dk_c, dv_c = split_diff(dqkv_c)
            nq_c, nk_c, nv_c = [t.reshape(B, m, NA_H, HEAD_DIM) for t in jnp.split(nqkv_c, 3, axis=-1)]
            ya_c = hyena_mix(hy_c, hy_conv[l], hy_skip[l], hyena_filters(m, *filt))
            yc_c = diff_finish(diff_attend(dq_c, dk_c, dv_c, lam), diff_norm[l], lam_init)
            yd_c = dense_attend(nq_c, nk_c, nv_c)
            ys_c = [t.reshape(B * m, MIX_W) for t in (ya_c, yc_c, yd_c)]
            ys = [jnp.concatenate([a, b], axis=0) for a, b in zip(ys, ys_c)]
        else:
            tok = tok[:n_lat]
        tok = merge_residual_ln(tok, ys[0], o_f, o_b, ys[1], ys[2], proj, mod, branch_proj[l].astype(bf16),
                                w_out[l].astype(bf16), gdn_norm[l], ln_g[l, 0], ln_b[l, 0], dn_alpha, n, m)

        n_tok = tok.shape[0]
        h2, idx, wts = moe_route(tok, mod, router_w, router_b)
        buf_tok, dest, blk_exp, n_used = moe_dispatch(idx, n_tok)
        yb = expert_ffn(blk_exp, n_used, jnp.take(h2, buf_tok, axis=0), exp_w1[l], exp_w3[l], exp_w2[l])
        tok = moe_combine_ln(tok, yb, dest.reshape(n_tok, TOP_K), wts, mod, ln_g[l, 1], ln_b[l, 1], dn_alpha)
    return tok[:n_lat].reshape(B, n, D)
```

```python
import functools
import math

import jax
import jax.numpy as jnp
from jax import lax
from jax.experimental import pallas as pl
from jax.experimental.pallas import tpu as pltpu

D_MODEL = 1024
GRID_W = 64
HEAD_DIM = 64
MIX_W = D_MODEL // 2
N_BRANCH = 4
HY_W = MIX_W
HY_ORDER = 2
HY_BANDS = 16
HY_MOD_SHIFT = 0.05
GDN_H = MIX_W // HEAD_DIM
GDN_CHUNK = 64
GDN_CONV = 3
DIFF_VD = 2 * HEAD_DIM
DIFF_H = MIX_W // DIFF_VD
NA_H = MIX_W // HEAD_DIM
NA_ROWS = 8
NA_COLS = 16
N_EXPERTS = 16
N_GROUPS = 4
TOP_K = 2
ROPE_THETA = 10000.0
LN_EPS = 1e-5
RMS_EPS = 1e-6
SPLIT_SIZES = (3 * HY_W, 3 * MIX_W, MIX_W, 2 * GDN_H, 2 * GDN_H, 3 * MIX_W, 3 * MIX_W, N_BRANCH * D_MODEL)
SPLIT_IDX = tuple(sum(SPLIT_SIZES[:i + 1]) for i in range(len(SPLIT_SIZES) - 1))

LANE = 128
ROW_TILE = 1536
MERGE_TILE = 256
VMEM_LIMIT = 48 * 1024 * 1024


def _ln_mod_matmul_kernel(x_ref, mod_ref, w_ref, wg_ref, o_ref, og_ref, h_ref):
    @pl.when(pl.program_id(1) == 0)
    def _():
        for s in range(mod_ref.shape[0]):
            rows = slice(s * MERGE_TILE, (s + 1) * MERGE_TILE)
            y = _layer_norm_rows(x_ref[rows, :])
            h = (y * mod_ref[s, 1:2, :] + mod_ref[s, 0:1, :]).astype(h_ref.dtype)
            h_ref[rows, :] = h
            og_ref[rows, :] = jnp.dot(h, wg_ref[...], preferred_element_type=jnp.float32)

    o_ref[...] = jnp.dot(h_ref[...], w_ref[...], preferred_element_type=jnp.float32).astype(o_ref.dtype)


def ln_mod_matmul(x, mod, w, w_gates, tn):
    T, D = x.shape
    N = w.shape[1]
    return pl.pallas_call(
        _ln_mod_matmul_kernel,
        grid=(T // ROW_TILE, N // tn),
        in_specs=[
            pl.BlockSpec((ROW_TILE, D), lambda i, j: (i, 0)),
            pl.BlockSpec((ROW_TILE // MERGE_TILE, 6, D), lambda i, j: (i, 0, 0)),
            pl.BlockSpec((D, tn), lambda i, j: (0, j)),
            pl.BlockSpec((D, LANE), lambda i, j: (0, 0)),
        ],
        out_specs=[pl.BlockSpec((ROW_TILE, tn), lambda i, j: (i, j)),
                   pl.BlockSpec((ROW_TILE, LANE), lambda i, j: (i, 0))],
        out_shape=[jax.ShapeDtypeStruct((T, N), jnp.bfloat16), jax.ShapeDtypeStruct((T, LANE), jnp.float32)],
        scratch_shapes=[pltpu.VMEM((ROW_TILE, D), jnp.bfloat16)],
        compiler_params=pltpu.CompilerParams(
            dimension_semantics=("arbitrary", "arbitrary"), vmem_limit_bytes=VMEM_LIMIT),
        name="ln_mod_matmul",
    )(x, mod, w, w_gates)


def _layer_norm_rows(x):
    mu = jnp.mean(x, axis=-1, keepdims=True)
    xc = x - mu
    var = jnp.mean(xc * xc, axis=-1, keepdims=True)
    return xc * lax.rsqrt(var + LN_EPS)


_NT_DIMS = (((1,), (1,)), ((), ()))


DIFF_TQ = 1024
DIFF_TK = 768
DIFF_ROWS = 32


def _diff_attn_kernel(lam_ref, q_ref, k_ref, vt_ref, nw_ref, o_ref, s_ref, *, out_scale):
    tq = q_ref.shape[1]
    n_tiles = k_ref.shape[3] // DIFF_TK
    q = q_ref[0]
    qc = (q[:, :HEAD_DIM], q[:, HEAD_DIM:])

    sub = 8
    fold = lambda t, op: op(t.reshape(t.shape[0] // sub, sub, t.shape[1]), axis=0)

    def scores(j, slot):
        start = pl.multiple_of(j * DIFF_TK, DIFF_TK)
        mx = []
        for c in range(2):
            k = k_ref[0, 0, c, pl.ds(start, DIFF_TK), :]
            st = lax.dot_general(k, qc[c], _NT_DIMS, preferred_element_type=jnp.float32)
            s_ref[slot, c] = st
            mx.append(fold(st, jnp.max))
        return tuple(mx)

    def softmax_pv(j, slot, mx, carry):
        new, ps = [], []
        for c in range(2):
            m_prev, l_prev, acc = carry[c]
            m_new = jnp.maximum(m_prev, jnp.max(mx[c], axis=0, keepdims=True))
            alpha = jnp.exp2(m_prev - m_new)
            psum, chunks = None, []
            for r in range(DIFF_TK // DIFF_ROWS):
                p = jnp.exp2(s_ref[slot, c, r * DIFF_ROWS:(r + 1) * DIFF_ROWS, :] - m_new)
                part = fold(p, jnp.sum)
                psum = part if psum is None else psum + part
                chunks.append(p.astype(jnp.bfloat16))
            new.append((m_new, alpha * l_prev + jnp.sum(psum, axis=0, keepdims=True), alpha * acc))
            ps.append(jnp.concatenate(chunks, axis=0))
        pv = jnp.dot(vt_ref[0, 0, j], jnp.concatenate(ps, axis=1), preferred_element_type=jnp.float32)
        return tuple((new[c][0], new[c][1], new[c][2] + pv[:, c * tq:(c + 1) * tq]) for c in range(2))

    def pair(jj, state):
        carry, mx0 = state
        j0 = 2 * jj
        mx1 = scores(j0 + 1, 1)
        carry = softmax_pv(j0, 0, mx0, carry)
        mx0 = scores(j0 + 2, 0)
        return softmax_pv(j0 + 1, 1, mx1, carry), mx0

    carry = tuple((jnp.full((1, tq), -jnp.inf, jnp.float32), jnp.zeros((1, tq), jnp.float32),
                   jnp.zeros((DIFF_VD, tq), jnp.float32)) for _ in range(2))
    mx0 = scores(0, 0)
    n_pairs = (n_tiles - 1) // 2
    carry, mx0 = lax.fori_loop(0, n_pairs, pair, (carry, mx0))
    if n_tiles % 2 == 0:
        mx1 = scores(n_tiles - 1, 1)
        carry = softmax_pv(n_tiles - 2, 0, mx0, carry)
        carry = softmax_pv(n_tiles - 1, 1, mx1, carry)
    else:
        carry = softmax_pv(n_tiles - 1, 0, mx0, carry)
    (_, l0, a0), (_, l1, a1) = carry
    o = a0 / l0 - lam_ref[0] * (a1 / l1)
    o = o * lax.rsqrt(jnp.mean(o * o, axis=0, keepdims=True) + RMS_EPS) * (nw_ref[...] * out_scale)
    o_ref[0] = o.T.astype(o_ref.dtype)


def diff_attention(q, k, vt, lam, norm_w, out_scale, out_dtype=jnp.float32):
    B, nq, W = q.shape
    nk = k.shape[3]
    assert nq % DIFF_TQ == 0 and nk % DIFF_TK == 0
    H = W // DIFF_VD
    return pl.pallas_call(
        functools.partial(_diff_attn_kernel, out_scale=out_scale),
        grid=(B, H, nq // DIFF_TQ),
        in_specs=[
            pl.BlockSpec(memory_space=pltpu.SMEM),
            pl.BlockSpec((1, DIFF_TQ, DIFF_VD), lambda b, h, i: (b, i, h)),
            pl.BlockSpec((1, 1, 2, nk, HEAD_DIM), lambda b, h, i: (b, h, 0, 0, 0)),
            pl.BlockSpec((1, 1, nk // DIFF_TK, DIFF_VD, DIFF_TK), lambda b, h, i: (b, h, 0, 0, 0)),
            pl.BlockSpec((DIFF_VD, 1), lambda b, h, i: (0, 0)),
        ],
        out_specs=pl.BlockSpec((1, DIFF_TQ, DIFF_VD), lambda b, h, i: (b, i, h)),
        out_shape=jax.ShapeDtypeStruct((B, nq, W), out_dtype),
        scratch_shapes=[pltpu.VMEM((2, 2, DIFF_TK, DIFF_TQ), jnp.float32)],
        compiler_params=pltpu.CompilerParams(
            dimension_semantics=("arbitrary",) * 3, vmem_limit_bytes=VMEM_LIMIT),
        name="diff_attention",
    )(lam.reshape(1).astype(jnp.float32), q, k, vt, norm_w.reshape(DIFF_VD, 1))


NA_QROWS = 8
NA_KROWS = 16
NA_NEG = -1e30


def _na_kernel(q_ref, k_ref, v_ref, kc_ref, vc_ref, tab_ref, o_ref, *, n_kblocks):
    g = pl.program_id(2)
    kb = (NA_KROWS // 4) * GRID_W
    base = jnp.clip(2 * g - 1, 0, n_kblocks - 4)
    start = pl.multiple_of(base * kb, kb)
    nwin = NA_KROWS * GRID_W
    q = q_ref[...] * HEAD_DIM ** -0.5
    kw = k_ref[pl.ds(start, nwin), :]
    vw = v_ref[pl.ds(start, nwin), :]
    kc = kc_ref[...]
    vc = vc_ref[...]
    outs = []
    for hh in range(2):
        sl = slice(hh * HEAD_DIM, (hh + 1) * HEAD_DIM)
        qh = q[:, sl]
        s = lax.dot_general(qh, kw[:, sl], _NT_DIMS, preferred_element_type=jnp.float32) + tab_ref[0, hh].astype(jnp.float32)
        sc = lax.dot_general(qh, kc[:, sl], _NT_DIMS, preferred_element_type=jnp.float32)
        m = jnp.maximum(jnp.max(s, axis=-1, keepdims=True), jnp.max(sc, axis=-1, keepdims=True))
        p = jnp.exp(s - m)
        pc = jnp.exp(sc - m)
        l = jnp.sum(p, axis=-1, keepdims=True) + jnp.sum(pc, axis=-1, keepdims=True)
        o = (jnp.dot(p.astype(vw.dtype), vw[:, sl], preferred_element_type=jnp.float32)
             + jnp.dot(pc.astype(vc.dtype), vc[:, sl], preferred_element_type=jnp.float32))
        outs.append(o / l)
    o_ref[...] = jnp.concatenate(outs, axis=-1).astype(o_ref.dtype)


def na_bias_tables(rpb, R):
    H = rpb.shape[0]
    G = R // NA_QROWS
    cols = jnp.arange(GRID_W)
    cstart = jnp.clip(cols - NA_COLS // 2, 0, GRID_W - NA_COLS)
    col_ok = (cols[None, :] >= cstart[:, None]) & (cols[None, :] < cstart[:, None] + NA_COLS)
    dc = jnp.clip(cols[None, :] - cols[:, None], -(NA_COLS - 1), NA_COLS - 1) + (NA_COLS - 1)
    rpb_c = rpb[:, :, dc]
    tabs = []
    for g in (0, 1, G - 1):
        base = min(max(2 * g - 1, 0), R // 4 - 4)
        r = NA_QROWS * g + jnp.arange(NA_QROWS)
        rs = jnp.clip(r - NA_ROWS // 2, 0, R - NA_ROWS)
        kr = 4 * base + jnp.arange(NA_KROWS)
        valid = (kr[None, :] >= rs[:, None]) & (kr[None, :] < rs[:, None] + NA_ROWS)
        dr = jnp.clip(kr[None, :] - r[:, None] + (NA_ROWS - 1), 0, 2 * NA_ROWS - 2)
        tab = rpb_c[:, dr]
        ok = valid[None, :, :, None, None] & col_ok[None, None, None, :, :]
        tab = jnp.where(ok, tab, NA_NEG)
        tabs.append(jnp.transpose(tab, (0, 1, 3, 2, 4)).reshape(H, NA_QROWS * GRID_W, NA_KROWS * GRID_W))
    return jnp.stack(tabs, axis=0).astype(jnp.bfloat16)


def na_attention(proj_main, tabs, B, n, m, out_dtype=jnp.float32):
    W = MIX_W
    tq = NA_QROWS * GRID_W
    G = n // tq
    n_kblocks = n // ((NA_KROWS // 4) * GRID_W)
    cq, ck, cv = [(MAIN_OFF[5] + p * W) // LANE for p in range(3)]
    ctx0 = B * n // m

    def tab_index(b, hp, g):
        return (jnp.where(g == 0, 0, jnp.where(g == G - 1, 2, 1)), hp, 0, 0)

    return pl.pallas_call(
        functools.partial(_na_kernel, n_kblocks=n_kblocks),
        grid=(B, W // LANE, G),
        in_specs=[
            pl.BlockSpec((tq, LANE), lambda b, hp, g: (b * G + g, cq + hp)),
            pl.BlockSpec((n, LANE), lambda b, hp, g: (b, ck + hp)),
            pl.BlockSpec((n, LANE), lambda b, hp, g: (b, cv + hp)),
            pl.BlockSpec((m, LANE), lambda b, hp, g: (ctx0 + b, ck + hp)),
            pl.BlockSpec((m, LANE), lambda b, hp, g: (ctx0 + b, cv + hp)),
            pl.BlockSpec((1, 2, tq, NA_KROWS * GRID_W), tab_index),
        ],
        out_specs=pl.BlockSpec((tq, LANE), lambda b, hp, g: (b * G + g, hp)),
        out_shape=jax.ShapeDtypeStruct((B * n, W), out_dtype),
        compiler_params=pltpu.CompilerParams(
            dimension_semantics=("arbitrary",) * 3, vmem_limit_bytes=VMEM_LIMIT),
        name="na_attention",
    )(proj_main, proj_main, proj_main, proj_main, proj_main, tabs)


def _merge_kernel(x_ref, ya_ref, of_ref, ob_ref, yc_ref, yd_ref, gc_ref, gz_ref, mod_ref, p_ref, wo_ref,
                  gn_ref, hs_ref, lng_ref, lnb_ref, o_ref, *, alpha):
    o = of_ref[0] + ob_ref[0]
    sq = o * o
    hi = sq.astype(jnp.bfloat16)
    lo = (sq - hi.astype(jnp.float32)).astype(jnp.bfloat16)
    ms = (jnp.dot(hi, hs_ref[...], preferred_element_type=jnp.float32)
          + jnp.dot(lo, hs_ref[...], preferred_element_type=jnp.float32)) * (1.0 / HEAD_DIM)
    gz = gz_ref[...].astype(jnp.float32)
    yb = o * lax.rsqrt(ms + RMS_EPS) * gn_ref[...] * (gz * jax.nn.sigmoid(gz))
    acc = None
    for mi, y in enumerate((ya_ref[...], yb, yc_ref[...], yd_ref[...])):
        br = jnp.dot(y.astype(jnp.bfloat16), p_ref[mi], preferred_element_type=jnp.float32)
        gm = jax.nn.sigmoid(gc_ref[:, mi * D_MODEL:(mi + 1) * D_MODEL].astype(jnp.float32))
        acc = gm * br if acc is None else acc + gm * br
    mix = jnp.dot(acc.astype(jnp.bfloat16), wo_ref[...], preferred_element_type=jnp.float32)
    z = alpha * x_ref[...] + mod_ref[0, 2:3, :] * mix
    o_ref[...] = _layer_norm_rows(z) * lng_ref[...] + lnb_ref[...]


def merge_residual_ln(x, ya, o_f, o_b, yc, yd, proj_main, mod, proj, w_o, gdn_norm, ln_g, ln_b, alpha, n, m):
    T, D = x.shape
    R = MERGE_TILE
    lat_tiles, ctx_tiles = n // R, m // R
    scan_rows = functools.partial(_stream_to_scan_rows, n_lat_tiles=o_f.shape[0] * lat_tiles,
                                  lat_tiles=lat_tiles, ctx_tiles=ctx_tiles)
    row = lambda i: (i, 0)
    const2 = lambda i: (0, 0)
    branch = pl.BlockSpec((R, MIX_W), row)
    scan = pl.BlockSpec((1, R, MIX_W), scan_rows)
    head_sum = (jnp.arange(MIX_W)[:, None] // HEAD_DIM == jnp.arange(MIX_W)[None, :] // HEAD_DIM).astype(jnp.bfloat16)
    return pl.pallas_call(
        functools.partial(_merge_kernel, alpha=alpha),
        grid=(T // R,),
        in_specs=[pl.BlockSpec((R, D), row), branch, scan, scan, branch, branch,
                  pl.BlockSpec((R, N_BRANCH * D), row),
                  pl.BlockSpec((R, MIX_W), lambda i: (i, MAIN_OFF[3] // MIX_W)),
                  pl.BlockSpec((1, 6, D), lambda i: (i, 0, 0)),
                  pl.BlockSpec((N_BRANCH, MIX_W, D), lambda i: (0, 0, 0)),
                  pl.BlockSpec((D, D), const2),
                  pl.BlockSpec((1, MIX_W), const2),
                  pl.BlockSpec((MIX_W, MIX_W), const2),
                  pl.BlockSpec((1, D), const2),
                  pl.BlockSpec((1, D), const2)],
        out_specs=pl.BlockSpec((R, D), row),
        out_shape=jax.ShapeDtypeStruct((T, D), jnp.float32),
        compiler_params=pltpu.CompilerParams(dimension_semantics=("arbitrary",), vmem_limit_bytes=VMEM_LIMIT),
        name="merge_residual_ln",
    )(x, ya, o_f, o_b, yc, yd, proj_main, proj_main, mod, proj, w_o,
      jnp.tile(gdn_norm, GDN_H).reshape(1, MIX_W), head_sum, ln_g.reshape(1, D), ln_b.reshape(1, D))


def _route_kernel(x_ref, mod_ref, rw_ref, rb_ref, h_ref, idx_ref, wts_ref):
    h = (_layer_norm_rows(x_ref[...]) * mod_ref[0, 4:5, :] + mod_ref[0, 3:4, :]).astype(jnp.bfloat16)
    h_ref[...] = h.astype(h_ref.dtype)
    logits = lax.dot_general(rw_ref[...], h, _NT_DIMS, preferred_element_type=jnp.float32)
    s = jax.nn.sigmoid(logits)
    sel = s + rb_ref[...]
    per = N_EXPERTS // N_GROUPS
    srow = [s[e:e + 1, :] for e in range(N_EXPERTS)]
    vrow = [sel[e:e + 1, :] for e in range(N_EXPERTS)]
    best = None
    for gi in range(N_GROUPS):
        grp = vrow[gi * per:(gi + 1) * per]
        gs = None
        for a in range(per):
            for b in range(a + 1, per):
                ps = grp[a] + grp[b]
                gs = ps if gs is None else jnp.maximum(gs, ps)
        if best is None:
            best, bg = gs, jnp.zeros(gs.shape, jnp.int32)
        else:
            upd = gs > best
            bg = jnp.where(upd, gi, bg)
            best = jnp.where(upd, gs, best)
    cv, cs = [], []
    for j in range(per):
        v_j, s_j = vrow[j], srow[j]
        for gi in range(1, N_GROUPS):
            v_j = jnp.where(bg == gi, vrow[gi * per + j], v_j)
            s_j = jnp.where(bg == gi, srow[gi * per + j], s_j)
        cv.append(v_j)
        cs.append(s_j)

    def first_argmax(vals):
        bv, bi = vals[0], jnp.zeros(vals[0].shape, jnp.int32)
        for j in range(1, per):
            upd = vals[j] > bv
            bi = jnp.where(upd, j, bi)
            bv = jnp.where(upd, vals[j], bv)
        return bi

    i1 = first_argmax(cv)
    i2 = first_argmax([jnp.where(i1 == j, -jnp.inf, cv[j]) for j in range(per)])
    w1 = cs[0]
    w2 = cs[0]
    for j in range(1, per):
        w1 = jnp.where(i1 == j, cs[j], w1)
        w2 = jnp.where(i2 == j, cs[j], w2)
    tot = w1 + w2
    idx_ref[...] = jnp.concatenate([bg * per + i1, bg * per + i2], axis=0)
    wts_ref[...] = jnp.concatenate([w1 / tot, w2 / tot], axis=0)


def moe_route(x, mod, router_w, router_b):
    T, D = x.shape
    return pl.pallas_call(
        _route_kernel,
        grid=(T // MERGE_TILE,),
        in_specs=[pl.BlockSpec((MERGE_TILE, D), lambda i: (i, 0)),
                  pl.BlockSpec((1, 6, D), lambda i: (i, 0, 0)),
                  pl.BlockSpec((N_EXPERTS, D), lambda i: (0, 0)),
                  pl.BlockSpec((N_EXPERTS, 1), lambda i: (0, 0))],
        out_specs=[pl.BlockSpec((MERGE_TILE, D), lambda i: (i, 0)),
                   pl.BlockSpec((TOP_K, MERGE_TILE), lambda i: (0, i)),
                   pl.BlockSpec((TOP_K, MERGE_TILE), lambda i: (0, i))],
        out_shape=[jax.ShapeDtypeStruct((T, D), jnp.float32),
                   jax.ShapeDtypeStruct((TOP_K, T), jnp.int32),
                   jax.ShapeDtypeStruct((TOP_K, T), jnp.float32)],
        compiler_params=pltpu.CompilerParams(dimension_semantics=("arbitrary",), vmem_limit_bytes=VMEM_LIMIT),
        name="moe_route",
    )(x, mod, router_w.T.astype(jnp.bfloat16), router_b.reshape(N_EXPERTS, 1).astype(jnp.float32))


EXPERT_BLOCK = 256


def _expert_kernel(be_ref, nu_ref, x_ref, w1_ref, w3_ref, w2_ref, o_ref):
    i = pl.program_id(0)

    @pl.when(i < nu_ref[0])
    def _():
        x = x_ref[...].astype(jnp.bfloat16)
        bf = lambda ref: ref[0].astype(jnp.bfloat16)
        a = jnp.dot(x, bf(w1_ref), preferred_element_type=jnp.float32)
        b = jnp.dot(x, bf(w3_ref), preferred_element_type=jnp.float32)
        hmid = (a * jax.nn.sigmoid(a) * b).astype(jnp.bfloat16)
        o_ref[...] = jnp.dot(hmid, bf(w2_ref), preferred_element_type=jnp.float32).astype(o_ref.dtype)

    @pl.when(i >= nu_ref[0])
    def _():
        o_ref[...] = jnp.zeros(o_ref.shape, o_ref.dtype)


def expert_ffn(blk_exp, n_used, xb, w1, w3, w2, out_dtype=jnp.float32):
    cap, D = xb.shape
    F = w1.shape[-1]
    n_blk = cap // EXPERT_BLOCK
    return pl.pallas_call(
        _expert_kernel,
        grid_spec=pltpu.PrefetchScalarGridSpec(
            num_scalar_prefetch=2,
            grid=(n_blk,),
            in_specs=[pl.BlockSpec((EXPERT_BLOCK, D), lambda i, be, nu: (i, 0)),
                      pl.BlockSpec((1, D, F), lambda i, be, nu: (be[i], 0, 0)),
                      pl.BlockSpec((1, D, F), lambda i, be, nu: (be[i], 0, 0)),
                      pl.BlockSpec((1, F, D), lambda i, be, nu: (be[i], 0, 0))],
            out_specs=pl.BlockSpec((EXPERT_BLOCK, D), lambda i, be, nu: (i, 0)),
        ),
        out_shape=jax.ShapeDtypeStruct((cap, D), out_dtype),
        compiler_params=pltpu.CompilerParams(dimension_semantics=("arbitrary",), vmem_limit_bytes=VMEM_LIMIT),
        name="expert_ffn",
    )(blk_exp, n_used, xb, w1, w3, w2)


def _gdn_kernel(qf_ref, kf_ref, vf_ref, gf_ref, qb_ref, kb_ref, vb_ref, gb_ref, of_ref, ob_ref, s_ref):
    C = GDN_CHUNK
    nb = qf_ref.shape[0]

    @pl.when(pl.program_id(0) == 0)
    def _():
        s_ref[...] = jnp.zeros(s_ref.shape, jnp.float32)

    bf = lambda t: t.astype(jnp.bfloat16)
    mm = lambda a, b: jnp.dot(bf(a), bf(b), preferred_element_type=jnp.float32)
    nt = lambda a, b: lax.dot_general(bf(a), bf(b), _NT_DIMS, preferred_element_type=jnp.float32)
    tn = lambda a, b: lax.dot_general(bf(a), bf(b), (((0,), (0,)), ((), ())), preferred_element_type=jnp.float32)
    col = lambda t, h: t[:, h:h + 1]
    rel = lax.broadcasted_iota(jnp.int32, (C, C), 0) - lax.broadcasted_iota(jnp.int32, (C, C), 1)

    qs, ks, vs, decay, beta_c, e_gc, e_rest, e_tot, strict = [], [], [], [], [], [], [], [], []
    for bi in range(nb):
        for d, refs in enumerate(((qf_ref, kf_ref, vf_ref, gf_ref), (qb_ref, kb_ref, vb_ref, gb_ref))):
            q_ref, k_ref, v_ref, g_ref = refs
            incl = rel >= 0 if d == 0 else rel <= 0
            g = g_ref[bi]
            gc = jnp.dot(incl.astype(jnp.float32), g, precision=lax.Precision.HIGHEST,
                         preferred_element_type=jnp.float32)
            gc_t = gc.T
            tot = jnp.sum(g, axis=0, keepdims=True)
            eg, er, et = jnp.exp(gc), jnp.exp(tot - gc), jnp.exp(tot)
            q, k, v = q_ref[bi], k_ref[bi], v_ref[bi]
            for h in range(GDN_H):
                sl = slice(h * HEAD_DIM, (h + 1) * HEAD_DIM)
                gl = d * GDN_H + h
                qs.append(q[:, sl])
                ks.append(k[:, sl])
                vs.append(v[:, sl])
                decay.append(jnp.where(incl, jnp.exp(col(gc, gl) - gc_t[gl:gl + 1, :]), 0.0))
                strict.append(rel > 0 if d == 0 else rel < 0)
                beta_c.append(col(g, 2 * GDN_H + gl))
                e_gc.append(col(eg, gl))
                e_rest.append(col(er, gl))
                e_tot.append(col(et, gl))
    chains = range(len(qs))
    kb = [ks[c] * beta_c[c] for c in chains]
    a_mat = [jnp.where(strict[c], nt(kb[c], ks[c]) * decay[c], 0.0) for c in chains]
    intra = [nt(qs[c], ks[c]) * decay[c] for c in chains]
    ri = lax.broadcasted_iota(jnp.int32, (C, C), 0)
    ci = lax.broadcasted_iota(jnp.int32, (C, C), 1)

    def level_mask(s, upper):
        sh = s.bit_length() - 1
        same = (ri >> (sh + 1)) == (ci >> (sh + 1))
        r_hi = ((ri >> sh) & 1) == 1
        c_hi = ((ci >> sh) & 1) == 1
        return same & c_hi & ~r_hi if upper else same & r_hi & ~c_hi

    masks = {s: (level_mask(s, False), level_mask(s, True)) for s in (1, 2, 4, 8, 16, 32)}
    upper = [(c // GDN_H) % 2 == 1 for c in chains]
    eye = (ri == ci).astype(jnp.float32)
    t_inv = [eye - jnp.where(masks[1][upper[c]], a_mat[c], 0.0) for c in chains]
    for s in (2, 4, 8, 16, 32):
        tm = [mm(t_inv[c], jnp.where(masks[s][upper[c]], a_mat[c], 0.0)) for c in chains]
        t_inv = [t_inv[c] - mm(tm[c], t_inv[c]) for c in chains]
    xs = [mm(t_inv[c], jnp.concatenate([vs[c] * beta_c[c], kb[c] * e_gc[c]], axis=1)) for c in chains]
    st = [s_ref[c] for c in chains]
    v_new = [xs[c][:, :HEAD_DIM] - mm(xs[c][:, HEAD_DIM:], st[c]) for c in chains]
    o_st = [mm(qs[c] * e_gc[c], st[c]) for c in chains]
    o_in = [mm(intra[c], v_new[c]) for c in chains]
    s_up = [tn(ks[c] * e_rest[c], v_new[c]) for c in chains]
    for c in chains:
        s_ref[c] = st[c] * e_tot[c] + s_up[c]
    outs = [o_st[c] + o_in[c] for c in chains]
    for bi in range(nb):
        for d, o_ref in enumerate((of_ref, ob_ref)):
            c0 = (bi * 2 + d) * GDN_H
            o_ref[bi] = jnp.concatenate(outs[c0:c0 + GDN_H], axis=1)


def gdn_scan(q, k, v, gates, n_ctx_chunks):
    B, Tt, W = q.shape
    C = GDN_CHUNK
    NC = Tt // C

    def bwd(s):
        return jnp.where(s < n_ctx_chunks, n_ctx_chunks - 1 - s, NC - 1 - (s - n_ctx_chunks))

    tok_f = pl.BlockSpec((B, C, W), lambda s: (0, s, 0))
    tok_b = pl.BlockSpec((B, C, W), lambda s: (0, bwd(s), 0))
    gate_f = pl.BlockSpec((B, C, LANE), lambda s: (0, s, 0))
    gate_b = pl.BlockSpec((B, C, LANE), lambda s: (0, bwd(s), 0))
    return pl.pallas_call(
        _gdn_kernel,
        grid=(NC,),
        in_specs=[tok_f, tok_f, tok_f, gate_f, tok_b, tok_b, tok_b, gate_b],
        out_specs=[tok_f, tok_b],
        out_shape=[jax.ShapeDtypeStruct((B, Tt, W), jnp.float32)] * 2,
        scratch_shapes=[pltpu.VMEM((B * 2 * GDN_H, HEAD_DIM, HEAD_DIM), jnp.float32)],
        compiler_params=pltpu.CompilerParams(dimension_semantics=("arbitrary",), vmem_limit_bytes=VMEM_LIMIT),
        name="gdn_scan",
    )(q, k, v, gates, q, k, v, gates)


def _stream_to_scan_rows(i, n_lat_tiles, lat_tiles, ctx_tiles):
    ic = i - n_lat_tiles
    return (jnp.where(i < n_lat_tiles, i // lat_tiles, ic // ctx_tiles),
            jnp.where(i < n_lat_tiles, ctx_tiles + i % lat_tiles, ic % ctx_tiles), 0)


def _gdn_pre_kernel(xq_ref, xk_ref, xv_ref, pq_ref, pk_ref, pv_ref, nq_ref, nk_ref, nv_ref, gab_ref, cw_ref,
                    al_ref, dtb_ref, hs_ref, q_ref, k_ref, v_ref, gate_ref, *, n_lat_tiles, lat_tiles, ctx_tiles):
    i = pl.program_id(0)
    R = xq_ref.shape[0]
    pos = jnp.where(i < n_lat_tiles, i % lat_tiles, (i - n_lat_tiles) % ctx_tiles)
    last = jnp.where(i < n_lat_tiles, lat_tiles - 1, ctx_tiles - 1)
    rows = lax.broadcasted_iota(jnp.int32, (R, 1), 0)
    f32 = lambda t: t.astype(jnp.float32)

    def conv_silu(x_ref, p_ref, n_ref, part):
        x = f32(x_ref[...])
        before = jnp.where(pos == 0, 0.0, f32(p_ref[7:8, :]))
        after = jnp.where(pos == last, 0.0, f32(n_ref[0:1, :]))
        x_prev = jnp.where(rows == 0, before, pltpu.roll(x, 1, axis=0))
        x_next = jnp.where(rows == R - 1, after, pltpu.roll(x, R - 1, axis=0))
        cw = cw_ref[:, part * MIX_W:(part + 1) * MIX_W]
        u = x_prev * cw[0:1] + x * cw[1:2] + x_next * cw[2:3]
        return u * jax.nn.sigmoid(u)

    def head_sumsq(t):
        sq = t * t
        hi = sq.astype(jnp.bfloat16)
        lo = (sq - f32(hi)).astype(jnp.bfloat16)
        return (jnp.dot(hi, hs_ref[...], preferred_element_type=jnp.float32)
                + jnp.dot(lo, hs_ref[...], preferred_element_type=jnp.float32))

    q = conv_silu(xq_ref, pq_ref, nq_ref, 0)
    k = conv_silu(xk_ref, pk_ref, nk_ref, 1)
    q_ref[0] = q * lax.rsqrt(head_sumsq(q) + RMS_EPS) * HEAD_DIM ** -0.5
    k_ref[0] = k * lax.rsqrt(head_sumsq(k) + RMS_EPS)
    v_ref[0] = conv_silu(xv_ref, pv_ref, nv_ref, 2)
    gab = gab_ref[...]
    t = gab + dtb_ref[...]
    softplus = jnp.maximum(t, 0.0) + jnp.log1p(jnp.exp(-jnp.abs(t)))
    lane = lax.broadcasted_iota(jnp.int32, (1, LANE), 1)
    gate_ref[0] = jnp.where(lane < 2 * GDN_H, -jnp.exp(al_ref[...]) * softplus,
                            jnp.where(lane < 4 * GDN_H, jax.nn.sigmoid(gab), 0.0))


def gdn_pre_scan_inputs(proj_main, gab, conv_w, a_log, dt_bias, B, n, m):
    T = proj_main.shape[0]
    R = MERGE_TILE
    lat_tiles, ctx_tiles = n // R, m // R
    maps = dict(n_lat_tiles=B * lat_tiles, lat_tiles=lat_tiles, ctx_tiles=ctx_tiles)
    c0 = MAIN_OFF[2] // MIX_W
    halo = R // 8
    cur = [pl.BlockSpec((R, MIX_W), functools.partial(lambda i, c: (i, c), c=c0 + p)) for p in range(3)]
    prev = [pl.BlockSpec((8, MIX_W), functools.partial(lambda i, c: (jnp.maximum(i * halo - 1, 0), c), c=c0 + p))
            for p in range(3)]
    nxt = [pl.BlockSpec((8, MIX_W), functools.partial(lambda i, c: (jnp.minimum((i + 1) * halo, T // 8 - 1), c),
                                                      c=c0 + p)) for p in range(3)]
    const2 = lambda i: (0, 0)
    out_rows = lambda i: _stream_to_scan_rows(i, **maps)
    pad16 = lambda t: jnp.pad(t.reshape(1, 2 * GDN_H), ((0, 0), (0, LANE - 2 * GDN_H)))
    head_sum = (jnp.arange(MIX_W)[:, None] // HEAD_DIM == jnp.arange(MIX_W)[None, :] // HEAD_DIM).astype(jnp.bfloat16)
    tok = jax.ShapeDtypeStruct((B, m + n, MIX_W), jnp.float32)
    return pl.pallas_call(
        functools.partial(_gdn_pre_kernel, **maps),
        grid=(T // R,),
        in_specs=cur + prev + nxt + [pl.BlockSpec((R, LANE), lambda i: (i, 0)),
                                     pl.BlockSpec((GDN_CONV, 3 * MIX_W), const2),
                                     pl.BlockSpec((1, LANE), const2), pl.BlockSpec((1, LANE), const2),
                                     pl.BlockSpec((MIX_W, MIX_W), const2)],
        out_specs=[pl.BlockSpec((1, R, MIX_W), out_rows)] * 3 + [pl.BlockSpec((1, R, LANE), out_rows)],
        out_shape=[tok, tok, tok, jax.ShapeDtypeStruct((B, m + n, LANE), jnp.float32)],
        compiler_params=pltpu.CompilerParams(dimension_semantics=("arbitrary",), vmem_limit_bytes=VMEM_LIMIT),
        name="gdn_pre",
    )(*([proj_main] * 9), gab, conv_w.T, pad16(a_log), pad16(dt_bias), head_sum)


def gdn_branch(proj_main, gab, conv_w, a_log, dt_bias, B, n, m):
    q, k, v, gates = gdn_pre_scan_inputs(proj_main, gab, conv_w, a_log, dt_bias, B, n, m)
    return gdn_scan(q, k, v, gates, m // GDN_CHUNK)


FFT_R = 128
FFT_COLS = 4096
FFT_K1_STEP = 8


def _dft_tables(L):
    R = FFT_R
    N = R * R
    half = L // R
    idx = jnp.arange(R, dtype=jnp.int32)
    ang1 = (-2.0 * math.pi / R) * ((idx[:, None] * idx[None, :]) % R).astype(jnp.float32)
    fr, fi = jnp.cos(ang1), jnp.sin(ang1)
    blk = lambda re, im: jnp.concatenate([jnp.concatenate([re, -im], axis=1),
                                          jnp.concatenate([im, re], axis=1)], axis=0)
    m1 = blk(fr[:, :half], fi[:, :half])
    m1_real = jnp.concatenate([fr, fi], axis=0)
    m3 = blk(fr.T[:half], -fi.T[:half]) * (1.0 / N)
    k = idx[:, None, None] + R * idx[None, :, None]
    ang2 = (-2.0 * math.pi / N) * ((idx[None, None, :] * k) % N).astype(jnp.float32)
    gr, gi = jnp.cos(ang2), jnp.sin(ang2)
    g_fwd = jnp.concatenate([jnp.concatenate([gr, -gi], axis=2),
                             jnp.concatenate([gi, gr], axis=2)], axis=1)
    g_inv = jnp.swapaxes(g_fwd, 1, 2)
    bf = lambda t: t.astype(jnp.bfloat16)
    return bf(m1), bf(m1_real), bf(m3), bf(g_fwd), bf(g_inv)


def _colmm_kernel(m_ref, x_ref, o_ref):
    o_ref[...] = jnp.dot(m_ref[...], x_ref[...].astype(jnp.bfloat16),
                         preferred_element_type=jnp.float32).astype(o_ref.dtype)


def colmm(mat, x, out_dtype=jnp.bfloat16):
    M, K = mat.shape
    n_cols = x.shape[1]
    return pl.pallas_call(
        _colmm_kernel,
        grid=(n_cols // FFT_COLS,),
        in_specs=[pl.BlockSpec((M, K), lambda j: (0, 0)), pl.BlockSpec((K, FFT_COLS), lambda j: (0, j))],
        out_specs=pl.BlockSpec((M, FFT_COLS), lambda j: (0, j)),
        out_shape=jax.ShapeDtypeStruct((M, n_cols), out_dtype),
        compiler_params=pltpu.CompilerParams(dimension_semantics=("arbitrary",), vmem_limit_bytes=VMEM_LIMIT),
        name="hyena_colmm",
    )(mat, x)


def _colmm_gate_kernel(m_ref, r_ref, z_ref, x_ref, skip_ref, o_ref):
    y = jnp.dot(m_ref[...], r_ref[...], preferred_element_type=jnp.float32)
    o_ref[...] = x_ref[...] * (y + z_ref[...] * skip_ref[...])


def colmm_gate(mat, r, z, gate, skip_cols):
    M, K = mat.shape
    n_cols = r.shape[1]
    col = lambda j: (0, j)
    return pl.pallas_call(
        _colmm_gate_kernel,
        grid=(n_cols // FFT_COLS,),
        in_specs=[pl.BlockSpec((M, K), lambda j: (0, 0)), pl.BlockSpec((K, FFT_COLS), col),
                  pl.BlockSpec((M, FFT_COLS), col), pl.BlockSpec((M, FFT_COLS), col),
                  pl.BlockSpec((1, FFT_COLS), col)],
        out_specs=pl.BlockSpec((M, FFT_COLS), col),
        out_shape=jax.ShapeDtypeStruct((M, n_cols), jnp.float32),
        compiler_params=pltpu.CompilerParams(dimension_semantics=("arbitrary",), vmem_limit_bytes=VMEM_LIMIT),
        name="hyena_colmm_gate",
    )(mat, r, z, gate, skip_cols)


def _spectrum_kernel(p_ref, g_ref, o_ref):
    R = FFT_R
    for j in range(FFT_K1_STEP):
        p = jnp.concatenate([p_ref[0, j], p_ref[1, j]], axis=0)
        q = jnp.dot(g_ref[j], p, preferred_element_type=jnp.float32)
        o_ref[0, j] = q[:R]
        o_ref[1, j] = q[R:]


def _freq_kernel(p_ref, g_ref, gi_ref, h_ref, o_ref):
    R = FFT_R
    for j in range(FFT_K1_STEP):
        p = jnp.concatenate([p_ref[0, j], p_ref[1, j]], axis=0)
        q = jnp.dot(g_ref[j], p, preferred_element_type=jnp.float32)
        qr, qi = q[:R], q[R:]
        hr, hi = h_ref[0, j], h_ref[1, j]
        y = jnp.concatenate([qr * hr - qi * hi, qr * hi + qi * hr], axis=0).astype(jnp.bfloat16)
        r = jnp.dot(gi_ref[j], y, preferred_element_type=jnp.float32).astype(o_ref.dtype)
        o_ref[0, j] = r[:R]
        o_ref[1, j] = r[R:]


def hyena_spectrum(p, g_fwd):
    C = p.shape[-1]
    R, S = FFT_R, FFT_K1_STEP
    blk = pl.BlockSpec((2, S, R, LANE * 4), lambda i, c: (0, i, 0, c))
    return pl.pallas_call(
        _spectrum_kernel,
        grid=(R // S, C // (LANE * 4)),
        in_specs=[blk, pl.BlockSpec((S, 2 * R, 2 * R), lambda i, c: (i, 0, 0))],
        out_specs=blk,
        out_shape=jax.ShapeDtypeStruct((2, R, R, C), jnp.float32),
        compiler_params=pltpu.CompilerParams(dimension_semantics=("arbitrary",) * 2, vmem_limit_bytes=VMEM_LIMIT),
        name="hyena_spectrum",
    )(p, g_fwd)


def hyena_freq(p, g_fwd, g_inv, spec, order):
    C = p.shape[-1]
    R, S = FFT_R, FFT_K1_STEP
    blk = pl.BlockSpec((2, S, R, C), lambda i: (0, i, 0, 0))
    gspec = pl.BlockSpec((S, 2 * R, 2 * R), lambda i: (i, 0, 0))
    return pl.pallas_call(
        _freq_kernel,
        grid=(R // S,),
        in_specs=[blk, gspec, gspec, pl.BlockSpec((2, S, R, C), lambda i: (0, i, 0, order))],
        out_specs=blk,
        out_shape=jax.ShapeDtypeStruct((2, R, R, C), jnp.bfloat16),
        compiler_params=pltpu.CompilerParams(dimension_semantics=("arbitrary",), vmem_limit_bytes=VMEM_LIMIT),
        name="hyena_freq",
    )(p, g_fwd, g_inv, spec)


def hyena_taps(L, w1, b1, w2, b2, w3, freq, deltas):
    f32 = jnp.float32
    pos = jnp.arange(L, dtype=f32)
    pos_b = L - pos
    bands = jnp.linspace(1e-4, HY_BANDS - 1, HY_BANDS, dtype=f32)[None, :]

    def mlp(j, w3_dir, delta_dir):
        t = (j / (L - 1))[:, None]
        ang = (2.0 * math.pi / L) * j[:, None]
        feats = jnp.concatenate([t, jnp.cos(bands * ang), -jnp.sin(bands * ang)], axis=-1)
        h = jnp.sin(freq * (feats @ w1 + b1))
        h = jnp.sin(freq * (h @ w2 + b2))
        h = h @ w3_dir
        return h * (jnp.exp(-t * jnp.abs(delta_dir).reshape(1, -1)) + HY_MOD_SHIFT)

    w3d = w3.reshape(w3.shape[0], 2, HY_ORDER * HY_W)
    h_f = mlp(pos, w3d[:, 0], deltas[0])
    h_b = mlp(pos_b, w3d[:, 1], deltas[1])
    h_b = jnp.where((pos > 0)[:, None], h_b, 0.0)
    taps = jnp.concatenate([h_f, h_b], axis=0)
    return taps / jnp.sum(jnp.abs(taps), axis=0, keepdims=True)


def hyena_latent(proj, conv_w, skip, filt, tables):
    B, L, _ = proj.shape
    assert B == 2 and L % FFT_R == 0 and 2 * L == FFT_R * FFT_R
    R, C = FFT_R, HY_W
    m1, m1_real, m3, g_fwd, g_inv = tables
    taps = hyena_taps(L, *filt)
    spec = hyena_spectrum(colmm(m1_real, taps.reshape(R, R * HY_ORDER * C)).reshape(2, R, R, HY_ORDER * C), g_fwd)
    u = centred_dwconv(proj, conv_w)
    v, x1, x2 = jnp.split(u, 3, axis=-1)
    rows = B * (L // R)
    z = v.reshape(rows, R * C)
    for o, gate in enumerate((x1, x2)):
        p = colmm(m1, z).reshape(2, R, R, C)
        r = hyena_freq(p, g_fwd, g_inv, spec, o).reshape(2 * R, R * C)
        z = colmm_gate(m3, r, z, gate.reshape(rows, R * C), jnp.tile(skip[o], R).reshape(1, R * C))
    return z.reshape(B, L, C)


def _combine_ln_kernel(dest_ref, x_ref, w_ref, mod_ref, lng_ref, lnb_ref, yb_ref, o_ref, buf_ref, sem, *, alpha):
    R = x_ref.shape[0]

    def row_copy(k, r):
        row = dest_ref[0, 0, k * R + r]
        return pltpu.make_async_copy(yb_ref.at[pl.ds(row, 1)], buf_ref.at[k, pl.ds(r, 1)], sem.at[k])

    def issue(r, carry):
        row_copy(0, r).start()
        row_copy(1, r).start()
        return carry

    def drain(r, carry):
        row_copy(0, r).wait()
        row_copy(1, r).wait()
        return carry

    lax.fori_loop(0, R, issue, 0, unroll=8)
    lax.fori_loop(0, R, drain, 0, unroll=8)
    y = w_ref[:, 0:1] * buf_ref[0] + w_ref[:, 1:2] * buf_ref[1]
    z = alpha * x_ref[...] + mod_ref[0, 5:6, :] * y
    o_ref[...] = _layer_norm_rows(z) * lng_ref[...] + lnb_ref[...]


def moe_combine_ln(x, yb, dest, wts, mod, ln_g, ln_b, alpha):
    T, D = x.shape
    R = MERGE_TILE
    dest_tiles = jnp.swapaxes(dest.reshape(T // R, R, TOP_K), 1, 2).reshape(T // R, 1, TOP_K * R)
    row = lambda i: (i, 0)
    return pl.pallas_call(
        functools.partial(_combine_ln_kernel, alpha=alpha),
        grid=(T // R,),
        in_specs=[pl.BlockSpec((1, 1, TOP_K * R), lambda i: (i, 0, 0), memory_space=pltpu.SMEM),
                  pl.BlockSpec((R, D), row),
                  pl.BlockSpec((R, TOP_K), row),
                  pl.BlockSpec((1, 6, D), lambda i: (i, 0, 0)),
                  pl.BlockSpec((1, D), lambda i: (0, 0)),
                  pl.BlockSpec((1, D), lambda i: (0, 0)),
                  pl.BlockSpec(memory_space=pl.ANY)],
        out_specs=pl.BlockSpec((R, D), row),
        out_shape=jax.ShapeDtypeStruct((T, D), jnp.float32),
        scratch_shapes=[pltpu.VMEM((TOP_K, R, D), jnp.float32), pltpu.SemaphoreType.DMA((TOP_K,))],
        compiler_params=pltpu.CompilerParams(dimension_semantics=("arbitrary",), vmem_limit_bytes=VMEM_LIMIT),
        name="moe_combine_ln",
    )(dest_tiles, x, wts.T, mod, ln_g.reshape(1, D), ln_b.reshape(1, D), yb)


def moe_dispatch(idx, n_tok):
    n_slot = n_tok * TOP_K
    e_flat = idx.T.reshape(-1)
    onehot = (e_flat[:, None] == jnp.arange(N_EXPERTS, dtype=jnp.int32)[None, :]).astype(jnp.int32)
    csum = jnp.cumsum(onehot, axis=0)
    rank = jnp.sum(csum * onehot, axis=1) - 1
    counts = csum[-1]
    padded = (counts + EXPERT_BLOCK - 1) // EXPERT_BLOCK * EXPERT_BLOCK
    pend = jnp.cumsum(padded)
    pstart = pend - padded
    dest = jnp.sum(onehot * pstart[None, :], axis=1) + rank
    n_blk = -(-n_slot // EXPERT_BLOCK) + N_EXPERTS
    cap = n_blk * EXPERT_BLOCK
    t_flat = jnp.arange(n_slot, dtype=jnp.int32) // TOP_K
    buf_tok = jnp.zeros((cap,), jnp.int32).at[dest].set(t_flat)
    blk_exp = jnp.minimum(jnp.searchsorted(pend, jnp.arange(n_blk, dtype=jnp.int32) * EXPERT_BLOCK, side='right'),
                          N_EXPERTS - 1).astype(jnp.int32)
    n_used = (pend[-1] // EXPERT_BLOCK).astype(jnp.int32).reshape(1)
    return buf_tok, dest, blk_exp, n_used


def rms_norm(x, w):
    return x * lax.rsqrt(jnp.mean(jnp.square(x), -1, keepdims=True) + RMS_EPS) * w


def centred_dwconv(u, w):
    K = w.shape[-1]
    T = u.shape[1]
    up = jnp.pad(u, ((0, 0), (K // 2, K // 2), (0, 0)))
    out = up[:, 0:T, :] * w[:, 0]
    for j in range(1, K):
        out = out + up[:, j:j + T, :] * w[:, j]
    return out


def hyena_filters(L, w1, b1, w2, b2, w3, freq, deltas):
    f32 = jnp.float32
    t = jnp.linspace(0.0, 1.0, L, dtype=f32)[:, None]
    ang = 2.0 * math.pi * jnp.arange(L, dtype=f32)[:, None] / L
    bands = jnp.linspace(1e-4, HY_BANDS - 1, HY_BANDS, dtype=f32)[None, :]
    feats = jnp.concatenate([t, jnp.cos(bands * ang), -jnp.sin(bands * ang)], axis=-1)
    h = jnp.sin(freq * (feats @ w1 + b1))
    h = jnp.sin(freq * (h @ w2 + b2))
    h = (h @ w3).reshape(L, 2, HY_ORDER, HY_W)
    window = jnp.exp(-t[:, :, None, None] * jnp.abs(deltas)) + HY_MOD_SHIFT
    h = h * window
    h_fwd, h_bwd = h[:, 0], h[:, 1]
    taps = jnp.concatenate([h_fwd, jnp.zeros_like(h_fwd[:1]), jnp.flip(h_bwd[1:], 0)], axis=0)
    taps = taps / jnp.sum(jnp.abs(taps), axis=0, keepdims=True)
    return jnp.fft.rfft(taps, axis=0)


def hyena_mix(proj, conv_w, skip, filt_f):
    L = proj.shape[1]
    u = centred_dwconv(proj, conv_w)
    v, x1, x2 = jnp.split(u, 3, axis=-1)
    z = v
    for o, gate in enumerate((x1, x2)):
        zf = jnp.fft.rfft(z, n=2 * L, axis=1)
        y = jnp.fft.irfft(zf * filt_f[:, o], n=2 * L, axis=1)[:, :L]
        z = gate * (y + z * skip[o])
    return z


def axial_rope_angles(n):
    t = jnp.arange(n)
    row = (t // GRID_W).astype(jnp.float32)
    col = (t % GRID_W).astype(jnp.float32)
    nf = HEAD_DIM // 4
    inv = ROPE_THETA ** (-jnp.arange(nf, dtype=jnp.float32) / nf)
    return row[:, None] * inv, col[:, None] * inv


def diff_attend(q, k, v, lam):
    s = jnp.einsum('bqhcd,bkhcd->bhcqk', q, k) * HEAD_DIM ** -0.5
    p = jax.nn.softmax(s, axis=-1)
    a = p[:, :, 0] - lam * p[:, :, 1]
    return jnp.einsum('bhqk,bkhd->bqhd', a, v)


def diff_finish(o, norm_w, lam_init):
    B, T = o.shape[:2]
    return (rms_norm(o, norm_w) * (1.0 - lam_init)).reshape(B, T, DIFF_H * DIFF_VD)


def split_diff(p):
    B, T, _ = p.shape
    q, k, v = jnp.split(p, 3, axis=-1)
    return (q.reshape(B, T, DIFF_H, 2, HEAD_DIM), k.reshape(B, T, DIFF_H, 2, HEAD_DIM),
            v.reshape(B, T, DIFF_H, DIFF_VD))


def dense_attend(q, k, v):
    s = jnp.einsum('bqhd,bkhd->bhqk', q, k) * HEAD_DIM ** -0.5
    p = jax.nn.softmax(s, axis=-1)
    return jnp.einsum('bhqk,bkhd->bqhd', p, v)


def _diff_pre_kernel(xq_ref, xk_ref, xv_ref, cos_ref, sin_ref, q_ref, k_ref, v_ref, *, n_lat_tiles, q_scale):
    is_lat = pl.program_id(0) < n_lat_tiles
    cos = jnp.tile(jnp.where(is_lat, cos_ref[...], 1.0), (1, DIFF_H))
    sin = jnp.tile(jnp.where(is_lat, sin_ref[...], 0.0), (1, DIFF_H))
    lane = lax.broadcasted_iota(jnp.int32, (1, MIX_W), 1)
    first_half = (lane % (HEAD_DIM // 2)) < HEAD_DIM // 4

    def rope(x):
        x = x.astype(jnp.float32)
        partner = jnp.where(first_half, pltpu.roll(x, MIX_W - HEAD_DIM // 4, axis=1),
                            pltpu.roll(x, HEAD_DIM // 4, axis=1))
        return x * cos + partner * sin

    q_ref[...] = (rope(xq_ref[...]) * q_scale).astype(q_ref.dtype)
    k_ref[0] = rope(xk_ref[...]).astype(k_ref.dtype)
    v_ref[0] = xv_ref[...]


def diff_pre(proj_main, cos_t, sin_t, B, n, m):
    T = proj_main.shape[0]
    R = MERGE_TILE
    lat_tiles, ctx_tiles = n // R, m // R
    n_lat_tiles = B * lat_tiles
    c0 = MAIN_OFF[4] // MIX_W
    cur = [pl.BlockSpec((R, MIX_W), functools.partial(lambda i, c: (i, c), c=c0 + p)) for p in range(3)]
    table = pl.BlockSpec((R, DIFF_VD), lambda i: (jnp.where(i < n_lat_tiles, i % lat_tiles, 0), 0))

    def key_rows(i):
        ic = i - n_lat_tiles
        return (jnp.where(i < n_lat_tiles, i // lat_tiles, ic // ctx_tiles),
                jnp.where(i < n_lat_tiles, i % lat_tiles, lat_tiles + ic % ctx_tiles), 0)

    kv = jax.ShapeDtypeStruct((B, n + m, MIX_W), jnp.bfloat16)
    return pl.pallas_call(
        functools.partial(_diff_pre_kernel, n_lat_tiles=n_lat_tiles,
                          q_scale=HEAD_DIM ** -0.5 * math.log2(math.e)),
        grid=(T // R,),
        in_specs=cur + [table, table],
        out_specs=[pl.BlockSpec((R, MIX_W), lambda i: (i, 0)),
                   pl.BlockSpec((1, R, MIX_W), key_rows), pl.BlockSpec((1, R, MIX_W), key_rows)],
        out_shape=[jax.ShapeDtypeStruct((T, MIX_W), jnp.bfloat16), kv, kv],
        compiler_params=pltpu.CompilerParams(dimension_semantics=("arbitrary",), vmem_limit_bytes=VMEM_LIMIT),
        name="diff_pre",
    )(proj_main, proj_main, proj_main, cos_t, sin_t)


def rope_tables(ang_r, ang_c):
    cos = jnp.concatenate([jnp.cos(ang_r)] * 2 + [jnp.cos(ang_c)] * 2, axis=-1)
    sin = jnp.concatenate([-jnp.sin(ang_r), jnp.sin(ang_r), -jnp.sin(ang_c), jnp.sin(ang_c)], axis=-1)
    return jnp.tile(cos, (1, 2)), jnp.tile(sin, (1, 2))


def diff_branch(proj_main, cos_t, sin_t, lam, norm_w, lam_init, B, n, m):
    q, k_all, v_all = diff_pre(proj_main, cos_t, sin_t, B, n, m)
    nk = n + m
    k_all = jnp.transpose(k_all.reshape(B, nk, DIFF_H, 2, HEAD_DIM), (0, 2, 3, 1, 4))
    vt_all = jnp.transpose(v_all.reshape(B, nk // DIFF_TK, DIFF_TK, DIFF_H, DIFF_VD), (0, 3, 1, 4, 2))
    return diff_attention(q[:B * n].reshape(B, n, MIX_W), k_all, vt_all, lam, norm_w, 1.0 - lam_init)


def _split_w_in(w):
    segs = jnp.split(w, SPLIT_IDX, axis=-1)
    ab = jnp.concatenate([segs[3], segs[4]], axis=-1)
    ab = jnp.pad(ab, ((0, 0), (0, LANE - ab.shape[-1])))
    main = jnp.concatenate([segs[7], segs[0], segs[1], segs[2], segs[5], segs[6]], axis=-1)
    return main.astype(jnp.bfloat16), ab.astype(jnp.bfloat16)


MAIN_SIZES = (N_BRANCH * D_MODEL, 3 * HY_W, 3 * MIX_W, MIX_W, 3 * MIX_W, 3 * MIX_W)
MAIN_OFF = tuple(sum(MAIN_SIZES[:i]) for i in range(len(MAIN_SIZES) + 1))
IN_TN = 768


def kernel(x, c, ctx, c_ctx, w_mod, b_mod, w_in, hy_conv, hy_w1, hy_b1, hy_w2, hy_b2, hy_w3, hy_freq,
           hy_deltas, hy_skip, gdn_conv, gdn_a_log, gdn_dt_bias, gdn_norm, diff_lam, diff_norm, na_rpb,
           branch_proj, w_out, ln_g, ln_b, router_w, router_b, exp_w1, exp_w3, exp_w2):
    B, n, D = x.shape
    m = ctx.shape[1]
    depth = w_mod.shape[0]
    dn_alpha = (2 * depth) ** 0.25
    ang_r, ang_c = axial_rope_angles(n)
    bf16 = jnp.bfloat16
    n_lat = B * n
    dft_tables = _dft_tables(n)
    cos_t, sin_t = rope_tables(ang_r, ang_c)
    tok = jnp.concatenate([x.reshape(n_lat, D), ctx.reshape(B * m, D)], axis=0)
    for l in range(depth):
        ctx_out = l < depth - 1
        mx = jax.nn.silu(c) @ w_mod[l] + b_mod[l]
        mc = jax.nn.silu(c_ctx) @ w_mod[l] + b_mod[l]
        one_plus = jnp.array([0.0, 1.0, 0.0, 0.0, 1.0, 0.0], jnp.float32)[:, None]
        mod_x = mx.reshape(B, 6, D) + one_plus
        mod_c = mc.reshape(1, 6, D) + one_plus
        mod = jnp.concatenate([jnp.repeat(mod_x, n // MERGE_TILE, axis=0),
                               jnp.repeat(mod_c, B * m // MERGE_TILE, axis=0)], axis=0)

        proj, gab = ln_mod_matmul(tok, mod, *_split_w_in(w_in[l]), IN_TN)

        def seg(i, latent):
            rows = slice(0, n_lat) if latent else slice(n_lat, None)
            return proj[rows, MAIN_OFF[i]:MAIN_OFF[i + 1]].reshape(B, -1, MAIN_SIZES[i]).astype(jnp.float32)

        hy_x, gqkv_x, gz_x, dqkv_x, nqkv_x = [seg(i, True) for i in range(1, 6)]
        hy_c, gqkv_c, gz_c, dqkv_c, nqkv_c = [seg(i, False) for i in range(1, 6)]
        gab_x, gab_c = gab[:n_lat].reshape(B, n, LANE), gab[n_lat:].reshape(B, m, LANE)
        ga_x, gb_x = gab_x[..., :2 * GDN_H], gab_x[..., 2 * GDN_H:4 * GDN_H]
        ga_c, gb_c = gab_c[..., :2 * GDN_H], gab_c[..., 2 * GDN_H:4 * GDN_H]
        filt = (hy_w1[l], hy_b1[l], hy_w2[l], hy_b2[l], hy_w3[l], hy_freq[l], hy_deltas[l])

        ya_x = hyena_latent(hy_x, hy_conv[l], hy_skip[l], filt, dft_tables)

        o_f, o_b = gdn_branch(proj, gab, gdn_conv[l], gdn_a_log[l], gdn_dt_bias[l], B, n, m)

        lq1, lk1, lq2, lk2 = diff_lam[l]
        lam_init = 0.8 - 0.6 * math.exp(-0.3 * l)
        lam = jnp.exp(jnp.sum(lq1 * lk1)) - jnp.exp(jnp.sum(lq2 * lk2)) + lam_init
        yc_x = diff_branch(proj, cos_t, sin_t, lam, diff_norm[l], lam_init, B, n, m)
        yd_x = na_attention(proj, na_bias_tables(na_rpb[l], n // GRID_W), B, n, m)

        ys = [t.reshape(n_lat, MIX_W) for t in (ya_x, yc_x, yd_x)]
        if ctx_out:
            dq_c, dk_c, dv_c = split_diff(dqkv_c)
            nq_c, nk_c, nv_c = [t.reshape(B, m, NA_H, HEAD_DIM) for t in jnp.split(nqkv_c, 3, axis=-1)]
            ya_c = hyena_mix(hy_c, hy_conv[l], hy_skip[l], hyena_filters(m, *filt))
            yc_c = diff_finish(diff_attend(dq_c, dk_c, dv_c, lam), diff_norm[l], lam_init)
            yd_c = dense_attend(nq_c, nk_c, nv_c)
            ys_c = [t.reshape(B * m, MIX_W) for t in (ya_c, yc_c, yd_c)]
            ys = [jnp.concatenate([a, b], axis=0) for a, b in zip(ys, ys_c)]
        else:
            tok = tok[:n_lat]
        tok = merge_residual_ln(tok, ys[0], o_f, o_b, ys[1], ys[2], proj, mod, branch_proj[l].astype(bf16),
                                w_out[l].astype(bf16), gdn_norm[l], ln_g[l, 0], ln_b[l, 0], dn_alpha, n, m)

        n_tok = tok.shape[0]
        h2, idx, wts = moe_route(tok, mod, router_w, router_b)
        buf_tok, dest, blk_exp, n_used = moe_dispatch(idx, n_tok)
        yb = expert_ffn(blk_exp, n_used, jnp.take(h2, buf_tok, axis=0), exp_w1[l], exp_w3[l], exp_w2[l])
        tok = moe_combine_ln(tok, yb, dest.reshape(n_tok, TOP_K), wts, mod, ln_g[l, 1], ln_b[l, 1], dn_alpha)
    return tok[:n_lat].reshape(B, n, D)
```

```python
import functools
import math

import jax
import jax.numpy as jnp
from jax import lax
from jax.experimental import pallas as pl
from jax.experimental.pallas import tpu as pltpu

D_MODEL = 1024
GRID_W = 64
HEAD_DIM = 64
MIX_W = D_MODEL // 2
N_BRANCH = 4
HY_W = MIX_W
HY_ORDER = 2
HY_BANDS = 16
HY_MOD_SHIFT = 0.05
GDN_H = MIX_W // HEAD_DIM
GDN_CHUNK = 64
GDN_CONV = 3
DIFF_VD = 2 * HEAD_DIM
DIFF_H = MIX_W // DIFF_VD
NA_H = MIX_W // HEAD_DIM
NA_ROWS = 8
NA_COLS = 16
N_EXPERTS = 16
N_GROUPS = 4
TOP_K = 2
ROPE_THETA = 10000.0
LN_EPS = 1e-5
RMS_EPS = 1e-6
SPLIT_SIZES = (3 * HY_W, 3 * MIX_W, MIX_W, 2 * GDN_H, 2 * GDN_H, 3 * MIX_W, 3 * MIX_W, N_BRANCH * D_MODEL)
SPLIT_IDX = tuple(sum(SPLIT_SIZES[:i + 1]) for i in range(len(SPLIT_SIZES) - 1))

LANE = 128
ROW_TILE = 1536
MERGE_TILE = 256
VMEM_LIMIT = 48 * 1024 * 1024


def _ln_mod_matmul_kernel(x_ref, mod_ref, w_ref, wg_ref, o_ref, og_ref, h_ref):
    @pl.when(pl.program_id(1) == 0)
    def _():
        for s in range(mod_ref.shape[0]):
            rows = slice(s * MERGE_TILE, (s + 1) * MERGE_TILE)
            y = _layer_norm_rows(x_ref[rows, :])
            h = (y * mod_ref[s, 1:2, :] + mod_ref[s, 0:1, :]).astype(h_ref.dtype)
            h_ref[rows, :] = h
            og_ref[rows, :] = jnp.dot(h, wg_ref[...], preferred_element_type=jnp.float32)

    o_ref[...] = jnp.dot(h_ref[...], w_ref[...], preferred_element_type=jnp.float32).astype(o_ref.dtype)


def ln_mod_matmul(x, mod, w, w_gates, tn):
    T, D = x.shape
    N = w.shape[1]
    return pl.pallas_call(
        _ln_mod_matmul_kernel,
        grid=(T // ROW_TILE, N // tn),
        in_specs=[
            pl.BlockSpec((ROW_TILE, D), lambda i, j: (i, 0)),
            pl.BlockSpec((ROW_TILE // MERGE_TILE, 6, D), lambda i, j: (i, 0, 0)),
            pl.BlockSpec((D, tn), lambda i, j: (0, j)),
            pl.BlockSpec((D, LANE), lambda i, j: (0, 0)),
        ],
        out_specs=[pl.BlockSpec((ROW_TILE, tn), lambda i, j: (i, j)),
                   pl.BlockSpec((ROW_TILE, LANE), lambda i, j: (i, 0))],
        out_shape=[jax.ShapeDtypeStruct((T, N), jnp.bfloat16), jax.ShapeDtypeStruct((T, LANE), jnp.float32)],
        scratch_shapes=[pltpu.VMEM((ROW_TILE, D), jnp.bfloat16)],
        compiler_params=pltpu.CompilerParams(
            dimension_semantics=("arbitrary", "arbitrary"), vmem_limit_bytes=VMEM_LIMIT),
        name="ln_mod_matmul",
    )(x, mod, w, w_gates)


def _layer_norm_rows(x):
    mu = jnp.mean(x, axis=-1, keepdims=True)
    xc = x - mu
    var = jnp.mean(xc * xc, axis=-1, keepdims=True)
    return xc * lax.rsqrt(var + LN_EPS)


_NT_DIMS = (((1,), (1,)), ((), ()))


DIFF_TQ = 1024
DIFF_TK = 768
DIFF_ROWS = 32


def _diff_attn_kernel(lam_ref, q_ref, k_ref, vt_ref, nw_ref, o_ref, s_ref, *, out_scale):
    tq = q_ref.shape[1]
    n_tiles = k_ref.shape[3] // DIFF_TK
    q = q_ref[0]
    qc = (q[:, :HEAD_DIM], q[:, HEAD_DIM:])

    sub = 8
    fold = lambda t, op: op(t.reshape(t.shape[0] // sub, sub, t.shape[1]), axis=0)

    def scores(j, slot):
        start = pl.multiple_of(j * DIFF_TK, DIFF_TK)
        mx = []
        for c in range(2):
            k = k_ref[0, 0, c, pl.ds(start, DIFF_TK), :]
            st = lax.dot_general(k, qc[c], _NT_DIMS, preferred_element_type=jnp.float32)
            s_ref[slot, c] = st
            mx.append(fold(st, jnp.max))
        return tuple(mx)

    def softmax_pv(j, slot, mx, carry):
        new, ps = [], []
        for c in range(2):
            m_prev, l_prev, acc = carry[c]
            m_new = jnp.maximum(m_prev, jnp.max(mx[c], axis=0, keepdims=True))
            alpha = jnp.exp2(m_prev - m_new)
            psum, chunks = None, []
            for r in range(DIFF_TK // DIFF_ROWS):
                p = jnp.exp2(s_ref[slot, c, r * DIFF_ROWS:(r + 1) * DIFF_ROWS, :] - m_new)
                part = fold(p, jnp.sum)
                psum = part if psum is None else psum + part
                chunks.append(p.astype(jnp.bfloat16))
            new.append((m_new, alpha * l_prev + jnp.sum(psum, axis=0, keepdims=True), alpha * acc))
            ps.append(jnp.concatenate(chunks, axis=0))
        pv = jnp.dot(vt_ref[0, 0, j], jnp.concatenate(ps, axis=1), preferred_element_type=jnp.float32)
        return tuple((new[c][0], new[c][1], new[c][2] + pv[:, c * tq:(c + 1) * tq]) for c in range(2))

    def pair(jj, state):
        carry, mx0 = state
        j0 = 2 * jj
        mx1 = scores(j0 + 1, 1)
        carry = softmax_pv(j0, 0, mx0, carry)
        mx0 = scores(j0 + 2, 0)
        return softmax_pv(j0 + 1, 1, mx1, carry), mx0

    carry = tuple((jnp.full((1, tq), -jnp.inf, jnp.float32), jnp.zeros((1, tq), jnp.float32),
                   jnp.zeros((DIFF_VD, tq), jnp.float32)) for _ in range(2))
    mx0 = scores(0, 0)
    n_pairs = (n_tiles - 1) // 2
    carry, mx0 = lax.fori_loop(0, n_pairs, pair, (carry, mx0))
    if n_tiles % 2 == 0:
        mx1 = scores(n_tiles - 1, 1)
        carry = softmax_pv(n_tiles - 2, 0, mx0, carry)
        carry = softmax_pv(n_tiles - 1, 1, mx1, carry)
    else:
        carry = softmax_pv(n_tiles - 1, 0, mx0, carry)
    (_, l0, a0), (_, l1, a1) = carry
    o = a0 / l0 - lam_ref[0] * (a1 / l1)
    o = o * lax.rsqrt(jnp.mean(o * o, axis=0, keepdims=True) + RMS_EPS) * (nw_ref[...] * out_scale)
    o_ref[0] = o.T.astype(o_ref.dtype)


def diff_attention(q, k, vt, lam, norm_w, out_scale, out_dtype=jnp.float32):
    B, nq, W = q.shape
    nk = k.shape[3]
    assert nq % DIFF_TQ == 0 and nk % DIFF_TK == 0
    H = W // DIFF_VD
    return pl.pallas_call(
        functools.partial(_diff_attn_kernel, out_scale=out_scale),
        grid=(B, H, nq // DIFF_TQ),
        in_specs=[
            pl.BlockSpec(memory_space=pltpu.SMEM),
            pl.BlockSpec((1, DIFF_TQ, DIFF_VD), lambda b, h, i: (b, i, h)),
            pl.BlockSpec((1, 1, 2, nk, HEAD_DIM), lambda b, h, i: (b, h, 0, 0, 0)),
            pl.BlockSpec((1, 1, nk // DIFF_TK, DIFF_VD, DIFF_TK), lambda b, h, i: (b, h, 0, 0, 0)),
            pl.BlockSpec((DIFF_VD, 1), lambda b, h, i: (0, 0)),
        ],
        out_specs=pl.BlockSpec((1, DIFF_TQ, DIFF_VD), lambda b, h, i: (b, i, h)),
        out_shape=jax.ShapeDtypeStruct((B, nq, W), out_dtype),
        scratch_shapes=[pltpu.VMEM((2, 2, DIFF_TK, DIFF_TQ), jnp.float32)],
        compiler_params=pltpu.CompilerParams(
            dimension_semantics=("arbitrary",) * 3, vmem_limit_bytes=VMEM_LIMIT),
        name="diff_attention",
    )(lam.reshape(1).astype(jnp.float32), q, k, vt, norm_w.reshape(DIFF_VD, 1))


NA_QROWS = 8
NA_KROWS = 16
NA_NEG = -1e30


def _na_kernel(q_ref, k_ref, v_ref, kc_ref, vc_ref, tab_ref, o_ref, *, n_kblocks):
    g = pl.program_id(2)
    kb = (NA_KROWS // 4) * GRID_W
    base = jnp.clip(2 * g - 1, 0, n_kblocks - 4)
    start = pl.multiple_of(base * kb, kb)
    nwin = NA_KROWS * GRID_W
    q = q_ref[...] * HEAD_DIM ** -0.5
    kw = k_ref[pl.ds(start, nwin), :]
    vw = v_ref[pl.ds(start, nwin), :]
    kc = kc_ref[...]
    vc = vc_ref[...]
    outs = []
    for hh in range(2):
        sl = slice(hh * HEAD_DIM, (hh + 1) * HEAD_DIM)
        qh = q[:, sl]
        s = lax.dot_general(qh, kw[:, sl], _NT_DIMS, preferred_element_type=jnp.float32) + tab_ref[0, hh].astype(jnp.float32)
        sc = lax.dot_general(qh, kc[:, sl], _NT_DIMS, preferred_element_type=jnp.float32)
        m = jnp.maximum(jnp.max(s, axis=-1, keepdims=True), jnp.max(sc, axis=-1, keepdims=True))
        p = jnp.exp(s - m)
        pc = jnp.exp(sc - m)
        l = jnp.sum(p, axis=-1, keepdims=True) + jnp.sum(pc, axis=-1, keepdims=True)
        o = (jnp.dot(p.astype(vw.dtype), vw[:, sl], preferred_element_type=jnp.float32)
             + jnp.dot(pc.astype(vc.dtype), vc[:, sl], preferred_element_type=jnp.float32))
        outs.append(o / l)
    o_ref[...] = jnp.concatenate(outs, axis=-1).astype(o_ref.dtype)


def na_bias_tables(rpb, R):
    H = rpb.shape[0]
    G = R // NA_QROWS
    cols = jnp.arange(GRID_W)
    cstart = jnp.clip(cols - NA_COLS // 2, 0, GRID_W - NA_COLS)
    col_ok = (cols[None, :] >= cstart[:, None]) & (cols[None, :] < cstart[:, None] + NA_COLS)
    dc = jnp.clip(cols[None, :] - cols[:, None], -(NA_COLS - 1), NA_COLS - 1) + (NA_COLS - 1)
    rpb_c = rpb[:, :, dc]
    tabs = []
    for g in (0, 1, G - 1):
        base = min(max(2 * g - 1, 0), R // 4 - 4)
        r = NA_QROWS * g + jnp.arange(NA_QROWS)
        rs = jnp.clip(r - NA_ROWS // 2, 0, R - NA_ROWS)
        kr = 4 * base + jnp.arange(NA_KROWS)
        valid = (kr[None, :] >= rs[:, None]) & (kr[None, :] < rs[:, None] + NA_ROWS)
        dr = jnp.clip(kr[None, :] - r[:, None] + (NA_ROWS - 1), 0, 2 * NA_ROWS - 2)
        tab = rpb_c[:, dr]
        ok = valid[None, :, :, None, None] & col_ok[None, None, None, :, :]
        tab = jnp.where(ok, tab, NA_NEG)
        tabs.append(jnp.transpose(tab, (0, 1, 3, 2, 4)).reshape(H, NA_QROWS * GRID_W, NA_KROWS * GRID_W))
    return jnp.stack(tabs, axis=0).astype(jnp.bfloat16)


def na_attention(proj_main, tabs, B, n, m, out_dtype=jnp.float32):
    W = MIX_W
    tq = NA_QROWS * GRID_W
    G = n // tq
    n_kblocks = n // ((NA_KROWS // 4) * GRID_W)
    cq, ck, cv = [(MAIN_OFF[5] + p * W) // LANE for p in range(3)]
    ctx0 = B * n // m

    def tab_index(b, hp, g):
        return (jnp.where(g == 0, 0, jnp.where(g == G - 1, 2, 1)), hp, 0, 0)

    return pl.pallas_call(
        functools.partial(_na_kernel, n_kblocks=n_kblocks),
        grid=(B, W // LANE, G),
        in_specs=[
            pl.BlockSpec((tq, LANE), lambda b, hp, g: (b * G + g, cq + hp)),
            pl.BlockSpec((n, LANE), lambda b, hp, g: (b, ck + hp)),
            pl.BlockSpec((n, LANE), lambda b, hp, g: (b, cv + hp)),
            pl.BlockSpec((m, LANE), lambda b, hp, g: (ctx0 + b, ck + hp)),
            pl.BlockSpec((m, LANE), lambda b, hp, g: (ctx0 + b, cv + hp)),
            pl.BlockSpec((1, 2, tq, NA_KROWS * GRID_W), tab_index),
        ],
        out_specs=pl.BlockSpec((tq, LANE), lambda b, hp, g: (b * G + g, hp)),
        out_shape=jax.ShapeDtypeStruct((B * n, W), out_dtype),
        compiler_params=pltpu.CompilerParams(
            dimension_semantics=("arbitrary",) * 3, vmem_limit_bytes=VMEM_LIMIT),
        name="na_attention",
    )(proj_main, proj_main, proj_main, proj_main, proj_main, tabs)


def _merge_kernel(x_ref, ya_ref, of_ref, ob_ref, yc_ref, yd_ref, gc_ref, gz_ref, mod_ref, p_ref, wo_ref,
                  gn_ref, hs_ref, lng_ref, lnb_ref, o_ref, *, alpha):
    o = of_ref[0] + ob_ref[0]
    sq = o * o
    hi = sq.astype(jnp.bfloat16)
    lo = (sq - hi.astype(jnp.float32)).astype(jnp.bfloat16)
    ms = (jnp.dot(hi, hs_ref[...], preferred_element_type=jnp.float32)
          + jnp.dot(lo, hs_ref[...], preferred_element_type=jnp.float32)) * (1.0 / HEAD_DIM)
    gz = gz_ref[...].astype(jnp.float32)
    yb = o * lax.rsqrt(ms + RMS_EPS) * gn_ref[...] * (gz * jax.nn.sigmoid(gz))
    acc = None
    for mi, y in enumerate((ya_ref[...], yb, yc_ref[...], yd_ref[...])):
        br = jnp.dot(y.astype(jnp.bfloat16), p_ref[mi], preferred_element_type=jnp.float32)
        gm = jax.nn.sigmoid(gc_ref[:, mi * D_MODEL:(mi + 1) * D_MODEL].astype(jnp.float32))
        acc = gm * br if acc is None else acc + gm * br
    mix = jnp.dot(acc.astype(jnp.bfloat16), wo_ref[...], preferred_element_type=jnp.float32)
    z = alpha * x_ref[...] + mod_ref[0, 2:3, :] * mix
    o_ref[...] = _layer_norm_rows(z) * lng_ref[...] + lnb_ref[...]


def merge_residual_ln(x, ya, o_f, o_b, yc, yd, proj_main, mod, proj, w_o, gdn_norm, ln_g, ln_b, alpha, n, m):
    T, D = x.shape
    R = MERGE_TILE
    lat_tiles, ctx_tiles = n // R, m // R
    scan_rows = functools.partial(_stream_to_scan_rows, n_lat_tiles=o_f.shape[0] * lat_tiles,
                                  lat_tiles=lat_tiles, ctx_tiles=ctx_tiles)
    row = lambda i: (i, 0)
    const2 = lambda i: (0, 0)
    branch = pl.BlockSpec((R, MIX_W), row)
    scan = pl.BlockSpec((1, R, MIX_W), scan_rows)
    head_sum = (jnp.arange(MIX_W)[:, None] // HEAD_DIM == jnp.arange(MIX_W)[None, :] // HEAD_DIM).astype(jnp.bfloat16)
    return pl.pallas_call(
        functools.partial(_merge_kernel, alpha=alpha),
        grid=(T // R,),
        in_specs=[pl.BlockSpec((R, D), row), branch, scan, scan, branch, branch,
                  pl.BlockSpec((R, N_BRANCH * D), row),
                  pl.BlockSpec((R, MIX_W), lambda i: (i, MAIN_OFF[3] // MIX_W)),
                  pl.BlockSpec((1, 6, D), lambda i: (i, 0, 0)),
                  pl.BlockSpec((N_BRANCH, MIX_W, D), lambda i: (0, 0, 0)),
                  pl.BlockSpec((D, D), const2),
                  pl.BlockSpec((1, MIX_W), const2),
                  pl.BlockSpec((MIX_W, MIX_W), const2),
                  pl.BlockSpec((1, D), const2),
                  pl.BlockSpec((1, D), const2)],
        out_specs=pl.BlockSpec((R, D), row),
        out_shape=jax.ShapeDtypeStruct((T, D), jnp.float32),
        compiler_params=pltpu.CompilerParams(dimension_semantics=("arbitrary",), vmem_limit_bytes=VMEM_LIMIT),
        name="merge_residual_ln",
    )(x, ya, o_f, o_b, yc, yd, proj_main, proj_main, mod, proj, w_o,
      jnp.tile(gdn_norm, GDN_H).reshape(1, MIX_W), head_sum, ln_g.reshape(1, D), ln_b.reshape(1, D))


def _route_kernel(x_ref, mod_ref, rw_ref, rb_ref, h_ref, idx_ref, wts_ref):
    h = (_layer_norm_rows(x_ref[...]) * mod_ref[0, 4:5, :] + mod_ref[0, 3:4, :]).astype(jnp.bfloat16)
    h_ref[...] = h.astype(h_ref.dtype)
    logits = lax.dot_general(rw_ref[...], h, _NT_DIMS, preferred_element_type=jnp.float32)
    s = jax.nn.sigmoid(logits)
    sel = s + rb_ref[...]
    per = N_EXPERTS // N_GROUPS
    srow = [s[e:e + 1, :] for e in range(N_EXPERTS)]
    vrow = [sel[e:e + 1, :] for e in range(N_EXPERTS)]
    best = None
    for gi in range(N_GROUPS):
        grp = vrow[gi * per:(gi + 1) * per]
        gs = None
        for a in range(per):
            for b in range(a + 1, per):
                ps = grp[a] + grp[b]
                gs = ps if gs is None else jnp.maximum(gs, ps)
        if best is None:
            best, bg = gs, jnp.zeros(gs.shape, jnp.int32)
        else:
            upd = gs > best
            bg = jnp.where(upd, gi, bg)
            best = jnp.where(upd, gs, best)
    cv, cs = [], []
    for j in range(per):
        v_j, s_j = vrow[j], srow[j]
        for gi in range(1, N_GROUPS):
            v_j = jnp.where(bg == gi, vrow[gi * per + j], v_j)
            s_j = jnp.where(bg == gi, srow[gi * per + j], s_j)
        cv.append(v_j)
        cs.append(s_j)

    def first_argmax(vals):
        bv, bi = vals[0], jnp.zeros(vals[0].shape, jnp.int32)
        for j in range(1, per):
            upd = vals[j] > bv
            bi = jnp.where(upd, j, bi)
            bv = jnp.where(upd, vals[j], bv)
        return bi

    i1 = first_argmax(cv)
    i2 = first_argmax([jnp.where(i1 == j, -jnp.inf, cv[j]) for j in range(per)])
    w1 = cs[0]
    w2 = cs[0]
    for j in range(1, per):
        w1 = jnp.where(i1 == j, cs[j], w1)
        w2 = jnp.where(i2 == j, cs[j], w2)
    tot = w1 + w2
    idx_ref[...] = jnp.concatenate([bg * per + i1, bg * per + i2], axis=0)
    wts_ref[...] = jnp.concatenate([w1 / tot, w2 / tot], axis=0)


def moe_route(x, mod, router_w, router_b):
    T, D = x.shape
    return pl.pallas_call(
        _route_kernel,
        grid=(T // MERGE_TILE,),
        in_specs=[pl.BlockSpec((MERGE_TILE, D), lambda i: (i, 0)),
                  pl.BlockSpec((1, 6, D), lambda i: (i, 0, 0)),
                  pl.BlockSpec((N_EXPERTS, D), lambda i: (0, 0)),
                  pl.BlockSpec((N_EXPERTS, 1), lambda i: (0, 0))],
        out_specs=[pl.BlockSpec((MERGE_TILE, D), lambda i: (i, 0)),
                   pl.BlockSpec((TOP_K, MERGE_TILE), lambda i: (0, i)),
                   pl.BlockSpec((TOP_K, MERGE_TILE), lambda i: (0, i))],
        out_shape=[jax.ShapeDtypeStruct((T, D), jnp.float32),
                   jax.ShapeDtypeStruct((TOP_K, T), jnp.int32),
                   jax.ShapeDtypeStruct((TOP_K, T), jnp.float32)],
        compiler_params=pltpu.CompilerParams(dimension_semantics=("arbitrary",), vmem_limit_bytes=VMEM_LIMIT),
        name="moe_route",
    )(x, mod, router_w.T.astype(jnp.bfloat16), router_b.reshape(N_EXPERTS, 1).astype(jnp.float32))


EXPERT_BLOCK = 256


def _expert_kernel(be_ref, nu_ref, x_ref, w1_ref, w3_ref, w2_ref, o_ref):
    i = pl.program_id(0)

    @pl.when(i < nu_ref[0])
    def _():
        x = x_ref[...].astype(jnp.bfloat16)
        bf = lambda ref: ref[0].astype(jnp.bfloat16)
        a = jnp.dot(x, bf(w1_ref), preferred_element_type=jnp.float32)
        b = jnp.dot(x, bf(w3_ref), preferred_element_type=jnp.float32)
        hmid = (a * jax.nn.sigmoid(a) * b).astype(jnp.bfloat16)
        o_ref[...] = jnp.dot(hmid, bf(w2_ref), preferred_element_type=jnp.float32).astype(o_ref.dtype)

    @pl.when(i >= nu_ref[0])
    def _():
        o_ref[...] = jnp.zeros(o_ref.shape, o_ref.dtype)


def expert_ffn(blk_exp, n_used, xb, w1, w3, w2, out_dtype=jnp.float32):
    cap, D = xb.shape
    F = w1.shape[-1]
    n_blk = cap // EXPERT_BLOCK
    return pl.pallas_call(
        _expert_kernel,
        grid_spec=pltpu.PrefetchScalarGridSpec(
            num_scalar_prefetch=2,
            grid=(n_blk,),
            in_specs=[pl.BlockSpec((EXPERT_BLOCK, D), lambda i, be, nu: (i, 0)),
                      pl.BlockSpec((1, D, F), lambda i, be, nu: (be[i], 0, 0)),
                      pl.BlockSpec((1, D, F), lambda i, be, nu: (be[i], 0, 0)),
                      pl.BlockSpec((1, F, D), lambda i, be, nu: (be[i], 0, 0))],
            out_specs=pl.BlockSpec((EXPERT_BLOCK, D), lambda i, be, nu: (i, 0)),
        ),
        out_shape=jax.ShapeDtypeStruct((cap, D), out_dtype),
        compiler_params=pltpu.CompilerParams(dimension_semantics=("arbitrary",), vmem_limit_bytes=VMEM_LIMIT),
        name="expert_ffn",
    )(blk_exp, n_used, xb, w1, w3, w2)


def _gdn_kernel(qf_ref, kf_ref, vf_ref, gf_ref, qb_ref, kb_ref, vb_ref, gb_ref, of_ref, ob_ref, s_ref):
    C = GDN_CHUNK
    nb = qf_ref.shape[0]

    @pl.when(pl.program_id(0) == 0)
    def _():
        s_ref[...] = jnp.zeros(s_ref.shape, jnp.float32)

    bf = lambda t: t.astype(jnp.bfloat16)
    mm = lambda a, b: jnp.dot(bf(a), bf(b), preferred_element_type=jnp.float32)
    nt = lambda a, b: lax.dot_general(bf(a), bf(b), _NT_DIMS, preferred_element_type=jnp.float32)
    tn = lambda a, b: lax.dot_general(bf(a), bf(b), (((0,), (0,)), ((), ())), preferred_element_type=jnp.float32)
    col = lambda t, h: t[:, h:h + 1]
    rel = lax.broadcasted_iota(jnp.int32, (C, C), 0) - lax.broadcasted_iota(jnp.int32, (C, C), 1)

    qs, ks, vs, decay, beta_c, e_gc, e_rest, e_tot, strict = [], [], [], [], [], [], [], [], []
    for bi in range(nb):
        for d, refs in enumerate(((qf_ref, kf_ref, vf_ref, gf_ref), (qb_ref, kb_ref, vb_ref, gb_ref))):
            q_ref, k_ref, v_ref, g_ref = refs
            incl = rel >= 0 if d == 0 else rel <= 0
            g = g_ref[bi]
            gc = jnp.dot(incl.astype(jnp.float32), g, precision=lax.Precision.HIGHEST,
                         preferred_element_type=jnp.float32)
            gc_t = gc.T
            tot = jnp.sum(g, axis=0, keepdims=True)
            eg, er, et = jnp.exp(gc), jnp.exp(tot - gc), jnp.exp(tot)
            q, k, v = q_ref[bi], k_ref[bi], v_ref[bi]
            for h in range(GDN_H):
                sl = slice(h * HEAD_DIM, (h + 1) * HEAD_DIM)
                gl = d * GDN_H + h
                qs.append(q[:, sl])
                ks.append(k[:, sl])
                vs.append(v[:, sl])
                decay.append(jnp.where(incl, jnp.exp(col(gc, gl) - gc_t[gl:gl + 1, :]), 0.0))
                strict.append(rel > 0 if d == 0 else rel < 0)
                beta_c.append(col(g, 2 * GDN_H + gl))
                e_gc.append(col(eg, gl))
                e_rest.append(col(er, gl))
                e_tot.append(col(et, gl))
    chains = range(len(qs))
    kb = [ks[c] * beta_c[c] for c in chains]
    a_mat = [jnp.where(strict[c], nt(kb[c], ks[c]) * decay[c], 0.0) for c in chains]
    intra = [nt(qs[c], ks[c]) * decay[c] for c in chains]
    ri = lax.broadcasted_iota(jnp.int32, (C, C), 0)
    ci = lax.broadcasted_iota(jnp.int32, (C, C), 1)

    def level_mask(s, upper):
        sh = s.bit_length() - 1
        same = (ri >> (sh + 1)) == (ci >> (sh + 1))
        r_hi = ((ri >> sh) & 1) == 1
        c_hi = ((ci >> sh) & 1) == 1
        return same & c_hi & ~r_hi if upper else same & r_hi & ~c_hi

    masks = {s: (level_mask(s, False), level_mask(s, True)) for s in (1, 2, 4, 8, 16, 32)}
    upper = [(c // GDN_H) % 2 == 1 for c in chains]
    eye = (ri == ci).astype(jnp.float32)
    t_inv = [eye - jnp.where(masks[1][upper[c]], a_mat[c], 0.0) for c in chains]
    for s in (2, 4, 8, 16, 32):
        tm = [mm(t_inv[c], jnp.where(masks[s][upper[c]], a_mat[c], 0.0)) for c in chains]
        t_inv = [t_inv[c] - mm(tm[c], t_inv[c]) for c in chains]
    xs = [mm(t_inv[c], jnp.concatenate([vs[c] * beta_c[c], kb[c] * e_gc[c]], axis=1)) for c in chains]
    st = [s_ref[c] for c in chains]
    v_new = [xs[c][:, :HEAD_DIM] - mm(xs[c][:, HEAD_DIM:], st[c]) for c in chains]
    o_st = [mm(qs[c] * e_gc[c], st[c]) for c in chains]
    o_in = [mm(intra[c], v_new[c]) for c in chains]
    s_up = [tn(ks[c] * e_rest[c], v_new[c]) for c in chains]
    for c in chains:
        s_ref[c] = st[c] * e_tot[c] + s_up[c]
    outs = [o_st[c] + o_in[c] for c in chains]
    for bi in range(nb):
        for d, o_ref in enumerate((of_ref, ob_ref)):
            c0 = (bi * 2 + d) * GDN_H
            o_ref[bi] = jnp.concatenate(outs[c0:c0 + GDN_H], axis=1)


def gdn_scan(q, k, v, gates, n_ctx_chunks):
    B, Tt, W = q.shape
    C = GDN_CHUNK
    NC = Tt // C

    def bwd(s):
        return jnp.where(s < n_ctx_chunks, n_ctx_chunks - 1 - s, NC - 1 - (s - n_ctx_chunks))

    tok_f = pl.BlockSpec((B, C, W), lambda s: (0, s, 0))
    tok_b = pl.BlockSpec((B, C, W), lambda s: (0, bwd(s), 0))
    gate_f = pl.BlockSpec((B, C, LANE), lambda s: (0, s, 0))
    gate_b = pl.BlockSpec((B, C, LANE), lambda s: (0, bwd(s), 0))
    return pl.pallas_call(
        _gdn_kernel,
        grid=(NC,),
        in_specs=[tok_f, tok_f, tok_f, gate_f, tok_b, tok_b, tok_b, gate_b],
        out_specs=[tok_f, tok_b],
        out_shape=[jax.ShapeDtypeStruct((B, Tt, W), jnp.float32)] * 2,
        scratch_shapes=[pltpu.VMEM((B * 2 * GDN_H, HEAD_DIM, HEAD_DIM), jnp.float32)],
        compiler_params=pltpu.CompilerParams(dimension_semantics=("arbitrary",), vmem_limit_bytes=VMEM_LIMIT),
        name="gdn_scan",
    )(q, k, v, gates, q, k, v, gates)


def _stream_to_scan_rows(i, n_lat_tiles, lat_tiles, ctx_tiles):
    ic = i - n_lat_tiles
    return (jnp.where(i < n_lat_tiles, i // lat_tiles, ic // ctx_tiles),
            jnp.where(i < n_lat_tiles, ctx_tiles + i % lat_tiles, ic % ctx_tiles), 0)


def _gdn_pre_kernel(xq_ref, xk_ref, xv_ref, pq_ref, pk_ref, pv_ref, nq_ref, nk_ref, nv_ref, gab_ref, cw_ref,
                    al_ref, dtb_ref, hs_ref, q_ref, k_ref, v_ref, gate_ref, *, n_lat_tiles, lat_tiles, ctx_tiles):
    i = pl.program_id(0)
    R = xq_ref.shape[0]
    pos = jnp.where(i < n_lat_tiles, i % lat_tiles, (i - n_lat_tiles) % ctx_tiles)
    last = jnp.where(i < n_lat_tiles, lat_tiles - 1, ctx_tiles - 1)
    rows = lax.broadcasted_iota(jnp.int32, (R, 1), 0)
    f32 = lambda t: t.astype(jnp.float32)

    def conv_silu(x_ref, p_ref, n_ref, part):
        x = f32(x_ref[...])
        before = jnp.where(pos == 0, 0.0, f32(p_ref[7:8, :]))
        after = jnp.where(pos == last, 0.0, f32(n_ref[0:1, :]))
        x_prev = jnp.where(rows == 0, before, pltpu.roll(x, 1, axis=0))
        x_next = jnp.where(rows == R - 1, after, pltpu.roll(x, R - 1, axis=0))
        cw = cw_ref[:, part * MIX_W:(part + 1) * MIX_W]
        u = x_prev * cw[0:1] + x * cw[1:2] + x_next * cw[2:3]
        return u * jax.nn.sigmoid(u)

    def head_sumsq(t):
        sq = t * t
        hi = sq.astype(jnp.bfloat16)
        lo = (sq - f32(hi)).astype(jnp.bfloat16)
        return (jnp.dot(hi, hs_ref[...], preferred_element_type=jnp.float32)
                + jnp.dot(lo, hs_ref[...], preferred_element_type=jnp.float32))

    q = conv_silu(xq_ref, pq_ref, nq_ref, 0)
    k = conv_silu(xk_ref, pk_ref, nk_ref, 1)
    q_ref[0] = q * lax.rsqrt(head_sumsq(q) + RMS_EPS) * HEAD_DIM ** -0.5
    k_ref[0] = k * lax.rsqrt(head_sumsq(k) + RMS_EPS)
    v_ref[0] = conv_silu(xv_ref, pv_ref, nv_ref, 2)
    gab = gab_ref[...]
    t = gab + dtb_ref[...]
    softplus = jnp.maximum(t, 0.0) + jnp.log1p(jnp.exp(-jnp.abs(t)))
    lane = lax.broadcasted_iota(jnp.int32, (1, LANE), 1)
    gate_ref[0] = jnp.where(lane < 2 * GDN_H, -jnp.exp(al_ref[...]) * softplus,
                            jnp.where(lane < 4 * GDN_H, jax.nn.sigmoid(gab), 0.0))


def gdn_pre_scan_inputs(proj_main, gab, conv_w, a_log, dt_bias, B, n, m):
    T = proj_main.shape[0]
    R = MERGE_TILE
    lat_tiles, ctx_tiles = n // R, m // R
    maps = dict(n_lat_tiles=B * lat_tiles, lat_tiles=lat_tiles, ctx_tiles=ctx_tiles)
    c0 = MAIN_OFF[2] // MIX_W
    halo = R // 8
    cur = [pl.BlockSpec((R, MIX_W), functools.partial(lambda i, c: (i, c), c=c0 + p)) for p in range(3)]
    prev = [pl.BlockSpec((8, MIX_W), functools.partial(lambda i, c: (jnp.maximum(i * halo - 1, 0), c), c=c0 + p))
            for p in range(3)]
    nxt = [pl.BlockSpec((8, MIX_W), functools.partial(lambda i, c: (jnp.minimum((i + 1) * halo, T // 8 - 1), c),
                                                      c=c0 + p)) for p in range(3)]
    const2 = lambda i: (0, 0)
    out_rows = lambda i: _stream_to_scan_rows(i, **maps)
    pad16 = lambda t: jnp.pad(t.reshape(1, 2 * GDN_H), ((0, 0), (0, LANE - 2 * GDN_H)))
    head_sum = (jnp.arange(MIX_W)[:, None] // HEAD_DIM == jnp.arange(MIX_W)[None, :] // HEAD_DIM).astype(jnp.bfloat16)
    tok = jax.ShapeDtypeStruct((B, m + n, MIX_W), jnp.float32)
    return pl.pallas_call(
        functools.partial(_gdn_pre_kernel, **maps),
        grid=(T // R,),
        in_specs=cur + prev + nxt + [pl.BlockSpec((R, LANE), lambda i: (i, 0)),
                                     pl.BlockSpec((GDN_CONV, 3 * MIX_W), const2),
                                     pl.BlockSpec((1, LANE), const2), pl.BlockSpec((1, LANE), const2),
                                     pl.BlockSpec((MIX_W, MIX_W), const2)],
        out_specs=[pl.BlockSpec((1, R, MIX_W), out_rows)] * 3 + [pl.BlockSpec((1, R, LANE), out_rows)],
        out_shape=[tok, tok, tok, jax.ShapeDtypeStruct((B, m + n, LANE), jnp.float32)],
        compiler_params=pltpu.CompilerParams(dimension_semantics=("arbitrary",), vmem_limit_bytes=VMEM_LIMIT),
        name="gdn_pre",
    )(*([proj_main] * 9), gab, conv_w.T, pad16(a_log), pad16(dt_bias), head_sum)


def gdn_branch(proj_main, gab, conv_w, a_log, dt_bias, B, n, m):
    q, k, v, gates = gdn_pre_scan_inputs(proj_main, gab, conv_w, a_log, dt_bias, B, n, m)
    return gdn_scan(q, k, v, gates, m // GDN_CHUNK)


FFT_R = 128
FFT_COLS = 4096
FFT_K1_STEP = 8


def _dft_tables(L):
    R = FFT_R
    N = R * R
    half = L // R
    idx = jnp.arange(R, dtype=jnp.int32)
    ang1 = (-2.0 * math.pi / R) * ((idx[:, None] * idx[None, :]) % R).astype(jnp.float32)
    fr, fi = jnp.cos(ang1), jnp.sin(ang1)
    blk = lambda re, im: jnp.concatenate([jnp.concatenate([re, -im], axis=1),
                                          jnp.concatenate([im, re], axis=1)], axis=0)
    m1 = blk(fr[:, :half], fi[:, :half])
    m1_real = jnp.concatenate([fr, fi], axis=0)
    m3 = blk(fr.T[:half], -fi.T[:half]) * (1.0 / N)
    k = idx[:, None, None] + R * idx[None, :, None]
    ang2 = (-2.0 * math.pi / N) * ((idx[None, None, :] * k) % N).astype(jnp.float32)
    gr, gi = jnp.cos(ang2), jnp.sin(ang2)
    g_fwd = jnp.concatenate([jnp.concatenate([gr, -gi], axis=2),
                             jnp.concatenate([gi, gr], axis=2)], axis=1)
    g_inv = jnp.swapaxes(g_fwd, 1, 2)
    bf = lambda t: t.astype(jnp.bfloat16)
    return bf(m1), bf(m1_real), bf(m3), bf(g_fwd), bf(g_inv)


def _colmm_kernel(m_ref, x_ref, o_ref):
    o_ref[...] = jnp.dot(m_ref[...], x_ref[...].astype(jnp.bfloat16),
                         preferred_element_type=jnp.float32).astype(o_ref.dtype)


def colmm(mat, x, out_dtype=jnp.bfloat16):
    M, K = mat.shape
    n_cols = x.shape[1]
    return pl.pallas_call(
        _colmm_kernel,
        grid=(n_cols // FFT_COLS,),
        in_specs=[pl.BlockSpec((M, K), lambda j: (0, 0)), pl.BlockSpec((K, FFT_COLS), lambda j: (0, j))],
        out_specs=pl.BlockSpec((M, FFT_COLS), lambda j: (0, j)),
        out_shape=jax.ShapeDtypeStruct((M, n_cols), out_dtype),
        compiler_params=pltpu.CompilerParams(dimension_semantics=("arbitrary",), vmem_limit_bytes=VMEM_LIMIT),
        name="hyena_colmm",
    )(mat, x)


def _colmm_gate_kernel(m_ref, r_ref, z_ref, x_ref, skip_ref, o_ref):
    y = jnp.dot(m_ref[...], r_ref[...], preferred_element_type=jnp.float32)
    o_ref[...] = x_ref[...] * (y + z_ref[...] * skip_ref[...])


def colmm_gate(mat, r, z, gate, skip_cols):
    M, K = mat.shape
    n_cols = r.shape[1]
    col = lambda j: (0, j)
    return pl.pallas_call(
        _colmm_gate_kernel,
        grid=(n_cols // FFT_COLS,),
        in_specs=[pl.BlockSpec((M, K), lambda j: (0, 0)), pl.BlockSpec((K, FFT_COLS), col),
                  pl.BlockSpec((M, FFT_COLS), col), pl.BlockSpec((M, FFT_COLS), col),
                  pl.BlockSpec((1, FFT_COLS), col)],
        out_specs=pl.BlockSpec((M, FFT_COLS), col),
        out_shape=jax.ShapeDtypeStruct((M, n_cols), jnp.float32),
        compiler_params=pltpu.CompilerParams(dimension_semantics=("arbitrary",), vmem_limit_bytes=VMEM_LIMIT),
        name="hyena_colmm_gate",
    )(mat, r, z, gate, skip_cols)


def _spectrum_kernel(p_ref, g_ref, o_ref):
    R = FFT_R
    for j in range(FFT_K1_STEP):
        p = jnp.concatenate([p_ref[0, j], p_ref[1, j]], axis=0)
        q = jnp.dot(g_ref[j], p, preferred_element_type=jnp.float32)
        o_ref[0, j] = q[:R]
        o_ref[1, j] = q[R:]


def _freq_kernel(p_ref, g_ref, gi_ref, h_ref, o_ref):
    R = FFT_R
    for j in range(FFT_K1_STEP):
        p = jnp.concatenate([p_ref[0, j], p_ref[1, j]], axis=0)
        q = jnp.dot(g_ref[j], p, preferred_element_type=jnp.float32)
        qr, qi = q[:R], q[R:]
        hr, hi = h_ref[0, j], h_ref[1, j]
        y = jnp.concatenate([qr * hr - qi * hi, qr * hi + qi * hr], axis=0).astype(jnp.bfloat16)
        r = jnp.dot(gi_ref[j], y, preferred_element_type=jnp.float32).astype(o_ref.dtype)
        o_ref[0, j] = r[:R]
        o_ref[1, j] = r[R:]


def hyena_spectrum(p, g_fwd):
    C = p.shape[-1]
    R, S = FFT_R, FFT_K1_STEP
    blk = pl.BlockSpec((2, S, R, LANE * 4), lambda i, c: (0, i, 0, c))
    return pl.pallas_call(
        _spectrum_kernel,
        grid=(R // S, C // (LANE * 4)),
        in_specs=[blk, pl.BlockSpec((S, 2 * R, 2 * R), lambda i, c: (i, 0, 0))],
        out_specs=blk,
        out_shape=jax.ShapeDtypeStruct((2, R, R, C), jnp.float32),
        compiler_params=pltpu.CompilerParams(dimension_semantics=("arbitrary",) * 2, vmem_limit_bytes=VMEM_LIMIT),
        name="hyena_spectrum",
    )(p, g_fwd)


def hyena_freq(p, g_fwd, g_inv, spec, order):
    C = p.shape[-1]
    R, S = FFT_R, FFT_K1_STEP
    blk = pl.BlockSpec((2, S, R, C), lambda i: (0, i, 0, 0))
    gspec = pl.BlockSpec((S, 2 * R, 2 * R), lambda i: (i, 0, 0))
    return pl.pallas_call(
        _freq_kernel,
        grid=(R // S,),
        in_specs=[blk, gspec, gspec, pl.BlockSpec((2, S, R, C), lambda i: (0, i, 0, order))],
        out_specs=blk,
        out_shape=jax.ShapeDtypeStruct((2, R, R, C), jnp.bfloat16),
        compiler_params=pltpu.CompilerParams(dimension_semantics=("arbitrary",), vmem_limit_bytes=VMEM_LIMIT),
        name="hyena_freq",
    )(p, g_fwd, g_inv, spec)


def hyena_taps(L, w1, b1, w2, b2, w3, freq, deltas):
    f32 = jnp.float32
    pos = jnp.arange(L, dtype=f32)
    pos_b = L - pos
    bands = jnp.linspace(1e-4, HY_BANDS - 1, HY_BANDS, dtype=f32)[None, :]

    def mlp(j, w3_dir, delta_dir):
        t = (j / (L - 1))[:, None]
        ang = (2.0 * math.pi / L) * j[:, None]
        feats = jnp.concatenate([t, jnp.cos(bands * ang), -jnp.sin(bands * ang)], axis=-1)
        h = jnp.sin(freq * (feats @ w1 + b1))
        h = jnp.sin(freq * (h @ w2 + b2))
        h = h @ w3_dir
        return h * (jnp.exp(-t * jnp.abs(delta_dir).reshape(1, -1)) + HY_MOD_SHIFT)

    w3d = w3.reshape(w3.shape[0], 2, HY_ORDER * HY_W)
    h_f = mlp(pos, w3d[:, 0], deltas[0])
    h_b = mlp(pos_b, w3d[:, 1], deltas[1])
    h_b = jnp.where((pos > 0)[:, None], h_b, 0.0)
    taps = jnp.concatenate([h_f, h_b], axis=0)
    return taps / jnp.sum(jnp.abs(taps), axis=0, keepdims=True)


def hyena_latent(proj, conv_w, skip, filt, tables):
    B, L, _ = proj.shape
    assert B == 2 and L % FFT_R == 0 and 2 * L == FFT_R * FFT_R
    R, C = FFT_R, HY_W
    m1, m1_real, m3, g_fwd, g_inv = tables
    taps = hyena_taps(L, *filt)
    spec = hyena_spectrum(colmm(m1_real, taps.reshape(R, R * HY_ORDER * C)).reshape(2, R, R, HY_ORDER * C), g_fwd)
    u = centred_dwconv(proj, conv_w)
    v, x1, x2 = jnp.split(u, 3, axis=-1)
    rows = B * (L // R)
    z = v.reshape(rows, R * C)
    for o, gate in enumerate((x1, x2)):
        p = colmm(m1, z).reshape(2, R, R, C)
        r = hyena_freq(p, g_fwd, g_inv, spec, o).reshape(2 * R, R * C)
        z = colmm_gate(m3, r, z, gate.reshape(rows, R * C), jnp.tile(skip[o], R).reshape(1, R * C))
    return z.reshape(B, L, C)


def _combine_ln_kernel(dest_ref, x_ref, w_ref, mod_ref, lng_ref, lnb_ref, yb_ref, o_ref, buf_ref, sem, *, alpha):
    R = x_ref.shape[0]

    def row_copy(k, r):
        row = dest_ref[0, 0, k * R + r]
        return pltpu.make_async_copy(yb_ref.at[pl.ds(row, 1)], buf_ref.at[k, pl.ds(r, 1)], sem.at[k])

    def issue(r, carry):
        row_copy(0, r).start(priority=0)
        row_copy(1, r).start(priority=1)
        return carry

    def drain(r, carry):
        row_copy(0, r).wait()
        row_copy(1, r).wait()
        return carry

    lax.fori_loop(0, R, issue, 0, unroll=8)
    lax.fori_loop(0, R, drain, 0, unroll=8)
    y = w_ref[:, 0:1] * buf_ref[0] + w_ref[:, 1:2] * buf_ref[1]
    z = alpha * x_ref[...] + mod_ref[0, 5:6, :] * y
    o_ref[...] = _layer_norm_rows(z) * lng_ref[...] + lnb_ref[...]


def moe_combine_ln(x, yb, dest, wts, mod, ln_g, ln_b, alpha):
    T, D = x.shape
    R = MERGE_TILE
    dest_tiles = jnp.swapaxes(dest.reshape(T // R, R, TOP_K), 1, 2).reshape(T // R, 1, TOP_K * R)
    row = lambda i: (i, 0)
    return pl.pallas_call(
        functools.partial(_combine_ln_kernel, alpha=alpha),
        grid=(T // R,),
        in_specs=[pl.BlockSpec((1, 1, TOP_K * R), lambda i: (i, 0, 0), memory_space=pltpu.SMEM),
                  pl.BlockSpec((R, D), row),
                  pl.BlockSpec((R, TOP_K), row),
                  pl.BlockSpec((1, 6, D), lambda i: (i, 0, 0)),
                  pl.BlockSpec((1, D), lambda i: (0, 0)),
                  pl.BlockSpec((1, D), lambda i: (0, 0)),
                  pl.BlockSpec(memory_space=pl.ANY)],
        out_specs=pl.BlockSpec((R, D), row),
        out_shape=jax.ShapeDtypeStruct((T, D), jnp.float32),
        scratch_shapes=[pltpu.VMEM((TOP_K, R, D), jnp.float32), pltpu.SemaphoreType.DMA((TOP_K,))],
        compiler_params=pltpu.CompilerParams(dimension_semantics=("arbitrary",), vmem_limit_bytes=VMEM_LIMIT),
        name="moe_combine_ln",
    )(dest_tiles, x, wts.T, mod, ln_g.reshape(1, D), ln_b.reshape(1, D), yb)


def moe_dispatch(idx, n_tok):
    n_slot = n_tok * TOP_K
    e_flat = idx.T.reshape(-1)
    onehot = (e_flat[:, None] == jnp.arange(N_EXPERTS, dtype=jnp.int32)[None, :]).astype(jnp.int32)
    csum = jnp.cumsum(onehot, axis=0)
    rank = jnp.sum(csum * onehot, axis=1) - 1
    counts = csum[-1]
    padded = (counts + EXPERT_BLOCK - 1) // EXPERT_BLOCK * EXPERT_BLOCK
    pend = jnp.cumsum(padded)
    pstart = pend - padded
    dest = jnp.sum(onehot * pstart[None, :], axis=1) + rank
    n_blk = -(-n_slot // EXPERT_BLOCK) + N_EXPERTS
    cap = n_blk * EXPERT_BLOCK
    t_flat = jnp.arange(n_slot, dtype=jnp.int32) // TOP_K
    buf_tok = jnp.zeros((cap,), jnp.int32).at[dest].set(t_flat)
    blk_exp = jnp.minimum(jnp.searchsorted(pend, jnp.arange(n_blk, dtype=jnp.int32) * EXPERT_BLOCK, side='right'),
                          N_EXPERTS - 1).astype(jnp.int32)
    n_used = (pend[-1] // EXPERT_BLOCK).astype(jnp.int32).reshape(1)
    return buf_tok, dest, blk_exp, n_used


def rms_norm(x, w):
    return x * lax.rsqrt(jnp.mean(jnp.square(x), -1, keepdims=True) + RMS_EPS) * w


def centred_dwconv(u, w):
    K = w.shape[-1]
    T = u.shape[1]
    up = jnp.pad(u, ((0, 0), (K // 2, K // 2), (0, 0)))
    out = up[:, 0:T, :] * w[:, 0]
    for j in range(1, K):
        out = out + up[:, j:j + T, :] * w[:, j]
    return out


def hyena_filters(L, w1, b1, w2, b2, w3, freq, deltas):
    f32 = jnp.float32
    t = jnp.linspace(0.0, 1.0, L, dtype=f32)[:, None]
    ang = 2.0 * math.pi * jnp.arange(L, dtype=f32)[:, None] / L
    bands = jnp.linspace(1e-4, HY_BANDS - 1, HY_BANDS, dtype=f32)[None, :]
    feats = jnp.concatenate([t, jnp.cos(bands * ang), -jnp.sin(bands * ang)], axis=-1)
    h = jnp.sin(freq * (feats @ w1 + b1))
    h = jnp.sin(freq * (h @ w2 + b2))
    h = (h @ w3).reshape(L, 2, HY_ORDER, HY_W)
    window = jnp.exp(-t[:, :, None, None] * jnp.abs(deltas)) + HY_MOD_SHIFT
    h = h * window
    h_fwd, h_bwd = h[:, 0], h[:, 1]
    taps = jnp.concatenate([h_fwd, jnp.zeros_like(h_fwd[:1]), jnp.flip(h_bwd[1:], 0)], axis=0)
    taps = taps / jnp.sum(jnp.abs(taps), axis=0, keepdims=True)
    return jnp.fft.rfft(taps, axis=0)


def hyena_mix(proj, conv_w, skip, filt_f):
    L = proj.shape[1]
    u = centred_dwconv(proj, conv_w)
    v, x1, x2 = jnp.split(u, 3, axis=-1)
    z = v
    for o, gate in enumerate((x1, x2)):
        zf = jnp.fft.rfft(z, n=2 * L, axis=1)
        y = jnp.fft.irfft(zf * filt_f[:, o], n=2 * L, axis=1)[:, :L]
        z = gate * (y + z * skip[o])
    return z


def axial_rope_angles(n):
    t = jnp.arange(n)
    row = (t // GRID_W).astype(jnp.float32)
    col = (t % GRID_W).astype(jnp.float32)
    nf = HEAD_DIM // 4
    inv = ROPE_THETA ** (-jnp.arange(nf, dtype=jnp.float32) / nf)
    return row[:, None] * inv, col[:, None] * inv


def diff_attend(q, k, v, lam):
    s = jnp.einsum('bqhcd,bkhcd->bhcqk', q, k) * HEAD_DIM ** -0.5
    p = jax.nn.softmax(s, axis=-1)
    a = p[:, :, 0] - lam * p[:, :, 1]
    return jnp.einsum('bhqk,bkhd->bqhd', a, v)


def diff_finish(o, norm_w, lam_init):
    B, T = o.shape[:2]
    return (rms_norm(o, norm_w) * (1.0 - lam_init)).reshape(B, T, DIFF_H * DIFF_VD)


def split_diff(p):
    B, T, _ = p.shape
    q, k, v = jnp.split(p, 3, axis=-1)
    return (q.reshape(B, T, DIFF_H, 2, HEAD_DIM), k.reshape(B, T, DIFF_H, 2, HEAD_DIM),
            v.reshape(B, T, DIFF_H, DIFF_VD))


def dense_attend(q, k, v):
    s = jnp.einsum('bqhd,bkhd->bhqk', q, k) * HEAD_DIM ** -0.5
    p = jax.nn.softmax(s, axis=-1)
    return jnp.einsum('bhqk,bkhd->bqhd', p, v)


def _diff_pre_kernel(xq_ref, xk_ref, xv_ref, cos_ref, sin_ref, q_ref, k_ref, v_ref, *, n_lat_tiles, q_scale):
    is_lat = pl.program_id(0) < n_lat_tiles
    cos = jnp.tile(jnp.where(is_lat, cos_ref[...], 1.0), (1, DIFF_H))
    sin = jnp.tile(jnp.where(is_lat, sin_ref[...], 0.0), (1, DIFF_H))
    lane = lax.broadcasted_iota(jnp.int32, (1, MIX_W), 1)
    first_half = (lane % (HEAD_DIM // 2)) < HEAD_DIM // 4

    def rope(x):
        x = x.astype(jnp.float32)
        partner = jnp.where(first_half, pltpu.roll(x, MIX_W - HEAD_DIM // 4, axis=1),
                            pltpu.roll(x, HEAD_DIM // 4, axis=1))
        return x * cos + partner * sin

    q_ref[...] = (rope(xq_ref[...]) * q_scale).astype(q_ref.dtype)
    k_ref[0] = rope(xk_ref[...]).astype(k_ref.dtype)
    v_ref[0] = xv_ref[...]


def diff_pre(proj_main, cos_t, sin_t, B, n, m):
    T = proj_main.shape[0]
    R = MERGE_TILE
    lat_tiles, ctx_tiles = n // R, m // R
    n_lat_tiles = B * lat_tiles
    c0 = MAIN_OFF[4] // MIX_W
    cur = [pl.BlockSpec((R, MIX_W), functools.partial(lambda i, c: (i, c), c=c0 + p)) for p in range(3)]
    table = pl.BlockSpec((R, DIFF_VD), lambda i: (jnp.where(i < n_lat_tiles, i % lat_tiles, 0), 0))

    def key_rows(i):
        ic = i - n_lat_tiles
        return (jnp.where(i < n_lat_tiles, i // lat_tiles, ic // ctx_tiles),
                jnp.where(i < n_lat_tiles, i % lat_tiles, lat_tiles + ic % ctx_tiles), 0)

    kv = jax.ShapeDtypeStruct((B, n + m, MIX_W), jnp.bfloat16)
    return pl.pallas_call(
        functools.partial(_diff_pre_kernel, n_lat_tiles=n_lat_tiles,
                          q_scale=HEAD_DIM ** -0.5 * math.log2(math.e)),
        grid=(T // R,),
        in_specs=cur + [table, table],
        out_specs=[pl.BlockSpec((R, MIX_W), lambda i: (i, 0)),
                   pl.BlockSpec((1, R, MIX_W), key_rows), pl.BlockSpec((1, R, MIX_W), key_rows)],
        out_shape=[jax.ShapeDtypeStruct((T, MIX_W), jnp.bfloat16), kv, kv],
        compiler_params=pltpu.CompilerParams(dimension_semantics=("arbitrary",), vmem_limit_bytes=VMEM_LIMIT),
        name="diff_pre",
    )(proj_main, proj_main, proj_main, cos_t, sin_t)


def rope_tables(ang_r, ang_c):
    cos = jnp.concatenate([jnp.cos(ang_r)] * 2 + [jnp.cos(ang_c)] * 2, axis=-1)
    sin = jnp.concatenate([-jnp.sin(ang_r), jnp.sin(ang_r), -jnp.sin(ang_c), jnp.sin(ang_c)], axis=-1)
    return jnp.tile(cos, (1, 2)), jnp.tile(sin, (1, 2))


def diff_branch(proj_main, cos_t, sin_t, lam, norm_w, lam_init, B, n, m):
    q, k_all, v_all = diff_pre(proj_main, cos_t, sin_t, B, n, m)
    nk = n + m
    k_all = jnp.transpose(k_all.reshape(B, nk, DIFF_H, 2, HEAD_DIM), (0, 2, 3, 1, 4))
    vt_all = jnp.transpose(v_all.reshape(B, nk // DIFF_TK, DIFF_TK, DIFF_H, DIFF_VD), (0, 3, 1, 4, 2))
    return diff_attention(q[:B * n].reshape(B, n, MIX_W), k_all, vt_all, lam, norm_w, 1.0 - lam_init)


def _split_w_in(w):
    segs = jnp.split(w, SPLIT_IDX, axis=-1)
    ab = jnp.concatenate([segs[3], segs[4]], axis=-1)
    ab = jnp.pad(ab, ((0, 0), (0, LANE - ab.shape[-1])))
    main = jnp.concatenate([segs[7], segs[0], segs[1], segs[2], segs[5], segs[6]], axis=-1)
    return main.astype(jnp.bfloat16), ab.astype(jnp.bfloat16)


MAIN_SIZES = (N_BRANCH * D_MODEL, 3 * HY_W, 3 * MIX_W, MIX_W, 3 * MIX_W, 3 * MIX_W)
MAIN_OFF = tuple(sum(MAIN_SIZES[:i]) for i in range(len(MAIN_SIZES) + 1))
IN_TN = 768


def kernel(x, c, ctx, c_ctx, w_mod, b_mod, w_in, hy_conv, hy_w1, hy_b1, hy_w2, hy_b2, hy_w3, hy_freq,
           hy_deltas, hy_skip, gdn_conv, gdn_a_log, gdn_dt_bias, gdn_norm, diff_lam, diff_norm, na_rpb,
           branch_proj, w_out, ln_g, ln_b, router_w, router_b, exp_w1, exp_w3, exp_w2):
    B, n, D = x.shape
    m = ctx.shape[1]
    depth = w_mod.shape[0]
    dn_alpha = (2 * depth) ** 0.25
    ang_r, ang_c = axial_rope_angles(n)
    bf16 = jnp.bfloat16
    n_lat = B * n
    dft_tables = _dft_tables(n)
    cos_t, sin_t = rope_tables(ang_r, ang_c)
    tok = jnp.concatenate([x.reshape(n_lat, D), ctx.reshape(B * m, D)], axis=0)
    for l in range(depth):
        ctx_out = l < depth - 1
        mx = jax.nn.silu(c) @ w_mod[l] + b_mod[l]
        mc = jax.nn.silu(c_ctx) @ w_mod[l] + b_mod[l]
        one_plus = jnp.array([0.0, 1.0, 0.0, 0.0, 1.0, 0.0], jnp.float32)[:, None]
        mod_x = mx.reshape(B, 6, D) + one_plus
        mod_c = mc.reshape(1, 6, D) + one_plus
        mod = jnp.concatenate([jnp.repeat(mod_x, n // MERGE_TILE, axis=0),
                               jnp.repeat(mod_c, B * m // MERGE_TILE, axis=0)], axis=0)

        proj, gab = ln_mod_matmul(tok, mod, *_split_w_in(w_in[l]), IN_TN)

        def seg(i, latent):
            rows = slice(0, n_lat) if latent else slice(n_lat, None)
            return proj[rows, MAIN_OFF[i]:MAIN_OFF[i + 1]].reshape(B, -1, MAIN_SIZES[i]).astype(jnp.float32)

        hy_x, gqkv_x, gz_x, dqkv_x, nqkv_x = [seg(i, True) for i in range(1, 6)]
        hy_c, gqkv_c, gz_c, dqkv_c, nqkv_c = [seg(i, False) for i in range(1, 6)]
        gab_x, gab_c = gab[:n_lat].reshape(B, n, LANE), gab[n_lat:].reshape(B, m, LANE)
        ga_x, gb_x = gab_x[..., :2 * GDN_H], gab_x[..., 2 * GDN_H:4 * GDN_H]
        ga_c, gb_c = gab_c[..., :2 * GDN_H], gab_c[..., 2 * GDN_H:4 * GDN_H]
        filt = (hy_w1[l], hy_b1[l], hy_w2[l], hy_b2[l], hy_w3[l], hy_freq[l], hy_deltas[l])

        ya_x = hyena_latent(hy_x, hy_conv[l], hy_skip[l], filt, dft_tables)

        o_f, o_b = gdn_branch(proj, gab, gdn_conv[l], gdn_a_log[l], gdn_dt_bias[l], B, n, m)

        lq1, lk1, lq2, lk2 = diff_lam[l]
        lam_init = 0.8 - 0.6 * math.exp(-0.3 * l)
        lam = jnp.exp(jnp.sum(lq1 * lk1)) - jnp.exp(jnp.sum(lq2 * lk2)) + lam_init
        yc_x = diff_branch(proj, cos_t, sin_t, lam, diff_norm[l], lam_init, B, n, m)
        yd_x = na_attention(proj, na_bias_tables(na_rpb[l], n // GRID_W), B, n, m)

        ys = [t.reshape(n_lat, MIX_W) for t in (ya_x, yc_x, yd_x)]
        if ctx_out:
            dq_c, dk_c, dv_c = split_diff(dqkv_c)
            nq_c, nk_c, nv_c = [t.reshape(B, m, NA_H, HEAD_DIM) for t in jnp.split(nqkv_c, 3, axis=-1)]
            ya_c = hyena_mix(hy_c, hy_conv[l], hy_skip[l], hyena_filters(m, *filt))
            yc_c = diff_finish(diff_attend(dq_c, dk_c, dv_c, lam), diff_norm[l], lam_init)
            yd_c = dense_attend(nq_c, nk_c, nv_c)
            ys_c = [t.reshape(B * m, MIX_W) for t in (ya_c, yc_c, yd_c)]
            ys = [jnp.concatenate([a, b], axis=0) for a, b in zip(ys, ys_c)]
        else:
            tok = tok[:n_lat]
        tok = merge_residual_ln(tok, ys[0], o_f, o_b, ys[1], ys[2], proj, mod, branch_proj[l].astype(bf16),
                                w_out[l].astype(bf16), gdn_norm[l], ln_g[l, 0], ln_b[l, 0], dn_alpha, n, m)

        n_tok = tok.shape[0]
        h2, idx, wts = moe_route(tok, mod, router_w, router_b)
        buf_tok, dest, blk_exp, n_used = moe_dispatch(idx, n_tok)
        yb = expert_ffn(blk_exp, n_used, jnp.take(h2, buf_tok, axis=0), exp_w1[l], exp_w3[l], exp_w2[l])
        tok = moe_combine_ln(tok, yb, dest.reshape(n_tok, TOP_K), wts, mod, ln_g[l, 1], ln_b[l, 1], dn_alpha)
    return tok[:n_lat].reshape(B, n, D)
```
